```python
import math
import jax, jax.numpy as jnp
from jax import lax
import numpy as np

D_MODEL = 1024
BATCH = 2
SEQ = 8192
DEPTH = 1

HEAD_DIM = 64
N_Q_HEADS = 8
N_KV_HEADS = 2
GROUP = N_Q_HEADS // N_KV_HEADS
ATTN_WIDTH = N_Q_HEADS * HEAD_DIM
KV_WIDTH = N_KV_HEADS * HEAD_DIM
WINDOW = 128
BLOCK = 128
ROT_DIM = HEAD_DIM // 4
ROPE_THETA = 500000.0
CONV_WIDTH = D_MODEL - ATTN_WIDTH
CONV_K = 3
PROJ_WIDTH = ATTN_WIDTH + 2 * KV_WIDTH + 3 * CONV_WIDTH
MIX_WIDTH = ATTN_WIDTH + CONV_WIDTH
N_KEYS = 128
N_EXPERTS = N_KEYS * N_KEYS
PEER_HEADS = 8
PEER_QDIM = 256
PEER_HALF = PEER_QDIM // 2
PEER_TOPK = 16
PEER_CHUNK = 128
DN_ALPHA = (2.0 * DEPTH) ** 0.25
DN_BETA = (8.0 * DEPTH) ** -0.25
LN_EPS = 1e-5

kernel_name = "hymba_swa_sink_shortconv_peer_deepnorm"


def layer_norm(x, g, b):
    xf = x.astype(jnp.float32)
    mu = jnp.mean(xf, axis=-1, keepdims=True)
    xc = xf - mu
    var = jnp.mean(xc * xc, axis=-1, keepdims=True)
    return (xc * lax.rsqrt(var + LN_EPS) * g + b).astype(x.dtype)


def rope_tables(positions, dtype):
    inv_freq = ROPE_THETA ** (-jnp.arange(0, ROT_DIM, 2, dtype=jnp.float32) / ROT_DIM)
    ang = positions.astype(jnp.float32)[..., None] * inv_freq
    return jnp.cos(ang)[:, :, None, :].astype(dtype), jnp.sin(ang)[:, :, None, :].astype(dtype)


def partial_rope(t, cos, sin):
    r1 = t[..., :ROT_DIM // 2]
    r2 = t[..., ROT_DIM // 2:ROT_DIM]
    return jnp.concatenate([r1 * cos - r2 * sin, r2 * cos + r1 * sin, t[..., ROT_DIM:]], axis=-1)


def sliding_window_attention_sinks(q, k, v, sinks):
    B, S = q.shape[0], q.shape[1]
    nb = S // BLOCK
    qb = q.reshape(B, nb, BLOCK, N_KV_HEADS, GROUP, HEAD_DIM)
    pad = ((0, 0), (BLOCK, 0), (0, 0), (0, 0))
    kp = jnp.pad(k, pad).reshape(B, nb + 1, BLOCK, N_KV_HEADS, HEAD_DIM)
    vp = jnp.pad(v, pad).reshape(B, nb + 1, BLOCK, N_KV_HEADS, HEAD_DIM)
    kc = jnp.concatenate([kp[:, :-1], kp[:, 1:]], axis=2)
    vc = jnp.concatenate([vp[:, :-1], vp[:, 1:]], axis=2)
    s = jnp.einsum('bnqhgd,bnkhd->bnhgqk', qb, kc).astype(jnp.float32) * (HEAD_DIM ** -0.5)
    qi = jnp.arange(BLOCK)[:, None]
    kj = jnp.arange(2 * BLOCK)[None, :]
    diff = qi + BLOCK - kj
    band = (diff >= 0) & (diff < WINDOW)
    key_abs = jnp.arange(nb)[:, None, None] * BLOCK + kj[None] - BLOCK
    mask = band[None] & (key_abs >= 0)
    s = jnp.where(mask[None, :, None, None], s, -jnp.inf)
    sink = sinks.astype(jnp.float32).reshape(N_KV_HEADS, GROUP)[None, None, :, :, None, None]
    m = jnp.maximum(jnp.max(s, axis=-1, keepdims=True), sink)
    p = jnp.exp(s - m)
    p = p / (jnp.sum(p, axis=-1, keepdims=True) + jnp.exp(sink - m))
    o = jnp.einsum('bnhgqk,bnkhd->bnqhgd', p.astype(v.dtype), vc)
    return o.reshape(B, S, ATTN_WIDTH)


def causal_depthwise_conv3(u, w):
    up = jnp.pad(u, ((0, 0), (CONV_K - 1, 0), (0, 0)))
    return up[:, :-2] * w[0] + up[:, 1:-1] * w[1] + up[:, 2:] * w[2]


def hybrid_mixer(h, cos, sin, w_in, b_in, attn_sinks, conv_w, w_out, b_out):
    B, S, _ = h.shape
    proj = h @ w_in + b_in
    o1 = ATTN_WIDTH
    o2 = o1 + KV_WIDTH
    o3 = o2 + KV_WIDTH
    o4 = o3 + CONV_WIDTH
    o5 = o4 + CONV_WIDTH
    q = proj[..., :o1].reshape(B, S, N_Q_HEADS, HEAD_DIM)
    k = proj[..., o1:o2].reshape(B, S, N_KV_HEADS, HEAD_DIM)
    v = proj[..., o2:o3].reshape(B, S, N_KV_HEADS, HEAD_DIM)
    gate_b = proj[..., o3:o4]
    gate_c = proj[..., o4:o5]
    xc = proj[..., o5:]
    q = partial_rope(q, cos, sin)
    k = partial_rope(k, cos, sin)
    y_attn = sliding_window_attention_sinks(q, k, v, attn_sinks)
    y_conv = gate_b * causal_depthwise_conv3(gate_c * xc, conv_w)
    y = jnp.concatenate([y_attn, y_conv], axis=-1)
    return y @ w_out + b_out


def peer_ffn(h, w_pq, sub_keys1, sub_keys2, u_experts, v_experts):
    B, S, D = h.shape
    T = B * S
    xt = h.reshape(T, D)
    q = (xt @ w_pq).reshape(T, PEER_HEADS, PEER_QDIM)
    s1 = jnp.einsum('thd,nd->thn', q[..., :PEER_HALF], sub_keys1)
    s2 = jnp.einsum('thd,nd->thn', q[..., PEER_HALF:], sub_keys2)
    v1, i1 = lax.top_k(s1, PEER_TOPK)
    v2, i2 = lax.top_k(s2, PEER_TOPK)
    cand = (v1[..., :, None] + v2[..., None, :]).reshape(T, PEER_HEADS, PEER_TOPK * PEER_TOPK)
    sc, pos = lax.top_k(cand, PEER_TOPK)
    e = (jnp.take_along_axis(i1, pos // PEER_TOPK, axis=-1) * N_KEYS
         + jnp.take_along_axis(i2, pos % PEER_TOPK, axis=-1))
    g = jax.nn.softmax(sc.astype(jnp.float32), axis=-1).astype(h.dtype)
    nc = T // PEER_CHUNK
    kk = PEER_HEADS * PEER_TOPK
    xs = xt.reshape(nc, PEER_CHUNK, D)
    es = e.reshape(nc, PEER_CHUNK, kk)
    gs = g.reshape(nc, PEER_CHUNK, kk)

    def chunk(args):
        xc, ec, gc = args
        u = jnp.take(u_experts, ec, axis=0)
        a = jax.nn.gelu(jnp.einsum('tkd,td->tk', u, xc), approximate=False)
        vv = jnp.take(v_experts, ec, axis=0)
        return jnp.einsum('tk,tkd->td', gc * a, vv)

    out = lax.map(chunk, (xs, es, gs))
    return out.reshape(B, S, D)


def setup_inputs(seed: int = 0) -> dict:
    key = jax.random.key(seed)
    ks = jax.random.split(key, 20)
    f32 = jnp.float32
    nrm = lambda k, shape, scale: jax.random.normal(k, shape, f32) * scale
    x = jax.random.normal(ks[0], (BATCH, SEQ, D_MODEL), f32)
    offs = jax.random.randint(ks[1], (BATCH, 1), 0, 1024, dtype=jnp.int32)
    positions = (jnp.arange(SEQ, dtype=jnp.int32)[None, :] + offs).astype(jnp.int32)
    return {
        "x": x,
        "positions": positions,
        "w_in": nrm(ks[2], (DEPTH, D_MODEL, PROJ_WIDTH), D_MODEL ** -0.5),
        "b_in": nrm(ks[3], (DEPTH, PROJ_WIDTH), 0.02),
        "attn_sinks": nrm(ks[4], (DEPTH, N_Q_HEADS), 0.5),
        "conv_w": nrm(ks[5], (DEPTH, CONV_K, CONV_WIDTH), CONV_K ** -0.5),
        "w_out": nrm(ks[6], (DEPTH, MIX_WIDTH, D_MODEL), MIX_WIDTH ** -0.5 * DN_BETA),
        "b_out": nrm(ks[7], (DEPTH, D_MODEL), 0.02),
        "ln1_g": 1.0 + nrm(ks[8], (DEPTH, D_MODEL), 0.02),
        "ln1_b": nrm(ks[9], (DEPTH, D_MODEL), 0.02),
        "w_pq": nrm(ks[10], (DEPTH, D_MODEL, PEER_HEADS * PEER_QDIM), D_MODEL ** -0.5),
        "sub_keys1": nrm(ks[11], (DEPTH, N_KEYS, PEER_HALF), PEER_HALF ** -0.5),
        "sub_keys2": nrm(ks[12], (DEPTH, N_KEYS, PEER_HALF), PEER_HALF ** -0.5),
        "u_experts": nrm(ks[13], (DEPTH, N_EXPERTS, D_MODEL), D_MODEL ** -0.5),
        "v_experts": nrm(ks[14], (DEPTH, N_EXPERTS, D_MODEL), (PEER_HEADS * PEER_TOPK) ** -0.5 * DN_BETA),
        "ln2_g": 1.0 + nrm(ks[15], (DEPTH, D_MODEL), 0.02),
        "ln2_b": nrm(ks[16], (DEPTH, D_MODEL), 0.02),
    }


def reference(x, positions, w_in, b_in, attn_sinks, conv_w, w_out, b_out, ln1_g, ln1_b,
              w_pq, sub_keys1, sub_keys2, u_experts, v_experts, ln2_g, ln2_b):
    cos, sin = rope_tables(positions, x.dtype)
    h = x
    for l in range(DEPTH):
        mix = hybrid_mixer(h, cos, sin, w_in[l], b_in[l], attn_sinks[l], conv_w[l], w_out[l], b_out[l])
        h = layer_norm(DN_ALPHA * h + mix, ln1_g[l], ln1_b[l])
        ff = peer_ffn(h, w_pq[l], sub_keys1[l], sub_keys2[l], u_experts[l], v_experts[l])
        h = layer_norm(DN_ALPHA * h + ff, ln2_g[l], ln2_b[l])
    return h
```

```python
import functools

import jax
import jax.numpy as jnp
from jax import lax
from jax.experimental import pallas as pl
from jax.experimental.pallas import tpu as pltpu

D_MODEL = 1024
HEAD_DIM = 64
N_Q_HEADS = 8
N_KV_HEADS = 2
ATTN_WIDTH = N_Q_HEADS * HEAD_DIM
KV_WIDTH = N_KV_HEADS * HEAD_DIM
BLOCK = 128
ROT_DIM = HEAD_DIM // 4
ROPE_THETA = 500000.0
CONV_WIDTH = D_MODEL - ATTN_WIDTH
CONV_K = 3
N_KEYS = 128
PEER_HEADS = 8
PEER_QDIM = 256
PEER_HALF = PEER_QDIM // 2
PEER_TOPK = 16
DEPTH = 1
DN_ALPHA = (2.0 * DEPTH) ** 0.25
LN_EPS = 1e-5

LANES = 128
BF16_ROWS = 8
BF16_COLS = 256
VMEM_LIMIT = 56 * 1024 * 1024

_O_K = ATTN_WIDTH
_O_V = _O_K + KV_WIDTH
_O_GB = _O_V + KV_WIDTH
_O_GC = _O_GB + CONV_WIDTH
_O_XC = _O_GC + CONV_WIDTH
_O_END = _O_XC + CONV_WIDTH

BF16 = jnp.bfloat16
F32 = jnp.float32
NEG_INF = float("-inf")


def _dot(a, b):
    return jnp.dot(a, b, preferred_element_type=F32)


def _dot_nt(a, b):
    return lax.dot_general(a, b, (((1,), (1,)), ((), ())), preferred_element_type=F32)


def _layer_norm(z, g, b):
    mu = jnp.mean(z, axis=-1, keepdims=True)
    zc = z - mu
    var = jnp.mean(zc * zc, axis=-1, keepdims=True)
    return zc * lax.rsqrt(var + LN_EPS) * g + b


def _front_kernel(sinks_ref, x_ref, rc_ref, rsa_ref, rsb_ref, w_in_ref, b_in_ref,
                  conv_w_ref, w_out_ref, b_out_ref, g1_ref, be1_ref, w_pq_ref,
                  k1_ref, k2_ref,
                  h1_ref, h1t_ref, s1_ref, s2_ref,
                  kbuf, vbuf, ubuf, kvar, vvar, ybuf, *, tm, tiles_per_seq):
    step = pl.program_id(0)
    nblk = tm // BLOCK

    @pl.when(step % tiles_per_seq == 0)
    def _():
        kbuf[0:BLOCK, :] = jnp.zeros((BLOCK, KV_WIDTH), F32)
        vbuf[0:BLOCK, :] = jnp.zeros((BLOCK, KV_WIDTH), F32)
        ubuf[0:8, :] = jnp.zeros((8, CONV_WIDTH), F32)

    x = x_ref[...]
    xb = x.astype(BF16)

    def proj(lo, hi):
        return _dot(xb, w_in_ref[:, lo:hi]) + b_in_ref[:, lo:hi]

    rc = rc_ref[...]
    rsa = rsa_ref[...]
    rsb = rsb_ref[...]

    def rope(t):
        return (t * rc + pltpu.roll(t, LANES - ROT_DIM // 2, 1) * rsa
                + pltpu.roll(t, ROT_DIM // 2, 1) * rsb)

    kbuf[BLOCK:BLOCK + tm, :] = rope(proj(_O_K, _O_V))
    vbuf[BLOCK:BLOCK + tm, :] = proj(_O_V, _O_GB)

    lane = lax.broadcasted_iota(jnp.int32, (tm + BLOCK, LANES), 1)
    lo = lane < HEAD_DIM
    for buf, var in ((kbuf, kvar), (vbuf, vvar)):
        t = buf[...]
        tr = pltpu.roll(t, HEAD_DIM, 1)
        var[0] = jnp.where(lo, t, 0.0).astype(BF16)
        var[1] = jnp.where(lo, 0.0, tr).astype(BF16)
        var[2] = jnp.where(lo, tr, 0.0).astype(BF16)
        var[3] = jnp.where(lo, 0.0, t).astype(BF16)

    qi = lax.broadcasted_iota(jnp.int32, (BLOCK, 2 * BLOCK), 0)
    kj = lax.broadcasted_iota(jnp.int32, (BLOCK, 2 * BLOCK), 1)
    diff = qi + BLOCK - kj
    band = (diff >= 0) & (diff < BLOCK)

    for p in range(N_Q_HEADS // 2):
        q_slab = (rope(proj(p * LANES, (p + 1) * LANES)) * (HEAD_DIM ** -0.5)).astype(BF16)
        for blk in range(nblk):
            first = (step * nblk + blk) % (tiles_per_seq * nblk) == 0
            kmin = jnp.where(first, BLOCK, 0)
            mask = band & (kj >= kmin)
            qs = q_slab[blk * BLOCK:(blk + 1) * BLOCK, :]
            o = jnp.zeros((BLOCK, LANES), F32)
            for hh in range(2):
                h = 2 * p + hh
                var = (h // (N_Q_HEADS // N_KV_HEADS)) * 2 + hh
                kc = kvar[var, blk * BLOCK:(blk + 2) * BLOCK, :]
                vc = vvar[var, blk * BLOCK:(blk + 2) * BLOCK, :]
                s = jnp.where(mask, _dot_nt(qs, kc), NEG_INF)
                sink = sinks_ref[h]
                m = jnp.maximum(jnp.max(s, axis=-1, keepdims=True), sink)
                pe = jnp.exp(s - m)
                den = jnp.sum(pe, axis=-1, keepdims=True) + jnp.exp(sink - m)
                o = o + _dot((pe / den).astype(BF16), vc)
            ybuf[blk * BLOCK:(blk + 1) * BLOCK, p * LANES:(p + 1) * LANES] = o.astype(BF16)

    u = proj(_O_GC, _O_XC) * proj(_O_XC, _O_END)
    ubuf[8:8 + tm, :] = u
    um1 = ubuf[7:7 + tm, :]
    um2 = ubuf[6:6 + tm, :]
    cw = conv_w_ref[...]
    yc = proj(_O_GB, _O_GC) * (cw[0:1, :] * um2 + cw[1:2, :] * um1 + cw[2:3, :] * u)
    ybuf[:, ATTN_WIDTH:] = yc.astype(BF16)

    ubuf[0:8, :] = ubuf[tm:tm + 8, :]
    kbuf[0:BLOCK, :] = kbuf[tm:tm + BLOCK, :]
    vbuf[0:BLOCK, :] = vbuf[tm:tm + BLOCK, :]

    mix = _dot(ybuf[...], w_out_ref[...]) + b_out_ref[...]
    h1 = _layer_norm(DN_ALPHA * x + mix, g1_ref[...], be1_ref[...])
    h1_ref[...] = h1
    h1t_ref[...] = h1.T.astype(BF16)

    qp = _dot(h1.astype(BF16), w_pq_ref[...]).astype(BF16)
    k1 = k1_ref[...]
    k2 = k2_ref[...]
    for h in range(PEER_HEADS):
        base = h * PEER_QDIM
        s1_ref[h] = _dot_nt(k1, qp[:, base:base + PEER_HALF])
        s2_ref[h] = _dot_nt(k2, qp[:, base + PEER_HALF:base + PEER_QDIM])


def _front(xf, rc, rsa, rsb, sinks, w_in, b_in, conv_w, w_out, b_out, g1, be1, w_pq, k1, k2,
           *, seq, tm):
    T = xf.shape[0]
    proj_w = w_in.shape[1]
    tiles_per_seq = seq // tm
    full = lambda shape: pl.BlockSpec(shape, lambda i: (0,) * len(shape))
    tok = lambda w: pl.BlockSpec((tm, w), lambda i: (i, 0))
    kern = functools.partial(_front_kernel, tm=tm, tiles_per_seq=tiles_per_seq)
    return pl.pallas_call(
        kern,
        grid=(T // tm,),
        in_specs=[
            pl.BlockSpec(memory_space=pltpu.SMEM),
            tok(D_MODEL), tok(LANES), tok(LANES), tok(LANES),
            full((D_MODEL, proj_w)), full((1, proj_w)),
            full((CONV_K, CONV_WIDTH)), full((D_MODEL, D_MODEL)), full((1, D_MODEL)),
            full((1, D_MODEL)), full((1, D_MODEL)),
            full((D_MODEL, PEER_HEADS * PEER_QDIM)),
            full((N_KEYS, PEER_HALF)), full((N_KEYS, PEER_HALF)),
        ],
        out_specs=[
            tok(D_MODEL),
            pl.BlockSpec((D_MODEL, tm), lambda i: (0, i)),
            pl.BlockSpec((PEER_HEADS, N_KEYS, tm), lambda i: (0, 0, i)),
            pl.BlockSpec((PEER_HEADS, N_KEYS, tm), lambda i: (0, 0, i)),
        ],
        out_shape=[
            jax.ShapeDtypeStruct((T, D_MODEL), F32),
            jax.ShapeDtypeStruct((D_MODEL, T), BF16),
            jax.ShapeDtypeStruct((PEER_HEADS, N_KEYS, T), F32),
            jax.ShapeDtypeStruct((PEER_HEADS, N_KEYS, T), F32),
        ],
        scratch_shapes=[
            pltpu.VMEM((tm + BLOCK, KV_WIDTH), F32),
            pltpu.VMEM((tm + BLOCK, KV_WIDTH), F32),
            pltpu.VMEM((tm + 8, CONV_WIDTH), F32),
            pltpu.VMEM((4, tm + BLOCK, KV_WIDTH), BF16),
            pltpu.VMEM((4, tm + BLOCK, KV_WIDTH), BF16),
            pltpu.VMEM((tm, D_MODEL), BF16),
        ],
        compiler_params=pltpu.CompilerParams(
            dimension_semantics=("arbitrary",), vmem_limit_bytes=VMEM_LIMIT),
        name="front",
    )(sinks, xf, rc, rsa, rsb, w_in, b_in, conv_w, w_out, b_out, g1, be1, w_pq, k1, k2)


def _top16(s):
    rows = lax.broadcasted_iota(jnp.int32, s.shape, 0)
    rem = s
    rank = jnp.full(s.shape, float(PEER_TOPK), F32)
    vals = []
    for k in range(PEER_TOPK):
        m = jnp.max(rem, axis=0, keepdims=True)
        idx = jnp.min(jnp.where(rem == m, rows, N_KEYS), axis=0, keepdims=True)
        sel = rows == idx
        rem = jnp.where(sel, NEG_INF, rem)
        rank = jnp.where(sel, float(k), rank)
        vals.append(m)
    return vals, rank


_N_CAND_ROWS = PEER_TOPK + 7 * 8 + 8


def _cand_positions():
    import numpy as np
    pos = np.full((_N_CAND_ROWS,), -1, np.int32)
    for r in range(_N_CAND_ROWS):
        if r < 16:
            a, b = 0, r
        elif r < 72:
            a, b = 1 + (r - 16) // 8, (r - 16) % 8
        else:
            a, b = 8 + (r - 72), 0
        if (a + 1) * (b + 1) <= PEER_TOPK:
            pos[r] = a * PEER_TOPK + b
    return pos


def _select_kernel(pos_ref, s1_ref, s2_ref, r2_ref, e2_ref, n1_ref, e1_ref, *, ts):
    pos = pos_ref[...]
    valid = pos >= 0
    row16 = lax.broadcasted_iota(jnp.int32, (PEER_TOPK, LANES), 0)

    def group(g, carry):
        sl = pl.ds(pl.multiple_of(g * LANES, LANES), LANES)
        s1 = s1_ref[0, :, sl]
        s2 = s2_ref[0, :, sl]
        v1, rank1 = _top16(s1)
        v2, rank2 = _top16(s2)

        v1m = jnp.zeros((PEER_TOPK, LANES), F32)
        v2m = jnp.zeros((PEER_TOPK, LANES), F32)
        for a in range(PEER_TOPK):
            v1m = jnp.where(row16 == a, v1[a], v1m)
            v2m = jnp.where(row16 == a, v2[a], v2m)
        blocks = [v1[0] + v2m]
        for a in range(1, 8):
            blocks.append(v1[a] + v2m[0:8, :])
        blocks.append(v1m[8:16, :] + v2[0])
        cand = jnp.where(valid, jnp.concatenate(blocks, axis=0), NEG_INF)

        rem = cand
        chosen = jnp.zeros(cand.shape, F32)
        for _ in range(PEER_TOPK):
            m = jnp.max(rem, axis=0, keepdims=True)
            idx = jnp.min(jnp.where(rem == m, pos, 1 << 20), axis=0, keepdims=True)
            sel = (pos == idx) & valid
            rem = jnp.where(sel, NEG_INF, rem)
            chosen = jnp.where(sel, 1.0, chosen)

        top = v1[0] + v2[0]
        z = jnp.sum(chosen * jnp.exp(jnp.where(valid, cand, top) - top), axis=0, keepdims=True)

        counts = [jnp.sum(chosen[0:16, :], axis=0, keepdims=True)]
        for a in range(1, 8):
            counts.append(jnp.sum(chosen[8 + 8 * a:16 + 8 * a, :], axis=0, keepdims=True))
        for a in range(8, 16):
            counts.append(chosen[64 + a:65 + a, :])
        n1 = jnp.zeros(s1.shape, F32)
        for a in range(PEER_TOPK):
            n1 = jnp.where(rank1 == float(a), counts[a], n1)

        n1_ref[:, 0, :, sl] = n1.reshape(N_KEYS // 8, 8, LANES)
        e1_ref[:, 0, :, sl] = jnp.exp(s1 - v1[0]).reshape(N_KEYS // 8, 8, LANES)
        r2_ref[0, :, sl] = rank2.astype(BF16)
        e2_ref[0, :, sl] = (jnp.exp(s2 - v2[0]) / z).astype(BF16)
        return carry

    lax.fori_loop(0, ts // LANES, group, 0)


def _select(s1t, s2t, *, ts):
    H, K, T = s1t.shape
    import numpy as np
    pos = jnp.asarray(np.tile(_cand_positions()[:, None], (1, LANES)))
    blk = pl.BlockSpec((1, K, ts), lambda t, h: (h, 0, t))
    blk1 = pl.BlockSpec((K // 8, 1, 8, ts), lambda t, h: (0, h, 0, t))
    return pl.pallas_call(
        functools.partial(_select_kernel, ts=ts),
        grid=(T // ts, H),
        in_specs=[pl.BlockSpec((_N_CAND_ROWS, LANES), lambda t, h: (0, 0)), blk, blk],
        out_specs=[blk, blk, blk1, blk1],
        out_shape=[
            jax.ShapeDtypeStruct((H, K, T), BF16),
            jax.ShapeDtypeStruct((H, K, T), BF16),
            jax.ShapeDtypeStruct((K // 8, H, 8, T), F32),
            jax.ShapeDtypeStruct((K // 8, H, 8, T), F32),
        ],
        compiler_params=pltpu.CompilerParams(
            dimension_semantics=("arbitrary", "arbitrary"), vmem_limit_bytes=VMEM_LIMIT),
        name="select",
    )(pos, s1t, s2t)


def _peer_kernel(u_ref, vt_ref, xt_ref, r2_ref, e2_ref, n1_ref, e1_ref, h1_ref, g2_ref, be2_ref,
                 out_ref, acc_ref, w_ref, *, ec, tt):
    c = pl.program_id(1)
    nkb = ec // N_KEYS
    assert nkb == 8

    @pl.when(c == 0)
    def _():
        acc_ref[...] = jnp.zeros(acc_ref.shape, F32)

    def row_bf16(ref, h, ib, sl):
        return jnp.broadcast_to(ref[c, h, ib:ib + 1, sl], (BF16_ROWS, BF16_COLS)).astype(BF16)

    rows = BF16_ROWS
    for nt in range(tt // BF16_COLS):
        nsl = slice(nt * BF16_COLS, (nt + 1) * BF16_COLS)
        a = _dot(u_ref[...], xt_ref[:, nsl])
        for ib in range(nkb):
            n1b = [row_bf16(n1_ref, h, ib, nsl) for h in range(PEER_HEADS)]
            e1b = [row_bf16(e1_ref, h, ib, nsl) for h in range(PEER_HEADS)]
            for rg in range(N_KEYS // rows):
                ksl = slice(rg * rows, (rg + 1) * rows)
                g = jnp.zeros((rows, BF16_COLS), BF16)
                for h in range(PEER_HEADS):
                    e2 = e2_ref[h, ksl, nsl]
                    g = g + jnp.where(r2_ref[h, ksl, nsl] < n1b[h], e2, jnp.zeros_like(e2)) * e1b[h]
                esl = slice(ib * N_KEYS + rg * rows, ib * N_KEYS + (rg + 1) * rows)
                ab = a[esl, :]
                ge = 0.5 * ab * (1.0 + lax.erf(ab * (2.0 ** -0.5)))
                w_ref[esl, nsl] = ge.astype(BF16) * g
        acc_ref[:, nsl] += _dot(vt_ref[...], w_ref[:, nsl])

    @pl.when(c == pl.num_programs(1) - 1)
    def _():
        z = DN_ALPHA * h1_ref[...] + acc_ref[...].T
        out_ref[...] = _layer_norm(z, g2_ref[...], be2_ref[...])


def _peer(u_bf, vt_bf, h1t, r2, e2, n1, e1, h1, g2, be2, *, ec, tt):
    E = u_bf.shape[0]
    T = h1.shape[0]
    tokrows = pl.BlockSpec((PEER_HEADS, N_KEYS, tt), lambda t, c: (0, 0, t))
    tokkeys = pl.BlockSpec((N_KEYS // 8, PEER_HEADS, 8, tt), lambda t, c: (0, 0, 0, t))
    vec = pl.BlockSpec((1, D_MODEL), lambda t, c: (0, 0))
    return pl.pallas_call(
        functools.partial(_peer_kernel, ec=ec, tt=tt),
        grid=(T // tt, E // ec),
        in_specs=[
            pl.BlockSpec((ec, D_MODEL), lambda t, c: (c, 0)),
            pl.BlockSpec((D_MODEL, ec), lambda t, c: (0, c)),
            pl.BlockSpec((D_MODEL, tt), lambda t, c: (0, t)),
            tokrows, tokrows, tokkeys, tokkeys,
            pl.BlockSpec((tt, D_MODEL), lambda t, c: (t, 0)),
            vec, vec,
        ],
        out_specs=pl.BlockSpec((tt, D_MODEL), lambda t, c: (t, 0)),
        out_shape=jax.ShapeDtypeStruct((T, D_MODEL), F32),
        scratch_shapes=[
            pltpu.VMEM((D_MODEL, tt), F32),
            pltpu.VMEM((ec, tt), BF16),
        ],
        compiler_params=pltpu.CompilerParams(
            dimension_semantics=("arbitrary", "arbitrary"), vmem_limit_bytes=VMEM_LIMIT),
        name="peer",
    )(u_bf, vt_bf, h1t, r2, e2, n1, e1, h1, g2, be2)


def _rope_tables(positions):
    T = positions.size
    half = ROT_DIM // 2
    inv_freq = ROPE_THETA ** (-jnp.arange(0, ROT_DIM, 2, dtype=F32) / ROT_DIM)
    ang = positions.reshape(T, 1).astype(F32) * inv_freq
    cos, sin = jnp.cos(ang), jnp.sin(ang)
    zeros = lambda w: jnp.zeros((T, w), F32)
    rc = jnp.concatenate([cos, cos, jnp.ones((T, HEAD_DIM - ROT_DIM), F32)], axis=1)
    rsa = jnp.concatenate([-sin, zeros(HEAD_DIM - half)], axis=1)
    rsb = jnp.concatenate([zeros(half), sin, zeros(HEAD_DIM - ROT_DIM)], axis=1)
    rep = LANES // HEAD_DIM
    return jnp.tile(rc, (1, rep)), jnp.tile(rsa, (1, rep)), jnp.tile(rsb, (1, rep))


def _tiles(seq, total_tokens):
    tm = 256 if seq % 256 == 0 else BLOCK
    ts = 512 if total_tokens % 512 == 0 else LANES
    tt = 512 if total_tokens % 512 == 0 else 256
    ec = 8 * N_KEYS
    return tm, ts, tt, ec


def kernel(x, positions, w_in, b_in, attn_sinks, conv_w, w_out, b_out, ln1_g, ln1_b,
           w_pq, sub_keys1, sub_keys2, u_experts, v_experts, ln2_g, ln2_b):
    B, S, D = x.shape
    T = B * S
    assert D == D_MODEL and S % BLOCK == 0 and T % 256 == 0
    assert w_in.shape[0] == DEPTH
    tm, ts, tt, ec = _tiles(S, T)
    row = lambda v: v.reshape(1, -1).astype(F32)

    rc, rsa, rsb = _rope_tables(positions)
    h1, h1t, s1t, s2t = _front(
        x.reshape(T, D), rc, rsa, rsb, attn_sinks[0].astype(F32),
        w_in[0].astype(BF16), row(b_in[0]), conv_w[0].astype(F32),
        w_out[0].astype(BF16), row(b_out[0]), row(ln1_g[0]), row(ln1_b[0]),
        w_pq[0].astype(BF16), sub_keys1[0].astype(BF16), sub_keys2[0].astype(BF16),
        seq=S, tm=tm)
    r2, e2, n1, e1 = _select(s1t, s2t, ts=ts)
    out = _peer(u_experts[0].astype(BF16), v_experts[0].T.astype(BF16), h1t,
                r2, e2, n1, e1, h1, row(ln2_g[0]), row(ln2_b[0]), ec=ec, tt=tt)
    return out.reshape(B, S, D)
```

```python
import functools

import jax
import jax.numpy as jnp
from jax import lax
from jax.experimental import pallas as pl
from jax.experimental.pallas import tpu as pltpu

D_MODEL = 1024
HEAD_DIM = 64
N_Q_HEADS = 8
N_KV_HEADS = 2
ATTN_WIDTH = N_Q_HEADS * HEAD_DIM
KV_WIDTH = N_KV_HEADS * HEAD_DIM
BLOCK = 128
ROT_DIM = HEAD_DIM // 4
ROPE_THETA = 500000.0
CONV_WIDTH = D_MODEL - ATTN_WIDTH
CONV_K = 3
N_KEYS = 128
PEER_HEADS = 8
PEER_QDIM = 256
PEER_HALF = PEER_QDIM // 2
PEER_TOPK = 16
DEPTH = 1
DN_ALPHA = (2.0 * DEPTH) ** 0.25
LN_EPS = 1e-5

LANES = 128
BF16_TILE_ROWS = 16
PEER_DOT_ROWS = 128
GATE_PAD = 2 * LANES
VMEM_LIMIT = 56 * 1024 * 1024

_O_K = ATTN_WIDTH
_O_V = _O_K + KV_WIDTH
_O_GB = _O_V + KV_WIDTH
_O_GC = _O_GB + CONV_WIDTH
_O_XC = _O_GC + CONV_WIDTH
_O_END = _O_XC + CONV_WIDTH

BF16 = jnp.bfloat16
F32 = jnp.float32
NEG_INF = float("-inf")


def _dot(a, b):
    return jnp.dot(a, b, preferred_element_type=F32)


def _dot_nt(a, b):
    return lax.dot_general(a, b, (((1,), (1,)), ((), ())), preferred_element_type=F32)


def _layer_norm(z, g, b):
    mu = jnp.mean(z, axis=-1, keepdims=True)
    zc = z - mu
    var = jnp.mean(zc * zc, axis=-1, keepdims=True)
    return zc * lax.rsqrt(var + LN_EPS) * g + b


def _front_kernel(sinks_ref, x_ref, rc_ref, rsa_ref, rsb_ref, w_in_ref, b_in_ref,
                  conv_w_ref, w_out_ref, b_out_ref, g1_ref, be1_ref, w_pq_ref,
                  k1_ref, k2_ref,
                  h1_ref, h1t_ref, s1_ref, s2_ref,
                  kbuf, vbuf, ubuf, kvar, vvar, ybuf, *, tm, tiles_per_seq):
    step = pl.program_id(0)
    nblk = tm // BLOCK

    @pl.when(step % tiles_per_seq == 0)
    def _():
        kbuf[0:BLOCK, :] = jnp.zeros((BLOCK, KV_WIDTH), F32)
        vbuf[0:BLOCK, :] = jnp.zeros((BLOCK, KV_WIDTH), F32)
        ubuf[0:8, :] = jnp.zeros((8, CONV_WIDTH), F32)

    x = x_ref[...]
    xb = x.astype(BF16)

    def proj(lo, hi):
        return _dot(xb, w_in_ref[:, lo:hi]) + b_in_ref[:, lo:hi]

    rc = rc_ref[...]
    rsa = rsa_ref[...]
    rsb = rsb_ref[...]

    def rope(t):
        return (t * rc + pltpu.roll(t, LANES - ROT_DIM // 2, 1) * rsa
                + pltpu.roll(t, ROT_DIM // 2, 1) * rsb)

    kbuf[BLOCK:BLOCK + tm, :] = rope(proj(_O_K, _O_V))
    vbuf[BLOCK:BLOCK + tm, :] = proj(_O_V, _O_GB)

    lane = lax.broadcasted_iota(jnp.int32, (tm + BLOCK, LANES), 1)
    lo = lane < HEAD_DIM
    for buf, var in ((kbuf, kvar), (vbuf, vvar)):
        t = buf[...]
        tr = pltpu.roll(t, HEAD_DIM, 1)
        var[0] = jnp.where(lo, t, 0.0).astype(BF16)
        var[1] = jnp.where(lo, 0.0, tr).astype(BF16)
        var[2] = jnp.where(lo, tr, 0.0).astype(BF16)
        var[3] = jnp.where(lo, 0.0, t).astype(BF16)

    qi = lax.broadcasted_iota(jnp.int32, (BLOCK, 2 * BLOCK), 0)
    kj = lax.broadcasted_iota(jnp.int32, (BLOCK, 2 * BLOCK), 1)
    diff = qi + BLOCK - kj
    band = (diff >= 0) & (diff < BLOCK)

    for p in range(N_Q_HEADS // 2):
        q_slab = (rope(proj(p * LANES, (p + 1) * LANES)) * (HEAD_DIM ** -0.5)).astype(BF16)
        for blk in range(nblk):
            first = (step * nblk + blk) % (tiles_per_seq * nblk) == 0
            kmin = jnp.where(first, BLOCK, 0)
            mask = band & (kj >= kmin)
            qs = q_slab[blk * BLOCK:(blk + 1) * BLOCK, :]
            o = jnp.zeros((BLOCK, LANES), F32)
            for hh in range(2):
                h = 2 * p + hh
                var = (h // (N_Q_HEADS // N_KV_HEADS)) * 2 + hh
                kc = kvar[var, blk * BLOCK:(blk + 2) * BLOCK, :]
                vc = vvar[var, blk * BLOCK:(blk + 2) * BLOCK, :]
                s = jnp.where(mask, _dot_nt(qs, kc), NEG_INF)
                sink = sinks_ref[h]
                m = jnp.maximum(jnp.max(s, axis=-1, keepdims=True), sink)
                pe = jnp.exp(s - m)
                den = jnp.sum(pe, axis=-1, keepdims=True) + jnp.exp(sink - m)
                o = o + _dot((pe / den).astype(BF16), vc)
            ybuf[blk * BLOCK:(blk + 1) * BLOCK, p * LANES:(p + 1) * LANES] = o.astype(BF16)

    u = proj(_O_GC, _O_XC) * proj(_O_XC, _O_END)
    ubuf[8:8 + tm, :] = u
    um1 = ubuf[7:7 + tm, :]
    um2 = ubuf[6:6 + tm, :]
    cw = conv_w_ref[...]
    yc = proj(_O_GB, _O_GC) * (cw[0:1, :] * um2 + cw[1:2, :] * um1 + cw[2:3, :] * u)
    ybuf[:, ATTN_WIDTH:] = yc.astype(BF16)

    ubuf[0:8, :] = ubuf[tm:tm + 8, :]
    kbuf[0:BLOCK, :] = kbuf[tm:tm + BLOCK, :]
    vbuf[0:BLOCK, :] = vbuf[tm:tm + BLOCK, :]

    mix = _dot(ybuf[...], w_out_ref[...]) + b_out_ref[...]
    h1 = _layer_norm(DN_ALPHA * x + mix, g1_ref[...], be1_ref[...])
    h1_ref[...] = h1
    h1t_ref[...] = h1.T.astype(BF16)

    qp = _dot(h1.astype(BF16), w_pq_ref[...]).astype(BF16)
    k1 = k1_ref[...]
    k2 = k2_ref[...]
    for h in range(PEER_HEADS):
        base = h * PEER_QDIM
        s1_ref[h] = _dot_nt(k1, qp[:, base:base + PEER_HALF])
        s2_ref[h] = _dot_nt(k2, qp[:, base + PEER_HALF:base + PEER_QDIM])


def _front(xf, rc, rsa, rsb, sinks, w_in, b_in, conv_w, w_out, b_out, g1, be1, w_pq, k1, k2,
           *, seq, tm):
    T = xf.shape[0]
    proj_w = w_in.shape[1]
    tiles_per_seq = seq // tm
    full = lambda shape: pl.BlockSpec(shape, lambda i: (0,) * len(shape))
    tok = lambda w: pl.BlockSpec((tm, w), lambda i: (i, 0))
    kern = functools.partial(_front_kernel, tm=tm, tiles_per_seq=tiles_per_seq)
    return pl.pallas_call(
        kern,
        grid=(T // tm,),
        in_specs=[
            pl.BlockSpec(memory_space=pltpu.SMEM),
            tok(D_MODEL), tok(LANES), tok(LANES), tok(LANES),
            full((D_MODEL, proj_w)), full((1, proj_w)),
            full((CONV_K, CONV_WIDTH)), full((D_MODEL, D_MODEL)), full((1, D_MODEL)),
            full((1, D_MODEL)), full((1, D_MODEL)),
            full((D_MODEL, PEER_HEADS * PEER_QDIM)),
            full((N_KEYS, PEER_HALF)), full((N_KEYS, PEER_HALF)),
        ],
        out_specs=[
            tok(D_MODEL),
            pl.BlockSpec((D_MODEL, tm), lambda i: (0, i)),
            pl.BlockSpec((PEER_HEADS, N_KEYS, tm), lambda i: (0, 0, i)),
            pl.BlockSpec((PEER_HEADS, N_KEYS, tm), lambda i: (0, 0, i)),
        ],
        out_shape=[
            jax.ShapeDtypeStruct((T, D_MODEL), F32),
            jax.ShapeDtypeStruct((D_MODEL, T), BF16),
            jax.ShapeDtypeStruct((PEER_HEADS, N_KEYS, T), F32),
            jax.ShapeDtypeStruct((PEER_HEADS, N_KEYS, T), F32),
        ],
        scratch_shapes=[
            pltpu.VMEM((tm + BLOCK, KV_WIDTH), F32),
            pltpu.VMEM((tm + BLOCK, KV_WIDTH), F32),
            pltpu.VMEM((tm + 8, CONV_WIDTH), F32),
            pltpu.VMEM((4, tm + BLOCK, KV_WIDTH), BF16),
            pltpu.VMEM((4, tm + BLOCK, KV_WIDTH), BF16),
            pltpu.VMEM((tm, D_MODEL), BF16),
        ],
        compiler_params=pltpu.CompilerParams(
            dimension_semantics=("arbitrary",), vmem_limit_bytes=VMEM_LIMIT),
        name="front",
    )(sinks, xf, rc, rsa, rsb, w_in, b_in, conv_w, w_out, b_out, g1, be1, w_pq, k1, k2)


def _top16(s):
    rows = lax.broadcasted_iota(jnp.int32, s.shape, 0)
    rem = s
    rank = jnp.full(s.shape, float(PEER_TOPK), F32)
    vals = []
    for k in range(PEER_TOPK):
        m = jnp.max(rem, axis=0, keepdims=True)
        idx = jnp.min(jnp.where(rem == m, rows, N_KEYS), axis=0, keepdims=True)
        sel = rows == idx
        rem = jnp.where(sel, NEG_INF, rem)
        rank = jnp.where(sel, float(k), rank)
        vals.append(m)
    return vals, rank


_N_CAND_ROWS = PEER_TOPK + 7 * 8 + 8


def _cand_positions():
    import numpy as np
    pos = np.full((_N_CAND_ROWS,), -1, np.int32)
    for r in range(_N_CAND_ROWS):
        if r < 16:
            a, b = 0, r
        elif r < 72:
            a, b = 1 + (r - 16) // 8, (r - 16) % 8
        else:
            a, b = 8 + (r - 72), 0
        if (a + 1) * (b + 1) <= PEER_TOPK:
            pos[r] = a * PEER_TOPK + b
    return pos


def _select_kernel(pos_ref, s1_ref, s2_ref, r2_ref, e2_ref, n1_ref, e1_ref, *, ts):
    pos = pos_ref[...]
    valid = pos >= 0
    row16 = lax.broadcasted_iota(jnp.int32, (PEER_TOPK, LANES), 0)

    def group(g, carry):
        sl = pl.ds(pl.multiple_of(g * LANES, LANES), LANES)
        s1 = s1_ref[0, :, sl]
        s2 = s2_ref[0, :, sl]
        v1, rank1 = _top16(s1)
        v2, rank2 = _top16(s2)

        v1m = jnp.zeros((PEER_TOPK, LANES), F32)
        v2m = jnp.zeros((PEER_TOPK, LANES), F32)
        for a in range(PEER_TOPK):
            v1m = jnp.where(row16 == a, v1[a], v1m)
            v2m = jnp.where(row16 == a, v2[a], v2m)
        blocks = [v1[0] + v2m]
        for a in range(1, 8):
            blocks.append(v1[a] + v2m[0:8, :])
        blocks.append(v1m[8:16, :] + v2[0])
        cand = jnp.where(valid, jnp.concatenate(blocks, axis=0), NEG_INF)

        rem = cand
        chosen = jnp.zeros(cand.shape, F32)
        for _ in range(PEER_TOPK):
            m = jnp.max(rem, axis=0, keepdims=True)
            idx = jnp.min(jnp.where(rem == m, pos, 1 << 20), axis=0, keepdims=True)
            sel = (pos == idx) & valid
            rem = jnp.where(sel, NEG_INF, rem)
            chosen = jnp.where(sel, 1.0, chosen)

        top = v1[0] + v2[0]
        z = jnp.sum(chosen * jnp.exp(jnp.where(valid, cand, top) - top), axis=0, keepdims=True)

        counts = [jnp.sum(chosen[0:16, :], axis=0, keepdims=True)]
        for a in range(1, 8):
            counts.append(jnp.sum(chosen[8 + 8 * a:16 + 8 * a, :], axis=0, keepdims=True))
        for a in range(8, 16):
            counts.append(chosen[64 + a:65 + a, :])
        n1 = jnp.zeros(s1.shape, F32)
        for a in range(PEER_TOPK):
            n1 = jnp.where(rank1 == float(a), counts[a], n1)

        n1_ref[:, 0, :, sl] = n1.reshape(N_KEYS // 8, 8, LANES)
        e1_ref[:, 0, :, sl] = jnp.exp(s1 - v1[0]).reshape(N_KEYS // 8, 8, LANES)
        r2_ref[0, :, sl] = rank2.astype(BF16)
        e2_ref[0, :, sl] = (jnp.exp(s2 - v2[0]) / z).astype(BF16)
        return carry

    lax.fori_loop(0, ts // LANES, group, 0)


def _select(s1t, s2t, *, ts):
    H, K, T = s1t.shape
    import numpy as np
    pos = jnp.asarray(np.tile(_cand_positions()[:, None], (1, LANES)))
    blk = pl.BlockSpec((1, K, ts), lambda t, h: (h, 0, t))
    blk2 = blk
    blk1 = pl.BlockSpec((K // 8, 1, 8, ts), lambda t, h: (0, h, 0, t))
    return pl.pallas_call(
        functools.partial(_select_kernel, ts=ts),
        grid=(T // ts, H),
        in_specs=[pl.BlockSpec((_N_CAND_ROWS, LANES), lambda t, h: (0, 0)), blk, blk],
        out_specs=[blk2, blk2, blk1, blk1],
        out_shape=[
            jax.ShapeDtypeStruct((H, K, T), BF16),
            jax.ShapeDtypeStruct((H, K, T), BF16),
            jax.ShapeDtypeStruct((K // 8, H, 8, T), F32),
            jax.ShapeDtypeStruct((K // 8, H, 8, T), F32),
        ],
        compiler_params=pltpu.CompilerParams(
            dimension_semantics=("arbitrary", "arbitrary"), vmem_limit_bytes=VMEM_LIMIT),
        name="select",
    )(pos, s1t, s2t)


def _peer_kernel(u_ref, vt_ref, xt_ref, r2_in, e2_in, n1_in, e1_in, h1_ref, g2_ref, be2_ref,
                 out_ref, acc_ref, a0_ref, a1_ref, w0_ref, w1_ref,
                 r2_ref, e2_ref, n1_ref, e1_ref, *, ec, tt, nchunks):
    s = pl.program_id(1)
    nkb = ec // N_KEYS
    assert nkb == 8
    rows = BF16_TILE_ROWS

    @pl.when(s == 0)
    def _():
        acc_ref[...] = jnp.zeros(acc_ref.shape, F32)
        for ref in (a0_ref, a1_ref, w0_ref, w1_ref):
            ref[...] = jnp.zeros(ref.shape, ref.dtype)
        r2_ref[:, :, 0:tt] = r2_in[...]
        e2_ref[:, :, LANES:LANES + tt] = e2_in[...]
        n1_ref[...] = n1_in[...]
        e1_ref[...] = e1_in[...]

    cg = jnp.clip(s - 1, 0, nchunks - 1)

    def stages(a_new, a_prev, w_new, w_old):
        def row_bf16(ref, h, ib, sl):
            return jnp.broadcast_to(ref[cg, h, ib:ib + 1, sl], (rows, LANES)).astype(BF16)

        def gate_unit(kp, lt):
            lsl = slice(lt * LANES, (lt + 1) * LANES)
            nrg = N_KEYS // rows
            keys = (2 * kp, 2 * kp + 1)
            g = [[None] * nrg for _ in keys]
            for h in range(PEER_HEADS):
                n1b = [row_bf16(n1_ref, h, ib, lsl) for ib in keys]
                e1b = [row_bf16(e1_ref, h, ib, lsl) for ib in keys]
                for rg in range(nrg):
                    wsl = slice(rg * rows, (rg + 1) * rows)
                    r2 = r2_ref[h, wsl, lsl]
                    e2 = e2_ref[h, wsl, (lt + 1) * LANES:(lt + 2) * LANES]
                    for k in range(2):
                        term = jnp.where(r2 < n1b[k], e2, jnp.zeros_like(e2)) * e1b[k]
                        g[k][rg] = term if h == 0 else g[k][rg] + term
            for k, ib in enumerate(keys):
                for rg in range(nrg):
                    esl = slice(ib * N_KEYS + rg * rows, ib * N_KEYS + (rg + 1) * rows)
                    ab = a_prev[esl, lsl]
                    ge = 0.5 * ab * (1.0 + lax.erf(ab * (2.0 ** -0.5)))
                    w_new[esl, lsl] = ge.astype(BF16) * g[k][rg]

        def out_piece(p):
            psl = slice(p * PEER_DOT_ROWS, (p + 1) * PEER_DOT_ROWS)
            acc_ref[psl, :] += _dot(vt_ref[psl, :], w_old[...])

        def score_piece(p):
            psl = slice(p * PEER_DOT_ROWS, (p + 1) * PEER_DOT_ROWS)
            a_new[psl, :] = _dot(u_ref[psl, :], xt_ref[...])

        units = [(kp, lt) for kp in range(nkb // 2) for lt in range(tt // LANES)]
        npieces = ec // PEER_DOT_ROWS
        pieces = [f for p in range(npieces) for f in ((out_piece, p), (score_piece, p))]
        per = len(units) // len(pieces)
        assert per * len(pieces) == len(units)
        for n, (fn, p) in enumerate(pieces):
            for kp, lt in units[n * per:(n + 1) * per]:
                gate_unit(kp, lt)
            fn(p)


    @pl.when(s % 2 == 0)
    def _():
        stages(a0_ref, a1_ref, w1_ref, w0_ref)

    @pl.when(s % 2 == 1)
    def _():
        stages(a1_ref, a0_ref, w0_ref, w1_ref)

    @pl.when(s == nchunks + 1)
    def _():
        z = DN_ALPHA * h1_ref[...] + acc_ref[...].T
        out_ref[...] = _layer_norm(z, g2_ref[...], be2_ref[...])


def _peer(u_bf, vt_bf, h1t, r2, e2, n1, e1, h1, g2, be2, *, ec, tt):
    E = u_bf.shape[0]
    T = h1.shape[0]
    nchunks = E // ec
    last = nchunks - 1
    tokrows = pl.BlockSpec((PEER_HEADS, N_KEYS, tt), lambda t, s: (0, 0, t))
    tokkeys = pl.BlockSpec((N_KEYS // 8, PEER_HEADS, 8, tt), lambda t, s: (0, 0, 0, t))
    vec = pl.BlockSpec((1, D_MODEL), lambda t, s: (0, 0))
    return pl.pallas_call(
        functools.partial(_peer_kernel, ec=ec, tt=tt, nchunks=nchunks),
        grid=(T // tt, nchunks + 2),
        in_specs=[
            pl.BlockSpec((ec, D_MODEL), lambda t, s: (jnp.minimum(s, last), 0)),
            pl.BlockSpec((D_MODEL, ec), lambda t, s: (0, jnp.clip(s - 2, 0, last))),
            pl.BlockSpec((D_MODEL, tt), lambda t, s: (0, t)),
            tokrows, tokrows, tokkeys, tokkeys,
            pl.BlockSpec((tt, D_MODEL), lambda t, s: (t, 0)),
            vec, vec,
        ],
        out_specs=pl.BlockSpec((tt, D_MODEL), lambda t, s: (t, 0)),
        out_shape=jax.ShapeDtypeStruct((T, D_MODEL), F32),
        scratch_shapes=[
            pltpu.VMEM((D_MODEL, tt), F32),
            pltpu.VMEM((ec, tt), F32),
            pltpu.VMEM((ec, tt), F32),
            pltpu.VMEM((ec, tt), BF16),
            pltpu.VMEM((ec, tt), BF16),
            pltpu.VMEM((PEER_HEADS, N_KEYS, tt + GATE_PAD), BF16),
            pltpu.VMEM((PEER_HEADS, N_KEYS, tt + GATE_PAD), BF16),
            pltpu.VMEM((N_KEYS // 8, PEER_HEADS, 8, tt), F32),
            pltpu.VMEM((N_KEYS // 8, PEER_HEADS, 8, tt), F32),
        ],
        compiler_params=pltpu.CompilerParams(
            dimension_semantics=("arbitrary", "arbitrary"), vmem_limit_bytes=VMEM_LIMIT),
        name="peer",
    )(u_bf, vt_bf, h1t, r2, e2, n1, e1, h1, g2, be2)


def _rope_tables(positions):
    T = positions.size
    half = ROT_DIM // 2
    inv_freq = ROPE_THETA ** (-jnp.arange(0, ROT_DIM, 2, dtype=F32) / ROT_DIM)
    ang = positions.reshape(T, 1).astype(F32) * inv_freq
    cos, sin = jnp.cos(ang), jnp.sin(ang)
    zeros = lambda w: jnp.zeros((T, w), F32)
    rc = jnp.concatenate([cos, cos, jnp.ones((T, HEAD_DIM - ROT_DIM), F32)], axis=1)
    rsa = jnp.concatenate([-sin, zeros(HEAD_DIM - half)], axis=1)
    rsb = jnp.concatenate([zeros(half), sin, zeros(HEAD_DIM - ROT_DIM)], axis=1)
    rep = LANES // HEAD_DIM
    return jnp.tile(rc, (1, rep)), jnp.tile(rsa, (1, rep)), jnp.tile(rsb, (1, rep))


def _tiles(seq, total_tokens):
    tm = 256 if seq % 256 == 0 else BLOCK
    ts = 512 if total_tokens % 512 == 0 else LANES
    tt = 512 if total_tokens % 512 == 0 else 256
    ec = 8 * N_KEYS
    return tm, ts, tt, ec


def kernel(x, positions, w_in, b_in, attn_sinks, conv_w, w_out, b_out, ln1_g, ln1_b,
           w_pq, sub_keys1, sub_keys2, u_experts, v_experts, ln2_g, ln2_b):
    B, S, D = x.shape
    T = B * S
    assert D == D_MODEL and S % BLOCK == 0 and T % 256 == 0
    assert w_in.shape[0] == DEPTH
    tm, ts, tt, ec = _tiles(S, T)
    row = lambda v: v.reshape(1, -1).astype(F32)

    rc, rsa, rsb = _rope_tables(positions)
    h1, h1t, s1t, s2t = _front(
        x.reshape(T, D), rc, rsa, rsb, attn_sinks[0].astype(F32),
        w_in[0].astype(BF16), row(b_in[0]), conv_w[0].astype(F32),
        w_out[0].astype(BF16), row(b_out[0]), row(ln1_g[0]), row(ln1_b[0]),
        w_pq[0].astype(BF16), sub_keys1[0].astype(BF16), sub_keys2[0].astype(BF16),
        seq=S, tm=tm)
    r2, e2, n1, e1 = _select(s1t, s2t, ts=ts)
    out = _peer(u_experts[0].astype(BF16), v_experts[0].T.astype(BF16), h1t,
                r2, e2, n1, e1, h1, row(ln2_g[0]), row(ln2_b[0]), ec=ec, tt=tt)
    return out.reshape(B, S, D)
```

```python
import functools

import jax
import jax.numpy as jnp
from jax import lax
from jax.experimental import pallas as pl
from jax.experimental.pallas import tpu as pltpu

D_MODEL = 1024
HEAD_DIM = 64
N_Q_HEADS = 8
N_KV_HEADS = 2
ATTN_WIDTH = N_Q_HEADS * HEAD_DIM
KV_WIDTH = N_KV_HEADS * HEAD_DIM
BLOCK = 128
ROT_DIM = HEAD_DIM // 4
ROPE_THETA = 500000.0
CONV_WIDTH = D_MODEL - ATTN_WIDTH
CONV_K = 3
N_KEYS = 128
PEER_HEADS = 8
PEER_QDIM = 256
PEER_HALF = PEER_QDIM // 2
PEER_TOPK = 16
DEPTH = 1
DN_ALPHA = (2.0 * DEPTH) ** 0.25
LN_EPS = 1e-5

LANES = 128
BF16_TILE_ROWS = 16
PEER_DOT_COLS = 256
GATE_PAD = 2 * LANES
VMEM_LIMIT = 56 * 1024 * 1024

_O_K = ATTN_WIDTH
_O_V = _O_K + KV_WIDTH
_O_GB = _O_V + KV_WIDTH
_O_GC = _O_GB + CONV_WIDTH
_O_XC = _O_GC + CONV_WIDTH
_O_END = _O_XC + CONV_WIDTH

BF16 = jnp.bfloat16
F32 = jnp.float32
NEG_INF = float("-inf")


def _dot(a, b):
    return jnp.dot(a, b, preferred_element_type=F32)


def _dot_nt(a, b):
    return lax.dot_general(a, b, (((1,), (1,)), ((), ())), preferred_element_type=F32)


def _layer_norm(z, g, b):
    mu = jnp.mean(z, axis=-1, keepdims=True)
    zc = z - mu
    var = jnp.mean(zc * zc, axis=-1, keepdims=True)
    return zc * lax.rsqrt(var + LN_EPS) * g + b


def _front_kernel(sinks_ref, x_ref, rc_ref, rsa_ref, rsb_ref, w_in_ref, b_in_ref,
                  conv_w_ref, w_out_ref, b_out_ref, g1_ref, be1_ref, w_pq_ref,
                  k1_ref, k2_ref,
                  h1_ref, h1t_ref, s1_ref, s2_ref,
                  kbuf, vbuf, ubuf, kvar, vvar, ybuf, *, tm, tiles_per_seq):
    step = pl.program_id(0)
    nblk = tm // BLOCK

    @pl.when(step % tiles_per_seq == 0)
    def _():
        kbuf[0:BLOCK, :] = jnp.zeros((BLOCK, KV_WIDTH), F32)
        vbuf[0:BLOCK, :] = jnp.zeros((BLOCK, KV_WIDTH), F32)
        ubuf[0:8, :] = jnp.zeros((8, CONV_WIDTH), F32)

    x = x_ref[...]
    xb = x.astype(BF16)

    def proj(lo, hi):
        return _dot(xb, w_in_ref[:, lo:hi]) + b_in_ref[:, lo:hi]

    rc = rc_ref[...]
    rsa = rsa_ref[...]
    rsb = rsb_ref[...]

    def rope(t):
        return (t * rc + pltpu.roll(t, LANES - ROT_DIM // 2, 1) * rsa
                + pltpu.roll(t, ROT_DIM // 2, 1) * rsb)

    kbuf[BLOCK:BLOCK + tm, :] = rope(proj(_O_K, _O_V))
    vbuf[BLOCK:BLOCK + tm, :] = proj(_O_V, _O_GB)

    lane = lax.broadcasted_iota(jnp.int32, (tm + BLOCK, LANES), 1)
    lo = lane < HEAD_DIM
    for buf, var in ((kbuf, kvar), (vbuf, vvar)):
        t = buf[...]
        tr = pltpu.roll(t, HEAD_DIM, 1)
        var[0] = jnp.where(lo, t, 0.0).astype(BF16)
        var[1] = jnp.where(lo, 0.0, tr).astype(BF16)
        var[2] = jnp.where(lo, tr, 0.0).astype(BF16)
        var[3] = jnp.where(lo, 0.0, t).astype(BF16)

    qi = lax.broadcasted_iota(jnp.int32, (BLOCK, 2 * BLOCK), 0)
    kj = lax.broadcasted_iota(jnp.int32, (BLOCK, 2 * BLOCK), 1)
    diff = qi + BLOCK - kj
    band = (diff >= 0) & (diff < BLOCK)

    for p in range(N_Q_HEADS // 2):
        q_slab = (rope(proj(p * LANES, (p + 1) * LANES)) * (HEAD_DIM ** -0.5)).astype(BF16)
        for blk in range(nblk):
            first = (step * nblk + blk) % (tiles_per_seq * nblk) == 0
            kmin = jnp.where(first, BLOCK, 0)
            mask = band & (kj >= kmin)
            qs = q_slab[blk * BLOCK:(blk + 1) * BLOCK, :]
            o = jnp.zeros((BLOCK, LANES), F32)
            for hh in range(2):
                h = 2 * p + hh
                var = (h // (N_Q_HEADS // N_KV_HEADS)) * 2 + hh
                kc = kvar[var, blk * BLOCK:(blk + 2) * BLOCK, :]
                vc = vvar[var, blk * BLOCK:(blk + 2) * BLOCK, :]
                s = jnp.where(mask, _dot_nt(qs, kc), NEG_INF)
                sink = sinks_ref[h]
                m = jnp.maximum(jnp.max(s, axis=-1, keepdims=True), sink)
                pe = jnp.exp(s - m)
                den = jnp.sum(pe, axis=-1, keepdims=True) + jnp.exp(sink - m)
                o = o + _dot((pe / den).astype(BF16), vc)
            ybuf[blk * BLOCK:(blk + 1) * BLOCK, p * LANES:(p + 1) * LANES] = o.astype(BF16)

    u = proj(_O_GC, _O_XC) * proj(_O_XC, _O_END)
    ubuf[8:8 + tm, :] = u
    um1 = ubuf[7:7 + tm, :]
    um2 = ubuf[6:6 + tm, :]
    cw = conv_w_ref[...]
    yc = proj(_O_GB, _O_GC) * (cw[0:1, :] * um2 + cw[1:2, :] * um1 + cw[2:3, :] * u)
    ybuf[:, ATTN_WIDTH:] = yc.astype(BF16)

    ubuf[0:8, :] = ubuf[tm:tm + 8, :]
    kbuf[0:BLOCK, :] = kbuf[tm:tm + BLOCK, :]
    vbuf[0:BLOCK, :] = vbuf[tm:tm + BLOCK, :]

    mix = _dot(ybuf[...], w_out_ref[...]) + b_out_ref[...]
    h1 = _layer_norm(DN_ALPHA * x + mix, g1_ref[...], be1_ref[...])
    h1_ref[...] = h1
    h1t_ref[...] = h1.T.astype(BF16)

    qp = _dot(h1.astype(BF16), w_pq_ref[...]).astype(BF16)
    k1 = k1_ref[...]
    k2 = k2_ref[...]
    for h in range(PEER_HEADS):
        base = h * PEER_QDIM
        s1_ref[h] = _dot_nt(k1, qp[:, base:base + PEER_HALF])
        s2_ref[h] = _dot_nt(k2, qp[:, base + PEER_HALF:base + PEER_QDIM])


def _front(xf, rc, rsa, rsb, sinks, w_in, b_in, conv_w, w_out, b_out, g1, be1, w_pq, k1, k2,
           *, seq, tm):
    T = xf.shape[0]
    proj_w = w_in.shape[1]
    tiles_per_seq = seq // tm
    full = lambda shape: pl.BlockSpec(shape, lambda i: (0,) * len(shape))
    tok = lambda w: pl.BlockSpec((tm, w), lambda i: (i, 0))
    kern = functools.partial(_front_kernel, tm=tm, tiles_per_seq=tiles_per_seq)
    return pl.pallas_call(
        kern,
        grid=(T // tm,),
        in_specs=[
            pl.BlockSpec(memory_space=pltpu.SMEM),
            tok(D_MODEL), tok(LANES), tok(LANES), tok(LANES),
            full((D_MODEL, proj_w)), full((1, proj_w)),
            full((CONV_K, CONV_WIDTH)), full((D_MODEL, D_MODEL)), full((1, D_MODEL)),
            full((1, D_MODEL)), full((1, D_MODEL)),
            full((D_MODEL, PEER_HEADS * PEER_QDIM)),
            full((N_KEYS, PEER_HALF)), full((N_KEYS, PEER_HALF)),
        ],
        out_specs=[
            tok(D_MODEL),
            pl.BlockSpec((D_MODEL, tm), lambda i: (0, i)),
            pl.BlockSpec((PEER_HEADS, N_KEYS, tm), lambda i: (0, 0, i)),
            pl.BlockSpec((PEER_HEADS, N_KEYS, tm), lambda i: (0, 0, i)),
        ],
        out_shape=[
            jax.ShapeDtypeStruct((T, D_MODEL), F32),
            jax.ShapeDtypeStruct((D_MODEL, T), BF16),
            jax.ShapeDtypeStruct((PEER_HEADS, N_KEYS, T), F32),
            jax.ShapeDtypeStruct((PEER_HEADS, N_KEYS, T), F32),
        ],
        scratch_shapes=[
            pltpu.VMEM((tm + BLOCK, KV_WIDTH), F32),
            pltpu.VMEM((tm + BLOCK, KV_WIDTH), F32),
            pltpu.VMEM((tm + 8, CONV_WIDTH), F32),
            pltpu.VMEM((4, tm + BLOCK, KV_WIDTH), BF16),
            pltpu.VMEM((4, tm + BLOCK, KV_WIDTH), BF16),
            pltpu.VMEM((tm, D_MODEL), BF16),
        ],
        compiler_params=pltpu.CompilerParams(
            dimension_semantics=("arbitrary",), vmem_limit_bytes=VMEM_LIMIT),
        name="front",
    )(sinks, xf, rc, rsa, rsb, w_in, b_in, conv_w, w_out, b_out, g1, be1, w_pq, k1, k2)


def _top16(s):
    rows = lax.broadcasted_iota(jnp.int32, s.shape, 0)
    rem = s
    rank = jnp.full(s.shape, float(PEER_TOPK), F32)
    vals = []
    for k in range(PEER_TOPK):
        m = jnp.max(rem, axis=0, keepdims=True)
        idx = jnp.min(jnp.where(rem == m, rows, N_KEYS), axis=0, keepdims=True)
        sel = rows == idx
        rem = jnp.where(sel, NEG_INF, rem)
        rank = jnp.where(sel, float(k), rank)
        vals.append(m)
    return vals, rank


_N_CAND_ROWS = PEER_TOPK + 7 * 8 + 8


def _cand_positions():
    import numpy as np
    pos = np.full((_N_CAND_ROWS,), -1, np.int32)
    for r in range(_N_CAND_ROWS):
        if r < 16:
            a, b = 0, r
        elif r < 72:
            a, b = 1 + (r - 16) // 8, (r - 16) % 8
        else:
            a, b = 8 + (r - 72), 0
        if (a + 1) * (b + 1) <= PEER_TOPK:
            pos[r] = a * PEER_TOPK + b
    return pos


def _select_exact(s1, s2, pos, valid, row16):
    v1, rank1 = _top16(s1)
    v2, rank2 = _top16(s2)

    v1m = jnp.zeros((PEER_TOPK, LANES), F32)
    v2m = jnp.zeros((PEER_TOPK, LANES), F32)
    for a in range(PEER_TOPK):
        v1m = jnp.where(row16 == a, v1[a], v1m)
        v2m = jnp.where(row16 == a, v2[a], v2m)
    blocks = [v1[0] + v2m]
    for a in range(1, 8):
        blocks.append(v1[a] + v2m[0:8, :])
    blocks.append(v1m[8:16, :] + v2[0])
    cand = jnp.where(valid, jnp.concatenate(blocks, axis=0), NEG_INF)

    rem = cand
    chosen = jnp.zeros(cand.shape, F32)
    for _ in range(PEER_TOPK):
        m = jnp.max(rem, axis=0, keepdims=True)
        idx = jnp.min(jnp.where(rem == m, pos, 1 << 20), axis=0, keepdims=True)
        sel = (pos == idx) & valid
        rem = jnp.where(sel, NEG_INF, rem)
        chosen = jnp.where(sel, 1.0, chosen)

    top = v1[0] + v2[0]
    z = jnp.sum(chosen * jnp.exp(jnp.where(valid, cand, top) - top), axis=0, keepdims=True)

    counts = [jnp.sum(chosen[0:16, :], axis=0, keepdims=True)]
    for a in range(1, 8):
        counts.append(jnp.sum(chosen[8 + 8 * a:16 + 8 * a, :], axis=0, keepdims=True))
    for a in range(8, 16):
        counts.append(chosen[64 + a:65 + a, :])
    n1 = jnp.zeros(s1.shape, F32)
    for a in range(PEER_TOPK):
        n1 = jnp.where(rank1 == float(a), counts[a], n1)
    return n1, jnp.exp(s1 - v1[0]), rank2, jnp.exp(s2 - v2[0]) / z


def _sort_pairs(n):
    pairs = []
    p = 1
    while p < n:
        k = p
        while k >= 1:
            for j in range(k % p, n - k, 2 * k):
                for i in range(min(k, n - j - k)):
                    if (i + j) // (2 * p) == (i + j + k) // (2 * p):
                        pairs.append((i + j, i + j + k))
            k //= 2
        p *= 2
    return pairs


def _hi_lo(a, b):
    if a is None:
        return b, None
    if b is None:
        return a, None
    return jnp.maximum(a, b), jnp.minimum(a, b)


def _sublane_all(x, op):
    for shift in (4, 2, 1):
        x = op(x, pltpu.roll(x, shift, 0))
    return x


def _top16_sorted(x):
    x = list(x)
    for i, j in _sort_pairs(len(x)):
        x[i], x[j] = _hi_lo(x[i], x[j])
    for shift in (4, 2, 1):
        y = [None if t is None else pltpu.roll(t, shift, 0) for t in x]
        c = [_hi_lo(x[r], y[PEER_TOPK - 1 - r])[0] for r in range(PEER_TOPK)]
        d = PEER_TOPK // 2
        while d >= 1:
            for i in range(PEER_TOPK):
                if not i & d:
                    c[i], c[i + d] = _hi_lo(c[i], c[i + d])
            d //= 2
        x = c
    return x


def _select_fast(s1, s2):
    nv = N_KEYS // 8
    t1 = [s1[8 * v:8 * v + 8, :] for v in range(nv)]
    t2 = [s2[8 * v:8 * v + 8, :] for v in range(nv)]
    v1 = _top16_sorted(t1)
    v2 = _top16_sorted(t2)
    sub = lax.broadcasted_iota(jnp.int32, (8, LANES), 0)
    one = jnp.ones((8, LANES), F32)
    zero = jnp.zeros((8, LANES), F32)

    def spread(vals):
        out = vals[0]
        for r in range(1, 8):
            out = jnp.where(sub == r, vals[r], out)
        return out

    v2lo, v2hi, v1hi = spread(v2[0:8]), spread(v2[8:16]), spread(v1[8:16])
    cand = [v1[0] + v2lo, v1[0] + v2hi]
    for a in range(1, 8):
        cand.append(jnp.where(sub < PEER_TOPK // (a + 1), v1[a] + v2lo, NEG_INF))
    cand.append(v1hi + v2[0])
    best = _top16_sorted(cand + [None] * (PEER_TOPK - len(cand)))
    tau = best[PEER_TOPK - 1]
    z = one
    for k in range(1, PEER_TOPK):
        z = z + jnp.exp(best[k] - best[0])

    count = lambda t: _sublane_all(jnp.where(t >= tau, one, zero), jnp.add)
    counts = [count(cand[0]) + count(cand[1])]
    counts += [count(cand[a + 1]) for a in range(1, 8)]
    counts += [jnp.where(v1[a] + v2[0] >= tau, one, zero) for a in range(8, PEER_TOPK)]
    total = counts[0]
    for a in range(1, PEER_TOPK):
        total = total + counts[a]
    tie = jnp.where(total != float(PEER_TOPK), one, zero)

    n1 = [zero] * nv
    rank2 = [jnp.full((8, LANES), float(PEER_TOPK), F32)] * nv
    for a in reversed(range(PEER_TOPK)):
        n1 = [jnp.where(t >= v1[a], counts[a], n) for t, n in zip(t1, n1)]
        rank2 = [jnp.where(t >= v2[a], float(a), r) for t, r in zip(t2, rank2)]
    for v, tiles in ((v1, t1), (v2, t2)):
        inside = zero
        for t in tiles:
            inside = inside + jnp.where(t >= v[PEER_TOPK - 1], one, zero)
        tie = jnp.where(_sublane_all(inside, jnp.add) != float(PEER_TOPK), one, tie)
        for a in range(PEER_TOPK - 1):
            tie = jnp.where(v[a] == v[a + 1], one, tie)

    inv_z = 1.0 / z
    cat = lambda tiles: jnp.concatenate(tiles, axis=0)
    e1 = cat([jnp.exp(t - v1[0]) for t in t1])
    e2 = cat([jnp.exp(t - v2[0]) * inv_z for t in t2])
    return cat(n1), e1, cat(rank2), e2, tie


def _select_kernel(pos_ref, s1_ref, s2_ref, r2_ref, e2_ref, n1_ref, e1_ref, *, ts):
    def group(g, carry):
        sl = pl.ds(pl.multiple_of(g * LANES, LANES), LANES)
        s1 = s1_ref[0, :, sl]
        s2 = s2_ref[0, :, sl]
        n1, e1, rank2, e2, tie = _select_fast(s1, s2)

        def exact():
            pos = pos_ref[...]
            row16 = lax.broadcasted_iota(jnp.int32, (PEER_TOPK, LANES), 0)
            return _select_exact(s1, s2, pos, pos >= 0, row16)

        n1, e1, rank2, e2 = lax.cond(jnp.max(tie) > 0.0, exact, lambda: (n1, e1, rank2, e2))
        n1_ref[:, 0, :, sl] = n1.reshape(N_KEYS // 8, 8, LANES)
        e1_ref[:, 0, :, sl] = e1.reshape(N_KEYS // 8, 8, LANES)
        r2_ref[0, :, sl] = rank2.astype(BF16)
        e2_ref[0, :, sl] = e2.astype(BF16)
        return carry

    lax.fori_loop(0, ts // LANES, group, 0)


def _select(s1t, s2t, *, ts):
    H, K, T = s1t.shape
    import numpy as np
    pos = jnp.asarray(np.tile(_cand_positions()[:, None], (1, LANES)))
    blk = pl.BlockSpec((1, K, ts), lambda t, h: (h, 0, t))
    blk2 = blk
    blk1 = pl.BlockSpec((K // 8, 1, 8, ts), lambda t, h: (0, h, 0, t))
    return pl.pallas_call(
        functools.partial(_select_kernel, ts=ts),
        grid=(T // ts, H),
        in_specs=[pl.BlockSpec((_N_CAND_ROWS, LANES), lambda t, h: (0, 0)), blk, blk],
        out_specs=[blk2, blk2, blk1, blk1],
        out_shape=[
            jax.ShapeDtypeStruct((H, K, T), BF16),
            jax.ShapeDtypeStruct((H, K, T), BF16),
            jax.ShapeDtypeStruct((K // 8, H, 8, T), F32),
            jax.ShapeDtypeStruct((K // 8, H, 8, T), F32),
        ],
        compiler_params=pltpu.CompilerParams(
            dimension_semantics=("arbitrary", "arbitrary"), vmem_limit_bytes=VMEM_LIMIT),
        name="select",
    )(pos, s1t, s2t)


def _peer_kernel(u_ref, vt_ref, xt_ref, r2_in, e2_in, n1_in, e1_in, h1_ref, g2_ref, be2_ref,
                 out_ref, acc_ref, a0_ref, a1_ref, w0_ref, w1_ref,
                 r2_ref, e2_ref, n1_ref, e1_ref, *, ec, tt, nchunks):
    s = pl.program_id(1)
    nkb = ec // N_KEYS
    assert nkb == 8
    rows = BF16_TILE_ROWS

    @pl.when(s == 0)
    def _():
        acc_ref[...] = jnp.zeros(acc_ref.shape, F32)
        for ref in (a0_ref, a1_ref, w0_ref, w1_ref):
            ref[...] = jnp.zeros(ref.shape, ref.dtype)
        r2_ref[:, :, 0:tt] = r2_in[...]
        e2_ref[:, :, LANES:LANES + tt] = e2_in[...]
        n1_ref[...] = n1_in[...]
        e1_ref[...] = e1_in[...]

    cg = jnp.clip(s - 1, 0, nchunks - 1)

    def stages(a_new, a_prev, w_new, w_old):
        def row_bf16(ref, h, ib, sl):
            return jnp.broadcast_to(ref[cg, h, ib:ib + 1, sl], (rows, LANES)).astype(BF16)

        def gate_unit(kp, lt):
            lsl = slice(lt * LANES, (lt + 1) * LANES)
            nrg = N_KEYS // rows
            keys = (2 * kp, 2 * kp + 1)
            g = [[None] * nrg for _ in keys]
            for h in range(PEER_HEADS):
                n1b = [row_bf16(n1_ref, h, ib, lsl) for ib in keys]
                e1b = [row_bf16(e1_ref, h, ib, lsl) for ib in keys]
                for rg in range(nrg):
                    wsl = slice(rg * rows, (rg + 1) * rows)
                    r2 = r2_ref[h, wsl, lsl]
                    e2 = e2_ref[h, wsl, (lt + 1) * LANES:(lt + 2) * LANES]
                    for k in range(2):
                        term = jnp.where(r2 < n1b[k], e2, jnp.zeros_like(e2)) * e1b[k]
                        g[k][rg] = term if h == 0 else g[k][rg] + term
            for k, ib in enumerate(keys):
                for rg in range(nrg):
                    esl = slice(ib * N_KEYS + rg * rows, ib * N_KEYS + (rg + 1) * rows)
                    ab = a_prev[esl, lsl]
                    ge = 0.5 * ab * (1.0 + lax.erf(ab * (2.0 ** -0.5)))
                    w_new[esl, lsl] = ge.astype(BF16) * g[k][rg]

        def out_piece(p):
            nsl = slice(p * PEER_DOT_COLS, (p + 1) * PEER_DOT_COLS)
            acc_ref[:, nsl] += _dot(vt_ref[...], w_old[:, nsl])

        def score_piece(p):
            nsl = slice(p * PEER_DOT_COLS, (p + 1) * PEER_DOT_COLS)
            a_new[:, nsl] = _dot(u_ref[...], xt_ref[:, nsl])

        units = [(kp, lt) for kp in range(nkb // 2) for lt in range(tt // LANES)]
        npieces = tt // PEER_DOT_COLS
        pieces = [f for p in range(npieces) for f in ((out_piece, p), (score_piece, p))]
        per = len(units) // len(pieces)
        assert per * len(pieces) == len(units)
        for n, (fn, p) in enumerate(pieces):
            for kp, lt in units[n * per:(n + 1) * per]:
                gate_unit(kp, lt)
            fn(p)


    @pl.when(s % 2 == 0)
    def _():
        stages(a0_ref, a1_ref, w1_ref, w0_ref)

    @pl.when(s % 2 == 1)
    def _():
        stages(a1_ref, a0_ref, w0_ref, w1_ref)

    @pl.when(s == nchunks + 1)
    def _():
        z = DN_ALPHA * h1_ref[...] + acc_ref[...].T
        out_ref[...] = _layer_norm(z, g2_ref[...], be2_ref[...])


def _peer(u_bf, vt_bf, h1t, r2, e2, n1, e1, h1, g2, be2, *, ec, tt):
    E = u_bf.shape[0]
    T = h1.shape[0]
    nchunks = E // ec
    last = nchunks - 1
    tokrows = pl.BlockSpec((PEER_HEADS, N_KEYS, tt), lambda t, s: (0, 0, t))
    tokkeys = pl.BlockSpec((N_KEYS // 8, PEER_HEADS, 8, tt), lambda t, s: (0, 0, 0, t))
    vec = pl.BlockSpec((1, D_MODEL), lambda t, s: (0, 0))
    return pl.pallas_call(
        functools.partial(_peer_kernel, ec=ec, tt=tt, nchunks=nchunks),
        grid=(T // tt, nchunks + 2),
        in_specs=[
            pl.BlockSpec((ec, D_MODEL), lambda t, s: (jnp.minimum(s, last), 0)),
            pl.BlockSpec((D_MODEL, ec), lambda t, s: (0, jnp.clip(s - 2, 0, last))),
            pl.BlockSpec((D_MODEL, tt), lambda t, s: (0, t)),
            tokrows, tokrows, tokkeys, tokkeys,
            pl.BlockSpec((tt, D_MODEL), lambda t, s: (t, 0)),
            vec, vec,
        ],
        out_specs=pl.BlockSpec((tt, D_MODEL), lambda t, s: (t, 0)),
        out_shape=jax.ShapeDtypeStruct((T, D_MODEL), F32),
        scratch_shapes=[
            pltpu.VMEM((D_MODEL, tt), F32),
            pltpu.VMEM((ec, tt), F32),
            pltpu.VMEM((ec, tt), F32),
            pltpu.VMEM((ec, tt), BF16),
            pltpu.VMEM((ec, tt), BF16),
            pltpu.VMEM((PEER_HEADS, N_KEYS, tt + GATE_PAD), BF16),
            pltpu.VMEM((PEER_HEADS, N_KEYS, tt + GATE_PAD), BF16),
            pltpu.VMEM((N_KEYS // 8, PEER_HEADS, 8, tt), F32),
            pltpu.VMEM((N_KEYS // 8, PEER_HEADS, 8, tt), F32),
        ],
        compiler_params=pltpu.CompilerParams(
            dimension_semantics=("arbitrary", "arbitrary"), vmem_limit_bytes=VMEM_LIMIT),
        name="peer",
    )(u_bf, vt_bf, h1t, r2, e2, n1, e1, h1, g2, be2)


def _peer_dense_kernel(u_ref, vt_ref, xt_ref, r2_in, e2_in, n1_in, e1_in, h1_ref, g2_ref, be2_ref,
                       out_ref, acc_ref, a_ref, w_ref, r2_ref, e2_ref, *, ec, tt):
    c = pl.program_id(1)
    nkb = ec // N_KEYS
    assert nkb == 8
    rows = BF16_TILE_ROWS
    nrg = N_KEYS // rows

    @pl.when(c == 0)
    def _():
        acc_ref[...] = jnp.zeros(acc_ref.shape, F32)
        r2_ref[:, :, 0:tt] = r2_in[...]
        e2_ref[:, :, LANES:LANES + tt] = e2_in[...]

    a_ref[...] = _dot(u_ref[...], xt_ref[...])

    def row_bf16(ref, h, ib, sl):
        return jnp.broadcast_to(ref[c, h, ib:ib + 1, sl], (rows, LANES)).astype(BF16)

    def gate_unit(kp, lt):
        lsl = slice(lt * LANES, (lt + 1) * LANES)
        keys = (2 * kp, 2 * kp + 1)
        g = [[None] * nrg for _ in keys]
        for h in range(PEER_HEADS):
            n1b = [row_bf16(n1_in, h, ib, lsl) for ib in keys]
            e1b = [row_bf16(e1_in, h, ib, lsl) for ib in keys]
            for rg in range(nrg):
                wsl = slice(rg * rows, (rg + 1) * rows)
                r2 = r2_ref[h, wsl, lsl]
                e2 = e2_ref[h, wsl, (lt + 1) * LANES:(lt + 2) * LANES]
                for k in range(2):
                    term = jnp.where(r2 < n1b[k], e2, jnp.zeros_like(e2)) * e1b[k]
                    g[k][rg] = term if h == 0 else g[k][rg] + term
        for k, ib in enumerate(keys):
            for rg in range(nrg):
                esl = slice(ib * N_KEYS + rg * rows, ib * N_KEYS + (rg + 1) * rows)
                ab = a_ref[esl, lsl]
                ge = 0.5 * ab * (1.0 + lax.erf(ab * (2.0 ** -0.5)))
                w_ref[esl, lsl] = ge.astype(BF16) * g[k][rg]

    for lt in range(tt // LANES):
        for kp in range(nkb // 2):
            gate_unit(kp, lt)

    acc_ref[...] += _dot(vt_ref[...], w_ref[...])

    @pl.when(c == pl.num_programs(1) - 1)
    def _():
        z = DN_ALPHA * h1_ref[...] + acc_ref[...].T
        out_ref[...] = _layer_norm(z, g2_ref[...], be2_ref[...])


def _peer_dense(u_bf, vt_bf, h1t, r2, e2, n1, e1, h1, g2, be2, *, ec, tt):
    E = u_bf.shape[0]
    T = h1.shape[0]
    tokrows = pl.BlockSpec((PEER_HEADS, N_KEYS, tt), lambda t, c: (0, 0, t))
    tokkeys = pl.BlockSpec((N_KEYS // 8, PEER_HEADS, 8, tt), lambda t, c: (0, 0, 0, t))
    vec = pl.BlockSpec((1, D_MODEL), lambda t, c: (0, 0))
    return pl.pallas_call(
        functools.partial(_peer_dense_kernel, ec=ec, tt=tt),
        grid=(T // tt, E // ec),
        in_specs=[
            pl.BlockSpec((ec, D_MODEL), lambda t, c: (c, 0)),
            pl.BlockSpec((D_MODEL, ec), lambda t, c: (0, c)),
            pl.BlockSpec((D_MODEL, tt), lambda t, c: (0, t)),
            tokrows, tokrows, tokkeys, tokkeys,
            pl.BlockSpec((tt, D_MODEL), lambda t, c: (t, 0)),
            vec, vec,
        ],
        out_specs=pl.BlockSpec((tt, D_MODEL), lambda t, c: (t, 0)),
        out_shape=jax.ShapeDtypeStruct((T, D_MODEL), F32),
        scratch_shapes=[
            pltpu.VMEM((D_MODEL, tt), F32),
            pltpu.VMEM((ec, tt), F32),
            pltpu.VMEM((ec, tt), BF16),
            pltpu.VMEM((PEER_HEADS, N_KEYS, tt + GATE_PAD), BF16),
            pltpu.VMEM((PEER_HEADS, N_KEYS, tt + GATE_PAD), BF16),
        ],
        compiler_params=pltpu.CompilerParams(
            dimension_semantics=("arbitrary", "arbitrary"), vmem_limit_bytes=VMEM_LIMIT),
        name="peer",
    )(u_bf, vt_bf, h1t, r2, e2, n1, e1, h1, g2, be2)


def _rope_tables(positions):
    T = positions.size
    half = ROT_DIM // 2
    inv_freq = ROPE_THETA ** (-jnp.arange(0, ROT_DIM, 2, dtype=F32) / ROT_DIM)
    ang = positions.reshape(T, 1).astype(F32) * inv_freq
    cos, sin = jnp.cos(ang), jnp.sin(ang)
    zeros = lambda w: jnp.zeros((T, w), F32)
    rc = jnp.concatenate([cos, cos, jnp.ones((T, HEAD_DIM - ROT_DIM), F32)], axis=1)
    rsa = jnp.concatenate([-sin, zeros(HEAD_DIM - half)], axis=1)
    rsb = jnp.concatenate([zeros(half), sin, zeros(HEAD_DIM - ROT_DIM)], axis=1)
    rep = LANES // HEAD_DIM
    return jnp.tile(rc, (1, rep)), jnp.tile(rsa, (1, rep)), jnp.tile(rsb, (1, rep))


def _tiles(seq, total_tokens):
    tm = 256 if seq % 256 == 0 else BLOCK
    ts = 512 if total_tokens % 512 == 0 else LANES
    tt = 512 if total_tokens % 512 == 0 else 256
    ec = 8 * N_KEYS
    return tm, ts, tt, ec


def kernel(x, positions, w_in, b_in, attn_sinks, conv_w, w_out, b_out, ln1_g, ln1_b,
           w_pq, sub_keys1, sub_keys2, u_experts, v_experts, ln2_g, ln2_b):
    B, S, D = x.shape
    T = B * S
    assert D == D_MODEL and S % BLOCK == 0 and T % 256 == 0
    assert w_in.shape[0] == DEPTH
    tm, ts, tt, ec = _tiles(S, T)
    row = lambda v: v.reshape(1, -1).astype(F32)

    rc, rsa, rsb = _rope_tables(positions)
    h1, h1t, s1t, s2t = _front(
        x.reshape(T, D), rc, rsa, rsb, attn_sinks[0].astype(F32),
        w_in[0].astype(BF16), row(b_in[0]), conv_w[0].astype(F32),
        w_out[0].astype(BF16), row(b_out[0]), row(ln1_g[0]), row(ln1_b[0]),
        w_pq[0].astype(BF16), sub_keys1[0].astype(BF16), sub_keys2[0].astype(BF16),
        seq=S, tm=tm)
    r2, e2, n1, e1 = _select(s1t, s2t, ts=ts)
    out = _peer_dense(u_experts[0].astype(BF16), v_experts[0].T.astype(BF16), h1t,
                r2, e2, n1, e1, h1, row(ln2_g[0]), row(ln2_b[0]), ec=ec, tt=tt)
    return out.reshape(B, S, D)
```

```python
import functools

import jax
import jax.numpy as jnp
from jax import lax
from jax.experimental import pallas as pl
from jax.experimental.pallas import tpu as pltpu

D_MODEL = 1024
HEAD_DIM = 64
N_Q_HEADS = 8
N_KV_HEADS = 2
ATTN_WIDTH = N_Q_HEADS * HEAD_DIM
KV_WIDTH = N_KV_HEADS * HEAD_DIM
BLOCK = 128
ROT_DIM = HEAD_DIM // 4
ROPE_THETA = 500000.0
CONV_WIDTH = D_MODEL - ATTN_WIDTH
CONV_K = 3
N_KEYS = 128
PEER_HEADS = 8
PEER_QDIM = 256
PEER_HALF = PEER_QDIM // 2
PEER_TOPK = 16
DEPTH = 1
DN_ALPHA = (2.0 * DEPTH) ** 0.25
LN_EPS = 1e-5

LANES = 128
BF16_TILE_ROWS = 16
PEER_DOT_COLS = 256
PEER_SCORE_COLS = 512
GATE_PAD = 2 * LANES
VMEM_LIMIT = 56 * 1024 * 1024

_O_K = ATTN_WIDTH
_O_V = _O_K + KV_WIDTH
_O_GB = _O_V + KV_WIDTH
_O_GC = _O_GB + CONV_WIDTH
_O_XC = _O_GC + CONV_WIDTH
_O_END = _O_XC + CONV_WIDTH

BF16 = jnp.bfloat16
F32 = jnp.float32
NEG_INF = float("-inf")


def _dot(a, b):
    return jnp.dot(a, b, preferred_element_type=F32)


def _dot_nt(a, b):
    return lax.dot_general(a, b, (((1,), (1,)), ((), ())), preferred_element_type=F32)


def _layer_norm(z, g, b):
    mu = jnp.mean(z, axis=-1, keepdims=True)
    zc = z - mu
    var = jnp.mean(zc * zc, axis=-1, keepdims=True)
    return zc * lax.rsqrt(var + LN_EPS) * g + b


def _front_kernel(sinks_ref, x_ref, rc_ref, rsa_ref, rsb_ref, w_in_ref, b_in_ref,
                  conv_w_ref, w_out_ref, b_out_ref, g1_ref, be1_ref, w_pq_ref,
                  k1_ref, k2_ref,
                  h1_ref, h1t_ref, s1_ref, s2_ref,
                  kbuf, vbuf, ubuf, kvar, vvar, ybuf, *, tm, tiles_per_seq):
    step = pl.program_id(0)
    nblk = tm // BLOCK

    @pl.when(step % tiles_per_seq == 0)
    def _():
        kbuf[0:BLOCK, :] = jnp.zeros((BLOCK, KV_WIDTH), F32)
        vbuf[0:BLOCK, :] = jnp.zeros((BLOCK, KV_WIDTH), F32)
        ubuf[0:8, :] = jnp.zeros((8, CONV_WIDTH), F32)

    x = x_ref[...]
    xb = x.astype(BF16)

    def proj(lo, hi):
        return _dot(xb, w_in_ref[:, lo:hi]) + b_in_ref[:, lo:hi]

    rc = rc_ref[...]
    rsa = rsa_ref[...]
    rsb = rsb_ref[...]

    def rope(t):
        return (t * rc + pltpu.roll(t, LANES - ROT_DIM // 2, 1) * rsa
                + pltpu.roll(t, ROT_DIM // 2, 1) * rsb)

    kbuf[BLOCK:BLOCK + tm, :] = rope(proj(_O_K, _O_V))
    vbuf[BLOCK:BLOCK + tm, :] = proj(_O_V, _O_GB)

    lane = lax.broadcasted_iota(jnp.int32, (tm + BLOCK, LANES), 1)
    lo = lane < HEAD_DIM
    for buf, var in ((kbuf, kvar), (vbuf, vvar)):
        t = buf[...]
        tr = pltpu.roll(t, HEAD_DIM, 1)
        var[0] = jnp.where(lo, t, 0.0).astype(BF16)
        var[1] = jnp.where(lo, 0.0, tr).astype(BF16)
        var[2] = jnp.where(lo, tr, 0.0).astype(BF16)
        var[3] = jnp.where(lo, 0.0, t).astype(BF16)

    qi = lax.broadcasted_iota(jnp.int32, (BLOCK, 2 * BLOCK), 0)
    kj = lax.broadcasted_iota(jnp.int32, (BLOCK, 2 * BLOCK), 1)
    diff = qi + BLOCK - kj
    band = (diff >= 0) & (diff < BLOCK)

    for p in range(N_Q_HEADS // 2):
        q_slab = (rope(proj(p * LANES, (p + 1) * LANES)) * (HEAD_DIM ** -0.5)).astype(BF16)
        for blk in range(nblk):
            first = (step * nblk + blk) % (tiles_per_seq * nblk) == 0
            kmin = jnp.where(first, BLOCK, 0)
            mask = band & (kj >= kmin)
            qs = q_slab[blk * BLOCK:(blk + 1) * BLOCK, :]
            o = jnp.zeros((BLOCK, LANES), F32)
            for hh in range(2):
                h = 2 * p + hh
                var = (h // (N_Q_HEADS // N_KV_HEADS)) * 2 + hh
                kc = kvar[var, blk * BLOCK:(blk + 2) * BLOCK, :]
                vc = vvar[var, blk * BLOCK:(blk + 2) * BLOCK, :]
                s = jnp.where(mask, _dot_nt(qs, kc), NEG_INF)
                sink = sinks_ref[h]
                m = jnp.maximum(jnp.max(s, axis=-1, keepdims=True), sink)
                pe = jnp.exp(s - m)
                den = jnp.sum(pe, axis=-1, keepdims=True) + jnp.exp(sink - m)
                o = o + _dot((pe / den).astype(BF16), vc)
            ybuf[blk * BLOCK:(blk + 1) * BLOCK, p * LANES:(p + 1) * LANES] = o.astype(BF16)

    u = proj(_O_GC, _O_XC) * proj(_O_XC, _O_END)
    ubuf[8:8 + tm, :] = u
    um1 = ubuf[7:7 + tm, :]
    um2 = ubuf[6:6 + tm, :]
    cw = conv_w_ref[...]
    yc = proj(_O_GB, _O_GC) * (cw[0:1, :] * um2 + cw[1:2, :] * um1 + cw[2:3, :] * u)
    ybuf[:, ATTN_WIDTH:] = yc.astype(BF16)

    ubuf[0:8, :] = ubuf[tm:tm + 8, :]
    kbuf[0:BLOCK, :] = kbuf[tm:tm + BLOCK, :]
    vbuf[0:BLOCK, :] = vbuf[tm:tm + BLOCK, :]

    mix = _dot(ybuf[...], w_out_ref[...]) + b_out_ref[...]
    h1 = _layer_norm(DN_ALPHA * x + mix, g1_ref[...], be1_ref[...])
    h1_ref[...] = h1
    h1t_ref[...] = h1.T.astype(BF16)

    qp = _dot(h1.astype(BF16), w_pq_ref[...]).astype(BF16)
    k1 = k1_ref[...]
    k2 = k2_ref[...]
    for h in range(PEER_HEADS):
        base = h * PEER_QDIM
        s1_ref[h] = _dot_nt(k1, qp[:, base:base + PEER_HALF])
        s2_ref[h] = _dot_nt(k2, qp[:, base + PEER_HALF:base + PEER_QDIM])


def _front(xf, rc, rsa, rsb, sinks, w_in, b_in, conv_w, w_out, b_out, g1, be1, w_pq, k1, k2,
           *, seq, tm):
    T = xf.shape[0]
    proj_w = w_in.shape[1]
    tiles_per_seq = seq // tm
    full = lambda shape: pl.BlockSpec(shape, lambda i: (0,) * len(shape))
    tok = lambda w: pl.BlockSpec((tm, w), lambda i: (i, 0))
    kern = functools.partial(_front_kernel, tm=tm, tiles_per_seq=tiles_per_seq)
    return pl.pallas_call(
        kern,
        grid=(T // tm,),
        in_specs=[
            pl.BlockSpec(memory_space=pltpu.SMEM),
            tok(D_MODEL), tok(LANES), tok(LANES), tok(LANES),
            full((D_MODEL, proj_w)), full((1, proj_w)),
            full((CONV_K, CONV_WIDTH)), full((D_MODEL, D_MODEL)), full((1, D_MODEL)),
            full((1, D_MODEL)), full((1, D_MODEL)),
            full((D_MODEL, PEER_HEADS * PEER_QDIM)),
            full((N_KEYS, PEER_HALF)), full((N_KEYS, PEER_HALF)),
        ],
        out_specs=[
            tok(D_MODEL),
            pl.BlockSpec((D_MODEL, tm), lambda i: (0, i)),
            pl.BlockSpec((PEER_HEADS, N_KEYS, tm), lambda i: (0, 0, i)),
            pl.BlockSpec((PEER_HEADS, N_KEYS, tm), lambda i: (0, 0, i)),
        ],
        out_shape=[
            jax.ShapeDtypeStruct((T, D_MODEL), F32),
            jax.ShapeDtypeStruct((D_MODEL, T), BF16),
            jax.ShapeDtypeStruct((PEER_HEADS, N_KEYS, T), F32),
            jax.ShapeDtypeStruct((PEER_HEADS, N_KEYS, T), F32),
        ],
        scratch_shapes=[
            pltpu.VMEM((tm + BLOCK, KV_WIDTH), F32),
            pltpu.VMEM((tm + BLOCK, KV_WIDTH), F32),
            pltpu.VMEM((tm + 8, CONV_WIDTH), F32),
            pltpu.VMEM((4, tm + BLOCK, KV_WIDTH), BF16),
            pltpu.VMEM((4, tm + BLOCK, KV_WIDTH), BF16),
            pltpu.VMEM((tm, D_MODEL), BF16),
        ],
        compiler_params=pltpu.CompilerParams(
            dimension_semantics=("arbitrary",), vmem_limit_bytes=VMEM_LIMIT),
        name="front",
    )(sinks, xf, rc, rsa, rsb, w_in, b_in, conv_w, w_out, b_out, g1, be1, w_pq, k1, k2)


def _top16(s):
    rows = lax.broadcasted_iota(jnp.int32, s.shape, 0)
    rem = s
    rank = jnp.full(s.shape, float(PEER_TOPK), F32)
    vals = []
    for k in range(PEER_TOPK):
        m = jnp.max(rem, axis=0, keepdims=True)
        idx = jnp.min(jnp.where(rem == m, rows, N_KEYS), axis=0, keepdims=True)
        sel = rows == idx
        rem = jnp.where(sel, NEG_INF, rem)
        rank = jnp.where(sel, float(k), rank)
        vals.append(m)
    return vals, rank


_N_CAND_ROWS = PEER_TOPK + 7 * 8 + 8


def _cand_positions():
    import numpy as np
    pos = np.full((_N_CAND_ROWS,), -1, np.int32)
    for r in range(_N_CAND_ROWS):
        if r < 16:
            a, b = 0, r
        elif r < 72:
            a, b = 1 + (r - 16) // 8, (r - 16) % 8
        else:
            a, b = 8 + (r - 72), 0
        if (a + 1) * (b + 1) <= PEER_TOPK:
            pos[r] = a * PEER_TOPK + b
    return pos


def _select_exact(s1, s2, pos, valid, row16):
    v1, rank1 = _top16(s1)
    v2, rank2 = _top16(s2)

    v1m = jnp.zeros((PEER_TOPK, LANES), F32)
    v2m = jnp.zeros((PEER_TOPK, LANES), F32)
    for a in range(PEER_TOPK):
        v1m = jnp.where(row16 == a, v1[a], v1m)
        v2m = jnp.where(row16 == a, v2[a], v2m)
    blocks = [v1[0] + v2m]
    for a in range(1, 8):
        blocks.append(v1[a] + v2m[0:8, :])
    blocks.append(v1m[8:16, :] + v2[0])
    cand = jnp.where(valid, jnp.concatenate(blocks, axis=0), NEG_INF)

    rem = cand
    chosen = jnp.zeros(cand.shape, F32)
    for _ in range(PEER_TOPK):
        m = jnp.max(rem, axis=0, keepdims=True)
        idx = jnp.min(jnp.where(rem == m, pos, 1 << 20), axis=0, keepdims=True)
        sel = (pos == idx) & valid
        rem = jnp.where(sel, NEG_INF, rem)
        chosen = jnp.where(sel, 1.0, chosen)

    top = v1[0] + v2[0]
    z = jnp.sum(chosen * jnp.exp(jnp.where(valid, cand, top) - top), axis=0, keepdims=True)

    counts = [jnp.sum(chosen[0:16, :], axis=0, keepdims=True)]
    for a in range(1, 8):
        counts.append(jnp.sum(chosen[8 + 8 * a:16 + 8 * a, :], axis=0, keepdims=True))
    for a in range(8, 16):
        counts.append(chosen[64 + a:65 + a, :])
    n1 = jnp.zeros(s1.shape, F32)
    for a in range(PEER_TOPK):
        n1 = jnp.where(rank1 == float(a), counts[a], n1)
    return n1, jnp.exp(s1 - v1[0]), rank2, jnp.exp(s2 - v2[0]) / z


def _sort_pairs(n):
    pairs = []
    p = 1
    while p < n:
        k = p
        while k >= 1:
            for j in range(k % p, n - k, 2 * k):
                for i in range(min(k, n - j - k)):
                    if (i + j) // (2 * p) == (i + j + k) // (2 * p):
                        pairs.append((i + j, i + j + k))
            k //= 2
        p *= 2
    return pairs


def _hi_lo(a, b):
    if a is None:
        return b, None
    if b is None:
        return a, None
    return jnp.maximum(a, b), jnp.minimum(a, b)


def _sublane_all(x, op):
    for shift in (4, 2, 1):
        x = op(x, pltpu.roll(x, shift, 0))
    return x


def _top16_sorted(x):
    x = list(x)
    for i, j in _sort_pairs(len(x)):
        x[i], x[j] = _hi_lo(x[i], x[j])
    for shift in (4, 2, 1):
        y = [None if t is None else pltpu.roll(t, shift, 0) for t in x]
        c = [_hi_lo(x[r], y[PEER_TOPK - 1 - r])[0] for r in range(PEER_TOPK)]
        d = PEER_TOPK // 2
        while d >= 1:
            for i in range(PEER_TOPK):
                if not i & d:
                    c[i], c[i + d] = _hi_lo(c[i], c[i + d])
            d //= 2
        x = c
    return x


def _select_fast(s1, s2):
    nv = N_KEYS // 8
    t1 = [s1[8 * v:8 * v + 8, :] for v in range(nv)]
    t2 = [s2[8 * v:8 * v + 8, :] for v in range(nv)]
    v1 = _top16_sorted(t1)
    v2 = _top16_sorted(t2)
    sub = lax.broadcasted_iota(jnp.int32, (8, LANES), 0)
    one = jnp.ones((8, LANES), F32)
    zero = jnp.zeros((8, LANES), F32)

    def spread(vals):
        out = vals[0]
        for r in range(1, 8):
            out = jnp.where(sub == r, vals[r], out)
        return out

    v2lo, v2hi, v1hi = spread(v2[0:8]), spread(v2[8:16]), spread(v1[8:16])
    cand = [v1[0] + v2lo, v1[0] + v2hi]
    for a in range(1, 8):
        cand.append(jnp.where(sub < PEER_TOPK // (a + 1), v1[a] + v2lo, NEG_INF))
    cand.append(v1hi + v2[0])
    best = _top16_sorted(cand + [None] * (PEER_TOPK - len(cand)))
    tau = best[PEER_TOPK - 1]
    z = one
    for k in range(1, PEER_TOPK):
        z = z + jnp.exp(best[k] - best[0])

    count = lambda t: _sublane_all(jnp.where(t >= tau, one, zero), jnp.add)
    counts = [count(cand[0]) + count(cand[1])]
    counts += [count(cand[a + 1]) for a in range(1, 8)]
    counts += [jnp.where(v1[a] + v2[0] >= tau, one, zero) for a in range(8, PEER_TOPK)]
    total = counts[0]
    for a in range(1, PEER_TOPK):
        total = total + counts[a]
    tie = jnp.where(total != float(PEER_TOPK), one, zero)

    n1 = [zero] * nv
    rank2 = [jnp.full((8, LANES), float(PEER_TOPK), F32)] * nv
    for a in reversed(range(PEER_TOPK)):
        n1 = [jnp.where(t >= v1[a], counts[a], n) for t, n in zip(t1, n1)]
        rank2 = [jnp.where(t >= v2[a], float(a), r) for t, r in zip(t2, rank2)]
    for v, tiles in ((v1, t1), (v2, t2)):
        inside = zero
        for t in tiles:
            inside = inside + jnp.where(t >= v[PEER_TOPK - 1], one, zero)
        tie = jnp.where(_sublane_all(inside, jnp.add) != float(PEER_TOPK), one, tie)
        for a in range(PEER_TOPK - 1):
            tie = jnp.where(v[a] == v[a + 1], one, tie)

    inv_z = 1.0 / z
    cat = lambda tiles: jnp.concatenate(tiles, axis=0)
    e1 = cat([jnp.exp(t - v1[0]) for t in t1])
    e2 = cat([jnp.exp(t - v2[0]) * inv_z for t in t2])
    return cat(n1), e1, cat(rank2), e2, tie


def _select_kernel(pos_ref, s1_ref, s2_ref, r2_ref, e2_ref, n1_ref, e1_ref, *, ts):
    def group(g, carry):
        sl = pl.ds(pl.multiple_of(g * LANES, LANES), LANES)
        s1 = s1_ref[0, :, sl]
        s2 = s2_ref[0, :, sl]
        n1, e1, rank2, e2, tie = _select_fast(s1, s2)

        def exact():
            pos = pos_ref[...]
            row16 = lax.broadcasted_iota(jnp.int32, (PEER_TOPK, LANES), 0)
            return _select_exact(s1, s2, pos, pos >= 0, row16)

        n1, e1, rank2, e2 = lax.cond(jnp.max(tie) > 0.0, exact, lambda: (n1, e1, rank2, e2))
        n1_ref[:, 0, :, sl] = n1.reshape(N_KEYS // 8, 8, LANES)
        e1_ref[:, 0, :, sl] = e1.reshape(N_KEYS // 8, 8, LANES)
        r2_ref[0, :, sl] = rank2.astype(BF16)
        e2_ref[0, :, sl] = e2.astype(BF16)
        return carry

    lax.fori_loop(0, ts // LANES, group, 0)


def _select(s1t, s2t, *, ts):
    H, K, T = s1t.shape
    import numpy as np
    pos = jnp.asarray(np.tile(_cand_positions()[:, None], (1, LANES)))
    blk = pl.BlockSpec((1, K, ts), lambda t, h: (h, 0, t))
    blk2 = blk
    blk1 = pl.BlockSpec((K // 8, 1, 8, ts), lambda t, h: (0, h, 0, t))
    return pl.pallas_call(
        functools.partial(_select_kernel, ts=ts),
        grid=(T // ts, H),
        in_specs=[pl.BlockSpec((_N_CAND_ROWS, LANES), lambda t, h: (0, 0)), blk, blk],
        out_specs=[blk2, blk2, blk1, blk1],
        out_shape=[
            jax.ShapeDtypeStruct((H, K, T), BF16),
            jax.ShapeDtypeStruct((H, K, T), BF16),
            jax.ShapeDtypeStruct((K // 8, H, 8, T), F32),
            jax.ShapeDtypeStruct((K // 8, H, 8, T), F32),
        ],
        compiler_params=pltpu.CompilerParams(
            dimension_semantics=("arbitrary", "arbitrary"), vmem_limit_bytes=VMEM_LIMIT),
        name="select",
    )(pos, s1t, s2t)


def _peer_kernel(u_ref, vt_ref, xt_ref, r2_in, e2_in, n1_in, e1_in, h1_ref, g2_ref, be2_ref,
                 out_ref, acc_ref, a0_ref, a1_ref, w0_ref, w1_ref,
                 r2_ref, e2_ref, n1_ref, e1_ref, *, ec, tt, nchunks):
    s = pl.program_id(1)
    nkb = ec // N_KEYS
    assert nkb == 8
    rows = BF16_TILE_ROWS

    @pl.when(s == 0)
    def _():
        acc_ref[...] = jnp.zeros(acc_ref.shape, F32)
        for ref in (a0_ref, a1_ref, w0_ref, w1_ref):
            ref[...] = jnp.zeros(ref.shape, ref.dtype)
        r2_ref[:, :, 0:tt] = r2_in[...]
        e2_ref[:, :, LANES:LANES + tt] = e2_in[...]
        n1_ref[...] = n1_in[...]
        e1_ref[...] = e1_in[...]

    cg = jnp.clip(s - 1, 0, nchunks - 1)

    def stages(a_new, a_prev, w_new, w_old):
        def row_bf16(ref, h, ib, sl):
            return jnp.broadcast_to(ref[cg, h, ib:ib + 1, sl], (rows, LANES)).astype(BF16)

        def gate_unit(kp, lt):
            lsl = slice(lt * LANES, (lt + 1) * LANES)
            nrg = N_KEYS // rows
            keys = (2 * kp, 2 * kp + 1)
            g = [[None] * nrg for _ in keys]
            for h in range(PEER_HEADS):
                n1b = [row_bf16(n1_ref, h, ib, lsl) for ib in keys]
                e1b = [row_bf16(e1_ref, h, ib, lsl) for ib in keys]
                for rg in range(nrg):
                    wsl = slice(rg * rows, (rg + 1) * rows)
                    r2 = r2_ref[h, wsl, lsl]
                    e2 = e2_ref[h, wsl, (lt + 1) * LANES:(lt + 2) * LANES]
                    for k in range(2):
                        term = jnp.where(r2 < n1b[k], e2, jnp.zeros_like(e2)) * e1b[k]
                        g[k][rg] = term if h == 0 else g[k][rg] + term
            for k, ib in enumerate(keys):
                for rg in range(nrg):
                    esl = slice(ib * N_KEYS + rg * rows, ib * N_KEYS + (rg + 1) * rows)
                    ab = a_prev[esl, lsl]
                    ge = 0.5 * ab * (1.0 + lax.erf(ab * (2.0 ** -0.5)))
                    w_new[esl, lsl] = ge.astype(BF16) * g[k][rg]

        def out_piece(p):
            nsl = slice(p * PEER_DOT_COLS, (p + 1) * PEER_DOT_COLS)
            acc_ref[:, nsl] += _dot(vt_ref[...], w_old[:, nsl])

        def score_piece(p):
            nsl = slice(p * PEER_DOT_COLS, (p + 1) * PEER_DOT_COLS)
            a_new[:, nsl] = _dot(u_ref[...], xt_ref[:, nsl])

        units = [(kp, lt) for kp in range(nkb // 2) for lt in range(tt // LANES)]
        npieces = tt // PEER_DOT_COLS
        pieces = [f for p in range(npieces) for f in ((out_piece, p), (score_piece, p))]
        per = len(units) // len(pieces)
        assert per * len(pieces) == len(units)
        for n, (fn, p) in enumerate(pieces):
            for kp, lt in units[n * per:(n + 1) * per]:
                gate_unit(kp, lt)
            fn(p)


    @pl.when(s % 2 == 0)
    def _():
        stages(a0_ref, a1_ref, w1_ref, w0_ref)

    @pl.when(s % 2 == 1)
    def _():
        stages(a1_ref, a0_ref, w0_ref, w1_ref)

    @pl.when(s == nchunks + 1)
    def _():
        z = DN_ALPHA * h1_ref[...] + acc_ref[...].T
        out_ref[...] = _layer_norm(z, g2_ref[...], be2_ref[...])


def _peer(u_bf, vt_bf, h1t, r2, e2, n1, e1, h1, g2, be2, *, ec, tt):
    E = u_bf.shape[0]
    T = h1.shape[0]
    nchunks = E // ec
    last = nchunks - 1
    tokrows = pl.BlockSpec((PEER_HEADS, N_KEYS, tt), lambda t, s: (0, 0, t))
    tokkeys = pl.BlockSpec((N_KEYS // 8, PEER_HEADS, 8, tt), lambda t, s: (0, 0, 0, t))
    vec = pl.BlockSpec((1, D_MODEL), lambda t, s: (0, 0))
    return pl.pallas_call(
        functools.partial(_peer_kernel, ec=ec, tt=tt, nchunks=nchunks),
        grid=(T // tt, nchunks + 2),
        in_specs=[
            pl.BlockSpec((ec, D_MODEL), lambda t, s: (jnp.minimum(s, last), 0)),
            pl.BlockSpec((D_MODEL, ec), lambda t, s: (0, jnp.clip(s - 2, 0, last))),
            pl.BlockSpec((D_MODEL, tt), lambda t, s: (0, t)),
            tokrows, tokrows, tokkeys, tokkeys,
            pl.BlockSpec((tt, D_MODEL), lambda t, s: (t, 0)),
            vec, vec,
        ],
        out_specs=pl.BlockSpec((tt, D_MODEL), lambda t, s: (t, 0)),
        out_shape=jax.ShapeDtypeStruct((T, D_MODEL), F32),
        scratch_shapes=[
            pltpu.VMEM((D_MODEL, tt), F32),
            pltpu.VMEM((ec, tt), F32),
            pltpu.VMEM((ec, tt), F32),
            pltpu.VMEM((ec, tt), BF16),
            pltpu.VMEM((ec, tt), BF16),
            pltpu.VMEM((PEER_HEADS, N_KEYS, tt + GATE_PAD), BF16),
            pltpu.VMEM((PEER_HEADS, N_KEYS, tt + GATE_PAD), BF16),
            pltpu.VMEM((N_KEYS // 8, PEER_HEADS, 8, tt), F32),
            pltpu.VMEM((N_KEYS // 8, PEER_HEADS, 8, tt), F32),
        ],
        compiler_params=pltpu.CompilerParams(
            dimension_semantics=("arbitrary", "arbitrary"), vmem_limit_bytes=VMEM_LIMIT),
        name="peer",
    )(u_bf, vt_bf, h1t, r2, e2, n1, e1, h1, g2, be2)


def _peer_dense_kernel(u_ref, vt_ref, xt_ref, r2_in, e2_in, n1_in, e1_in, h1_ref, g2_ref, be2_ref,
                       out_ref, acc_ref, a_ref, w_ref, r2_ref, e2_ref, *, ec, tt):
    c = pl.program_id(1)
    nkb = ec // N_KEYS
    assert nkb == 8
    rows = BF16_TILE_ROWS
    nrg = N_KEYS // rows

    @pl.when(c == 0)
    def _():
        acc_ref[...] = jnp.zeros(acc_ref.shape, F32)
        r2_ref[:, :, 0:tt] = r2_in[...]
        e2_ref[:, :, LANES:LANES + tt] = e2_in[...]

    ta = a_ref.shape[1]

    def row_bf16(ref, h, ib, sl):
        return jnp.broadcast_to(ref[c, h, ib:ib + 1, sl], (rows, LANES)).astype(BF16)

    def gate_unit(kp, lt):
        lsl = slice(lt * LANES, (lt + 1) * LANES)
        keys = (2 * kp, 2 * kp + 1)
        g = [[None] * nrg for _ in keys]
        for h in range(PEER_HEADS):
            n1b = [row_bf16(n1_in, h, ib, lsl) for ib in keys]
            e1b = [row_bf16(e1_in, h, ib, lsl) for ib in keys]
            for rg in range(nrg):
                wsl = slice(rg * rows, (rg + 1) * rows)
                r2 = r2_ref[h, wsl, lsl]
                e2 = e2_ref[h, wsl, (lt + 1) * LANES:(lt + 2) * LANES]
                for k in range(2):
                    term = jnp.where(r2 < n1b[k], e2, jnp.zeros_like(e2)) * e1b[k]
                    g[k][rg] = term if h == 0 else g[k][rg] + term
        for k, ib in enumerate(keys):
            for rg in range(nrg):
                esl = slice(ib * N_KEYS + rg * rows, ib * N_KEYS + (rg + 1) * rows)
                ab = a_ref[esl, (lt * LANES) % ta:(lt * LANES) % ta + LANES]
                ge = 0.5 * ab * (1.0 + lax.erf(ab * (2.0 ** -0.5)))
                w_ref[esl, lsl] = ge.astype(BF16) * g[k][rg]

    for nt in range(tt // ta):
        a_ref[...] = _dot(u_ref[...], xt_ref[:, nt * ta:(nt + 1) * ta])
        for lt in range(nt * ta // LANES, (nt + 1) * ta // LANES):
            for kp in range(nkb // 2):
                gate_unit(kp, lt)

    acc_ref[...] += _dot(vt_ref[...], w_ref[...])

    @pl.when(c == pl.num_programs(1) - 1)
    def _():
        z = DN_ALPHA * h1_ref[...] + acc_ref[...].T
        out_ref[...] = _layer_norm(z, g2_ref[...], be2_ref[...])


def _peer_dense(u_bf, vt_bf, h1t, r2, e2, n1, e1, h1, g2, be2, *, ec, tt):
    E = u_bf.shape[0]
    T = h1.shape[0]
    once = pl.Buffered(1)
    tokrows = pl.BlockSpec((PEER_HEADS, N_KEYS, tt), lambda t, c: (0, 0, t), pipeline_mode=once)
    tokkeys = pl.BlockSpec((N_KEYS // 8, PEER_HEADS, 8, tt), lambda t, c: (0, 0, 0, t),
                           pipeline_mode=once)
    vec = pl.BlockSpec((1, D_MODEL), lambda t, c: (0, 0))
    return pl.pallas_call(
        functools.partial(_peer_dense_kernel, ec=ec, tt=tt),
        grid=(T // tt, E // ec),
        in_specs=[
            pl.BlockSpec((ec, D_MODEL), lambda t, c: (c, 0)),
            pl.BlockSpec((D_MODEL, ec), lambda t, c: (0, c)),
            pl.BlockSpec((D_MODEL, tt), lambda t, c: (0, t)),
            tokrows, tokrows, tokkeys, tokkeys,
            pl.BlockSpec((tt, D_MODEL), lambda t, c: (t, 0), pipeline_mode=once),
            vec, vec,
        ],
        out_specs=pl.BlockSpec((tt, D_MODEL), lambda t, c: (t, 0)),
        out_shape=jax.ShapeDtypeStruct((T, D_MODEL), F32),
        scratch_shapes=[
            pltpu.VMEM((D_MODEL, tt), F32),
            pltpu.VMEM((ec, min(tt, PEER_SCORE_COLS)), F32),
            pltpu.VMEM((ec, tt), BF16),
            pltpu.VMEM((PEER_HEADS, N_KEYS, tt + GATE_PAD), BF16),
            pltpu.VMEM((PEER_HEADS, N_KEYS, tt + GATE_PAD), BF16),
        ],
        compiler_params=pltpu.CompilerParams(
            dimension_semantics=("arbitrary", "arbitrary"), vmem_limit_bytes=VMEM_LIMIT),
        name="peer",
    )(u_bf, vt_bf, h1t, r2, e2, n1, e1, h1, g2, be2)


def _rope_tables(positions):
    T = positions.size
    half = ROT_DIM // 2
    inv_freq = ROPE_THETA ** (-jnp.arange(0, ROT_DIM, 2, dtype=F32) / ROT_DIM)
    ang = positions.reshape(T, 1).astype(F32) * inv_freq
    cos, sin = jnp.cos(ang), jnp.sin(ang)
    zeros = lambda w: jnp.zeros((T, w), F32)
    rc = jnp.concatenate([cos, cos, jnp.ones((T, HEAD_DIM - ROT_DIM), F32)], axis=1)
    rsa = jnp.concatenate([-sin, zeros(HEAD_DIM - half)], axis=1)
    rsb = jnp.concatenate([zeros(half), sin, zeros(HEAD_DIM - ROT_DIM)], axis=1)
    rep = LANES // HEAD_DIM
    return jnp.tile(rc, (1, rep)), jnp.tile(rsa, (1, rep)), jnp.tile(rsb, (1, rep))


def _tiles(seq, total_tokens):
    tm = 256 if seq % 256 == 0 else BLOCK
    ts = 512 if total_tokens % 512 == 0 else LANES
    tt = next(c for c in (1024, 512, 256) if total_tokens % c == 0)
    ec = 8 * N_KEYS
    return tm, ts, tt, ec


def kernel(x, positions, w_in, b_in, attn_sinks, conv_w, w_out, b_out, ln1_g, ln1_b,
           w_pq, sub_keys1, sub_keys2, u_experts, v_experts, ln2_g, ln2_b):
    B, S, D = x.shape
    T = B * S
    assert D == D_MODEL and S % BLOCK == 0 and T % 256 == 0
    assert w_in.shape[0] == DEPTH
    tm, ts, tt, ec = _tiles(S, T)
    row = lambda v: v.reshape(1, -1).astype(F32)

    rc, rsa, rsb = _rope_tables(positions)
    h1, h1t, s1t, s2t = _front(
        x.reshape(T, D), rc, rsa, rsb, attn_sinks[0].astype(F32),
        w_in[0].astype(BF16), row(b_in[0]), conv_w[0].astype(F32),
        w_out[0].astype(BF16), row(b_out[0]), row(ln1_g[0]), row(ln1_b[0]),
        w_pq[0].astype(BF16), sub_keys1[0].astype(BF16), sub_keys2[0].astype(BF16),
        seq=S, tm=tm)
    r2, e2, n1, e1 = _select(s1t, s2t, ts=ts)
    out = _peer_dense(u_experts[0].astype(BF16), v_experts[0].T.astype(BF16), h1t,
                r2, e2, n1, e1, h1, row(ln2_g[0]), row(ln2_b[0]), ec=ec, tt=tt)
    return out.reshape(B, S, D)
```

```python
import functools

import jax
import jax.numpy as jnp
from jax import lax
from jax.experimental import pallas as pl
from jax.experimental.pallas import tpu as pltpu

D_MODEL = 1024
HEAD_DIM = 64
N_Q_HEADS = 8
N_KV_HEADS = 2
ATTN_WIDTH = N_Q_HEADS * HEAD_DIM
KV_WIDTH = N_KV_HEADS * HEAD_DIM
BLOCK = 128
ROT_DIM = HEAD_DIM // 4
ROPE_THETA = 500000.0
CONV_WIDTH = D_MODEL - ATTN_WIDTH
CONV_K = 3
N_KEYS = 128
PEER_HEADS = 8
PEER_QDIM = 256
PEER_HALF = PEER_QDIM // 2
PEER_TOPK = 16
DEPTH = 1
DN_ALPHA = (2.0 * DEPTH) ** 0.25
LN_EPS = 1e-5

LANES = 128
BF16_TILE_ROWS = 16
PEER_DOT_COLS = 256
PEER_SCORE_COLS = 512
F32_TILE_ROWS = 8
GATE_UNIT_KEYS = 64
GATE_PAD = 2 * LANES
VMEM_LIMIT = 56 * 1024 * 1024

_O_K = ATTN_WIDTH
_O_V = _O_K + KV_WIDTH
_O_GB = _O_V + KV_WIDTH
_O_GC = _O_GB + CONV_WIDTH
_O_XC = _O_GC + CONV_WIDTH
_O_END = _O_XC + CONV_WIDTH

BF16 = jnp.bfloat16
F32 = jnp.float32
NEG_INF = float("-inf")


def _dot(a, b):
    return jnp.dot(a, b, preferred_element_type=F32)


def _dot_nt(a, b):
    return lax.dot_general(a, b, (((1,), (1,)), ((), ())), preferred_element_type=F32)


def _layer_norm(z, g, b):
    mu = jnp.mean(z, axis=-1, keepdims=True)
    zc = z - mu
    var = jnp.mean(zc * zc, axis=-1, keepdims=True)
    return zc * lax.rsqrt(var + LN_EPS) * g + b


def _front_kernel(sinks_ref, x_ref, rc_ref, rsa_ref, rsb_ref, w_in_ref, b_in_ref,
                  conv_w_ref, w_out_ref, b_out_ref, g1_ref, be1_ref, w_pq_ref,
                  k1_ref, k2_ref,
                  h1_ref, h1t_ref, s1_ref, s2_ref,
                  kbuf, vbuf, ubuf, kvar, vvar, ybuf, *, tm, tiles_per_seq):
    step = pl.program_id(0)
    nblk = tm // BLOCK

    @pl.when(step % tiles_per_seq == 0)
    def _():
        kbuf[0:BLOCK, :] = jnp.zeros((BLOCK, KV_WIDTH), F32)
        vbuf[0:BLOCK, :] = jnp.zeros((BLOCK, KV_WIDTH), F32)
        ubuf[0:8, :] = jnp.zeros((8, CONV_WIDTH), F32)

    x = x_ref[...]
    xb = x.astype(BF16)

    def proj(lo, hi):
        return _dot(xb, w_in_ref[:, lo:hi]) + b_in_ref[:, lo:hi]

    rc = rc_ref[...]
    rsa = rsa_ref[...]
    rsb = rsb_ref[...]

    def rope(t):
        return (t * rc + pltpu.roll(t, LANES - ROT_DIM // 2, 1) * rsa
                + pltpu.roll(t, ROT_DIM // 2, 1) * rsb)

    kbuf[BLOCK:BLOCK + tm, :] = rope(proj(_O_K, _O_V))
    vbuf[BLOCK:BLOCK + tm, :] = proj(_O_V, _O_GB)

    lane = lax.broadcasted_iota(jnp.int32, (tm + BLOCK, LANES), 1)
    lo = lane < HEAD_DIM
    for buf, var in ((kbuf, kvar), (vbuf, vvar)):
        t = buf[...]
        tr = pltpu.roll(t, HEAD_DIM, 1)
        var[0] = jnp.where(lo, t, 0.0).astype(BF16)
        var[1] = jnp.where(lo, 0.0, tr).astype(BF16)
        var[2] = jnp.where(lo, tr, 0.0).astype(BF16)
        var[3] = jnp.where(lo, 0.0, t).astype(BF16)

    qi = lax.broadcasted_iota(jnp.int32, (BLOCK, 2 * BLOCK), 0)
    kj = lax.broadcasted_iota(jnp.int32, (BLOCK, 2 * BLOCK), 1)
    diff = qi + BLOCK - kj
    band = (diff >= 0) & (diff < BLOCK)

    for p in range(N_Q_HEADS // 2):
        q_slab = (rope(proj(p * LANES, (p + 1) * LANES)) * (HEAD_DIM ** -0.5)).astype(BF16)
        for blk in range(nblk):
            first = (step * nblk + blk) % (tiles_per_seq * nblk) == 0
            kmin = jnp.where(first, BLOCK, 0)
            mask = band & (kj >= kmin)
            qs = q_slab[blk * BLOCK:(blk + 1) * BLOCK, :]
            o = jnp.zeros((BLOCK, LANES), F32)
            for hh in range(2):
                h = 2 * p + hh
                var = (h // (N_Q_HEADS // N_KV_HEADS)) * 2 + hh
                kc = kvar[var, blk * BLOCK:(blk + 2) * BLOCK, :]
                vc = vvar[var, blk * BLOCK:(blk + 2) * BLOCK, :]
                s = jnp.where(mask, _dot_nt(qs, kc), NEG_INF)
                sink = sinks_ref[h]
                m = jnp.maximum(jnp.max(s, axis=-1, keepdims=True), sink)
                pe = jnp.exp(s - m)
                den = jnp.sum(pe, axis=-1, keepdims=True) + jnp.exp(sink - m)
                o = o + _dot((pe / den).astype(BF16), vc)
            ybuf[blk * BLOCK:(blk + 1) * BLOCK, p * LANES:(p + 1) * LANES] = o.astype(BF16)

    u = proj(_O_GC, _O_XC) * proj(_O_XC, _O_END)
    ubuf[8:8 + tm, :] = u
    um1 = ubuf[7:7 + tm, :]
    um2 = ubuf[6:6 + tm, :]
    cw = conv_w_ref[...]
    yc = proj(_O_GB, _O_GC) * (cw[0:1, :] * um2 + cw[1:2, :] * um1 + cw[2:3, :] * u)
    ybuf[:, ATTN_WIDTH:] = yc.astype(BF16)

    ubuf[0:8, :] = ubuf[tm:tm + 8, :]
    kbuf[0:BLOCK, :] = kbuf[tm:tm + BLOCK, :]
    vbuf[0:BLOCK, :] = vbuf[tm:tm + BLOCK, :]

    mix = _dot(ybuf[...], w_out_ref[...]) + b_out_ref[...]
    h1 = _layer_norm(DN_ALPHA * x + mix, g1_ref[...], be1_ref[...])
    h1_ref[...] = h1
    h1t_ref[...] = h1.T.astype(BF16)

    qp = _dot(h1.astype(BF16), w_pq_ref[...]).astype(BF16)
    k1 = k1_ref[...]
    k2 = k2_ref[...]
    for h in range(PEER_HEADS):
        base = h * PEER_QDIM
        s1_ref[h] = _dot_nt(k1, qp[:, base:base + PEER_HALF])
        s2_ref[h] = _dot_nt(k2, qp[:, base + PEER_HALF:base + PEER_QDIM])


def _front(xf, rc, rsa, rsb, sinks, w_in, b_in, conv_w, w_out, b_out, g1, be1, w_pq, k1, k2,
           *, seq, tm):
    T = xf.shape[0]
    proj_w = w_in.shape[1]
    tiles_per_seq = seq // tm
    full = lambda shape: pl.BlockSpec(shape, lambda i: (0,) * len(shape))
    tok = lambda w: pl.BlockSpec((tm, w), lambda i: (i, 0))
    kern = functools.partial(_front_kernel, tm=tm, tiles_per_seq=tiles_per_seq)
    return pl.pallas_call(
        kern,
        grid=(T // tm,),
        in_specs=[
            pl.BlockSpec(memory_space=pltpu.SMEM),
            tok(D_MODEL), tok(LANES), tok(LANES), tok(LANES),
            full((D_MODEL, proj_w)), full((1, proj_w)),
            full((CONV_K, CONV_WIDTH)), full((D_MODEL, D_MODEL)), full((1, D_MODEL)),
            full((1, D_MODEL)), full((1, D_MODEL)),
            full((D_MODEL, PEER_HEADS * PEER_QDIM)),
            full((N_KEYS, PEER_HALF)), full((N_KEYS, PEER_HALF)),
        ],
        out_specs=[
            tok(D_MODEL),
            pl.BlockSpec((D_MODEL, tm), lambda i: (0, i)),
            pl.BlockSpec((PEER_HEADS, N_KEYS, tm), lambda i: (0, 0, i)),
            pl.BlockSpec((PEER_HEADS, N_KEYS, tm), lambda i: (0, 0, i)),
        ],
        out_shape=[
            jax.ShapeDtypeStruct((T, D_MODEL), F32),
            jax.ShapeDtypeStruct((D_MODEL, T), BF16),
            jax.ShapeDtypeStruct((PEER_HEADS, N_KEYS, T), F32),
            jax.ShapeDtypeStruct((PEER_HEADS, N_KEYS, T), F32),
        ],
        scratch_shapes=[
            pltpu.VMEM((tm + BLOCK, KV_WIDTH), F32),
            pltpu.VMEM((tm + BLOCK, KV_WIDTH), F32),
            pltpu.VMEM((tm + 8, CONV_WIDTH), F32),
            pltpu.VMEM((4, tm + BLOCK, KV_WIDTH), BF16),
            pltpu.VMEM((4, tm + BLOCK, KV_WIDTH), BF16),
            pltpu.VMEM((tm, D_MODEL), BF16),
        ],
        compiler_params=pltpu.CompilerParams(
            dimension_semantics=("arbitrary",), vmem_limit_bytes=VMEM_LIMIT),
        name="front",
    )(sinks, xf, rc, rsa, rsb, w_in, b_in, conv_w, w_out, b_out, g1, be1, w_pq, k1, k2)


def _top16(s):
    rows = lax.broadcasted_iota(jnp.int32, s.shape, 0)
    rem = s
    rank = jnp.full(s.shape, float(PEER_TOPK), F32)
    vals = []
    for k in range(PEER_TOPK):
        m = jnp.max(rem, axis=0, keepdims=True)
        idx = jnp.min(jnp.where(rem == m, rows, N_KEYS), axis=0, keepdims=True)
        sel = rows == idx
        rem = jnp.where(sel, NEG_INF, rem)
        rank = jnp.where(sel, float(k), rank)
        vals.append(m)
    return vals, rank


_N_CAND_ROWS = PEER_TOPK + 7 * 8 + 8


def _cand_positions():
    import numpy as np
    pos = np.full((_N_CAND_ROWS,), -1, np.int32)
    for r in range(_N_CAND_ROWS):
        if r < 16:
            a, b = 0, r
        elif r < 72:
            a, b = 1 + (r - 16) // 8, (r - 16) % 8
        else:
            a, b = 8 + (r - 72), 0
        if (a + 1) * (b + 1) <= PEER_TOPK:
            pos[r] = a * PEER_TOPK + b
    return pos


def _select_exact(s1, s2, pos, valid, row16):
    v1, rank1 = _top16(s1)
    v2, rank2 = _top16(s2)

    v1m = jnp.zeros((PEER_TOPK, LANES), F32)
    v2m = jnp.zeros((PEER_TOPK, LANES), F32)
    for a in range(PEER_TOPK):
        v1m = jnp.where(row16 == a, v1[a], v1m)
        v2m = jnp.where(row16 == a, v2[a], v2m)
    blocks = [v1[0] + v2m]
    for a in range(1, 8):
        blocks.append(v1[a] + v2m[0:8, :])
    blocks.append(v1m[8:16, :] + v2[0])
    cand = jnp.where(valid, jnp.concatenate(blocks, axis=0), NEG_INF)

    rem = cand
    chosen = jnp.zeros(cand.shape, F32)
    for _ in range(PEER_TOPK):
        m = jnp.max(rem, axis=0, keepdims=True)
        idx = jnp.min(jnp.where(rem == m, pos, 1 << 20), axis=0, keepdims=True)
        sel = (pos == idx) & valid
        rem = jnp.where(sel, NEG_INF, rem)
        chosen = jnp.where(sel, 1.0, chosen)

    top = v1[0] + v2[0]
    z = jnp.sum(chosen * jnp.exp(jnp.where(valid, cand, top) - top), axis=0, keepdims=True)

    counts = [jnp.sum(chosen[0:16, :], axis=0, keepdims=True)]
    for a in range(1, 8):
        counts.append(jnp.sum(chosen[8 + 8 * a:16 + 8 * a, :], axis=0, keepdims=True))
    for a in range(8, 16):
        counts.append(chosen[64 + a:65 + a, :])
    n1 = jnp.zeros(s1.shape, F32)
    for a in range(PEER_TOPK):
        n1 = jnp.where(rank1 == float(a), counts[a], n1)
    return n1, jnp.exp(s1 - v1[0]), rank2, jnp.exp(s2 - v2[0]) / z


def _sort_pairs(n):
    pairs = []
    p = 1
    while p < n:
        k = p
        while k >= 1:
            for j in range(k % p, n - k, 2 * k):
                for i in range(min(k, n - j - k)):
                    if (i + j) // (2 * p) == (i + j + k) // (2 * p):
                        pairs.append((i + j, i + j + k))
            k //= 2
        p *= 2
    return pairs


def _hi_lo(a, b):
    if a is None:
        return b, None
    if b is None:
        return a, None
    return jnp.maximum(a, b), jnp.minimum(a, b)


def _sublane_all(x, op):
    for shift in (4, 2, 1):
        x = op(x, pltpu.roll(x, shift, 0))
    return x


def _top16_sorted(x):
    x = list(x)
    for i, j in _sort_pairs(len(x)):
        x[i], x[j] = _hi_lo(x[i], x[j])
    for shift in (4, 2, 1):
        y = [None if t is None else pltpu.roll(t, shift, 0) for t in x]
        c = [_hi_lo(x[r], y[PEER_TOPK - 1 - r])[0] for r in range(PEER_TOPK)]
        d = PEER_TOPK // 2
        while d >= 1:
            for i in range(PEER_TOPK):
                if not i & d:
                    c[i], c[i + d] = _hi_lo(c[i], c[i + d])
            d //= 2
        x = c
    return x


def _select_fast(s1, s2):
    nv = N_KEYS // 8
    t1 = [s1[8 * v:8 * v + 8, :] for v in range(nv)]
    t2 = [s2[8 * v:8 * v + 8, :] for v in range(nv)]
    v1 = _top16_sorted(t1)
    v2 = _top16_sorted(t2)
    sub = lax.broadcasted_iota(jnp.int32, (8, LANES), 0)
    one = jnp.ones((8, LANES), F32)
    zero = jnp.zeros((8, LANES), F32)

    def spread(vals):
        out = vals[0]
        for r in range(1, 8):
            out = jnp.where(sub == r, vals[r], out)
        return out

    v2lo, v2hi, v1hi = spread(v2[0:8]), spread(v2[8:16]), spread(v1[8:16])
    cand = [v1[0] + v2lo, v1[0] + v2hi]
    for a in range(1, 8):
        cand.append(jnp.where(sub < PEER_TOPK // (a + 1), v1[a] + v2lo, NEG_INF))
    cand.append(v1hi + v2[0])
    best = _top16_sorted(cand + [None] * (PEER_TOPK - len(cand)))
    tau = best[PEER_TOPK - 1]
    z = one
    for k in range(1, PEER_TOPK):
        z = z + jnp.exp(best[k] - best[0])

    count = lambda t: _sublane_all(jnp.where(t >= tau, one, zero), jnp.add)
    counts = [count(cand[0]) + count(cand[1])]
    counts += [count(cand[a + 1]) for a in range(1, 8)]
    counts += [jnp.where(v1[a] + v2[0] >= tau, one, zero) for a in range(8, PEER_TOPK)]
    total = counts[0]
    for a in range(1, PEER_TOPK):
        total = total + counts[a]
    tie = jnp.where(total != float(PEER_TOPK), one, zero)

    n1 = [zero] * nv
    rank2 = [jnp.full((8, LANES), float(PEER_TOPK), F32)] * nv
    for a in reversed(range(PEER_TOPK)):
        n1 = [jnp.where(t >= v1[a], counts[a], n) for t, n in zip(t1, n1)]
        rank2 = [jnp.where(t >= v2[a], float(a), r) for t, r in zip(t2, rank2)]
    for v, tiles in ((v1, t1), (v2, t2)):
        inside = zero
        for t in tiles:
            inside = inside + jnp.where(t >= v[PEER_TOPK - 1], one, zero)
        tie = jnp.where(_sublane_all(inside, jnp.add) != float(PEER_TOPK), one, tie)
        for a in range(PEER_TOPK - 1):
            tie = jnp.where(v[a] == v[a + 1], one, tie)

    inv_z = 1.0 / z
    cat = lambda tiles: jnp.concatenate(tiles, axis=0)
    e1 = cat([jnp.exp(t - v1[0]) for t in t1])
    e2 = cat([jnp.exp(t - v2[0]) * inv_z for t in t2])
    return cat(n1), e1, cat(rank2), e2, tie


def _select_kernel(pos_ref, s1_ref, s2_ref, r2_ref, e2_ref, n1_ref, e1_ref, *, ts):
    def group(g, carry):
        sl = pl.ds(pl.multiple_of(g * LANES, LANES), LANES)
        s1 = s1_ref[0, :, sl]
        s2 = s2_ref[0, :, sl]
        n1, e1, rank2, e2, tie = _select_fast(s1, s2)

        def exact():
            pos = pos_ref[...]
            row16 = lax.broadcasted_iota(jnp.int32, (PEER_TOPK, LANES), 0)
            return _select_exact(s1, s2, pos, pos >= 0, row16)

        n1, e1, rank2, e2 = lax.cond(jnp.max(tie) > 0.0, exact, lambda: (n1, e1, rank2, e2))
        n1_ref[:, 0, :, sl] = n1.reshape(N_KEYS // 8, 8, LANES)
        e1_ref[:, 0, :, sl] = e1.reshape(N_KEYS // 8, 8, LANES)
        r2_ref[0, :, sl] = rank2.astype(BF16)
        e2_ref[0, :, sl] = e2.astype(BF16)
        return carry

    lax.fori_loop(0, ts // LANES, group, 0)


def _select(s1t, s2t, *, ts):
    H, K, T = s1t.shape
    import numpy as np
    pos = jnp.asarray(np.tile(_cand_positions()[:, None], (1, LANES)))
    blk = pl.BlockSpec((1, K, ts), lambda t, h: (h, 0, t))
    blk2 = blk
    blk1 = pl.BlockSpec((K // 8, 1, 8, ts), lambda t, h: (0, h, 0, t))
    return pl.pallas_call(
        functools.partial(_select_kernel, ts=ts),
        grid=(T // ts, H),
        in_specs=[pl.BlockSpec((_N_CAND_ROWS, LANES), lambda t, h: (0, 0)), blk, blk],
        out_specs=[blk2, blk2, blk1, blk1],
        out_shape=[
            jax.ShapeDtypeStruct((H, K, T), BF16),
            jax.ShapeDtypeStruct((H, K, T), BF16),
            jax.ShapeDtypeStruct((K // 8, H, 8, T), F32),
            jax.ShapeDtypeStruct((K // 8, H, 8, T), F32),
        ],
        compiler_params=pltpu.CompilerParams(
            dimension_semantics=("arbitrary", "arbitrary"), vmem_limit_bytes=VMEM_LIMIT),
        name="select",
    )(pos, s1t, s2t)


def _peer_kernel(u_ref, vt_ref, xt_ref, r2_in, e2_in, n1_in, e1_in, h1_ref, g2_ref, be2_ref,
                 out_ref, acc_ref, a0_ref, a1_ref, w0_ref, w1_ref,
                 r2_ref, e2_ref, n1_ref, e1_ref, *, ec, tt, nchunks):
    s = pl.program_id(1)
    nkb = ec // N_KEYS
    assert nkb == 8
    rows = BF16_TILE_ROWS

    @pl.when(s == 0)
    def _():
        acc_ref[...] = jnp.zeros(acc_ref.shape, F32)
        for ref in (a0_ref, a1_ref, w0_ref, w1_ref):
            ref[...] = jnp.zeros(ref.shape, ref.dtype)
        r2_ref[:, :, 0:tt] = r2_in[...]
        e2_ref[:, :, LANES:LANES + tt] = e2_in[...]
        n1_ref[...] = n1_in[...]
        e1_ref[...] = e1_in[...]

    cg = jnp.clip(s - 1, 0, nchunks - 1)

    def stages(a_new, a_prev, w_new, w_old):
        def row_bf16(ref, h, ib, sl):
            return jnp.broadcast_to(ref[cg, h, ib:ib + 1, sl], (rows, LANES)).astype(BF16)

        def gate_unit(kp, lt):
            lsl = slice(lt * LANES, (lt + 1) * LANES)
            nrg = N_KEYS // rows
            keys = (2 * kp, 2 * kp + 1)
            g = [[None] * nrg for _ in keys]
            for h in range(PEER_HEADS):
                n1b = [row_bf16(n1_ref, h, ib, lsl) for ib in keys]
                e1b = [row_bf16(e1_ref, h, ib, lsl) for ib in keys]
                for rg in range(nrg):
                    wsl = slice(rg * rows, (rg + 1) * rows)
                    r2 = r2_ref[h, wsl, lsl]
                    e2 = e2_ref[h, wsl, (lt + 1) * LANES:(lt + 2) * LANES]
                    for k in range(2):
                        term = jnp.where(r2 < n1b[k], e2, jnp.zeros_like(e2)) * e1b[k]
                        g[k][rg] = term if h == 0 else g[k][rg] + term
            for k, ib in enumerate(keys):
                for rg in range(nrg):
                    esl = slice(ib * N_KEYS + rg * rows, ib * N_KEYS + (rg + 1) * rows)
                    ab = a_prev[esl, lsl]
                    ge = 0.5 * ab * (1.0 + lax.erf(ab * (2.0 ** -0.5)))
                    w_new[esl, lsl] = ge.astype(BF16) * g[k][rg]

        def out_piece(p):
            nsl = slice(p * PEER_DOT_COLS, (p + 1) * PEER_DOT_COLS)
            acc_ref[:, nsl] += _dot(vt_ref[...], w_old[:, nsl])

        def score_piece(p):
            nsl = slice(p * PEER_DOT_COLS, (p + 1) * PEER_DOT_COLS)
            a_new[:, nsl] = _dot(u_ref[...], xt_ref[:, nsl])

        units = [(kp, lt) for kp in range(nkb // 2) for lt in range(tt // LANES)]
        npieces = tt // PEER_DOT_COLS
        pieces = [f for p in range(npieces) for f in ((out_piece, p), (score_piece, p))]
        per = len(units) // len(pieces)
        assert per * len(pieces) == len(units)
        for n, (fn, p) in enumerate(pieces):
            for kp, lt in units[n * per:(n + 1) * per]:
                gate_unit(kp, lt)
            fn(p)


    @pl.when(s % 2 == 0)
    def _():
        stages(a0_ref, a1_ref, w1_ref, w0_ref)

    @pl.when(s % 2 == 1)
    def _():
        stages(a1_ref, a0_ref, w0_ref, w1_ref)

    @pl.when(s == nchunks + 1)
    def _():
        z = DN_ALPHA * h1_ref[...] + acc_ref[...].T
        out_ref[...] = _layer_norm(z, g2_ref[...], be2_ref[...])


def _peer(u_bf, vt_bf, h1t, r2, e2, n1, e1, h1, g2, be2, *, ec, tt):
    E = u_bf.shape[0]
    T = h1.shape[0]
    nchunks = E // ec
    last = nchunks - 1
    tokrows = pl.BlockSpec((PEER_HEADS, N_KEYS, tt), lambda t, s: (0, 0, t))
    tokkeys = pl.BlockSpec((N_KEYS // 8, PEER_HEADS, 8, tt), lambda t, s: (0, 0, 0, t))
    vec = pl.BlockSpec((1, D_MODEL), lambda t, s: (0, 0))
    return pl.pallas_call(
        functools.partial(_peer_kernel, ec=ec, tt=tt, nchunks=nchunks),
        grid=(T // tt, nchunks + 2),
        in_specs=[
            pl.BlockSpec((ec, D_MODEL), lambda t, s: (jnp.minimum(s, last), 0)),
            pl.BlockSpec((D_MODEL, ec), lambda t, s: (0, jnp.clip(s - 2, 0, last))),
            pl.BlockSpec((D_MODEL, tt), lambda t, s: (0, t)),
            tokrows, tokrows, tokkeys, tokkeys,
            pl.BlockSpec((tt, D_MODEL), lambda t, s: (t, 0)),
            vec, vec,
        ],
        out_specs=pl.BlockSpec((tt, D_MODEL), lambda t, s: (t, 0)),
        out_shape=jax.ShapeDtypeStruct((T, D_MODEL), F32),
        scratch_shapes=[
            pltpu.VMEM((D_MODEL, tt), F32),
            pltpu.VMEM((ec, tt), F32),
            pltpu.VMEM((ec, tt), F32),
            pltpu.VMEM((ec, tt), BF16),
            pltpu.VMEM((ec, tt), BF16),
            pltpu.VMEM((PEER_HEADS, N_KEYS, tt + GATE_PAD), BF16),
            pltpu.VMEM((PEER_HEADS, N_KEYS, tt + GATE_PAD), BF16),
            pltpu.VMEM((N_KEYS // 8, PEER_HEADS, 8, tt), F32),
            pltpu.VMEM((N_KEYS // 8, PEER_HEADS, 8, tt), F32),
        ],
        compiler_params=pltpu.CompilerParams(
            dimension_semantics=("arbitrary", "arbitrary"), vmem_limit_bytes=VMEM_LIMIT),
        name="peer",
    )(u_bf, vt_bf, h1t, r2, e2, n1, e1, h1, g2, be2)


def _peer_dense_kernel(u_ref, vt_ref, xt_ref, r2_in, e2_in, n1_in, e1_in, h1_ref, g2_ref, be2_ref,
                       out_ref, acc_ref, a_ref, w_ref, r2_ref, e2_ref, *, ec, tt):
    c = pl.program_id(1)
    nkb = ec // N_KEYS
    assert nkb == 8
    rows = F32_TILE_ROWS
    nrg = GATE_UNIT_KEYS // rows

    @pl.when(c == 0)
    def _():
        acc_ref[...] = jnp.zeros(acc_ref.shape, F32)
        r2_ref[:, :, 0:tt] = r2_in[...].astype(F32)
        e2_ref[:, :, LANES:LANES + tt] = e2_in[...].astype(F32)

    ta = a_ref.shape[1]

    def row_tile(ref, h, ib, sl):
        return jnp.broadcast_to(ref[c, h, ib:ib + 1, sl], (rows, LANES))

    def gate_unit(kp, lt, part):
        lsl = slice(lt * LANES, (lt + 1) * LANES)
        keys = (2 * kp, 2 * kp + 1)
        g = [[None] * nrg for _ in keys]
        for h in range(PEER_HEADS):
            n1b = [row_tile(n1_in, h, ib, lsl) for ib in keys]
            e1b = [row_tile(e1_in, h, ib, lsl) for ib in keys]
            for rg in range(nrg):
                j0 = part * GATE_UNIT_KEYS + rg * rows
                r2 = r2_ref[h, j0:j0 + rows, lsl]
                e2 = e2_ref[h, j0:j0 + rows, (lt + 1) * LANES:(lt + 2) * LANES]
                for k in range(2):
                    term = jnp.where(r2 < n1b[k], e2, 0.0) * e1b[k]
                    g[k][rg] = term if h == 0 else g[k][rg] + term
        for k, ib in enumerate(keys):
            for rp in range(nrg // 2):
                e0 = ib * N_KEYS + part * GATE_UNIT_KEYS + rp * BF16_TILE_ROWS
                esl = slice(e0, e0 + BF16_TILE_ROWS)
                ab = a_ref[esl, (lt * LANES) % ta:(lt * LANES) % ta + LANES]
                ge = 0.5 * ab * (1.0 + lax.erf(ab * (2.0 ** -0.5)))
                gg = jnp.concatenate([g[k][2 * rp], g[k][2 * rp + 1]], axis=0)
                w_ref[esl, lsl] = (ge * gg).astype(BF16)

    for nt in range(tt // ta):
        a_ref[...] = _dot(u_ref[...], xt_ref[:, nt * ta:(nt + 1) * ta])
        for lt in range(nt * ta // LANES, (nt + 1) * ta // LANES):
            for kp in range(nkb // 2):
                for part in range(N_KEYS // GATE_UNIT_KEYS):
                    gate_unit(kp, lt, part)

    acc_ref[...] += _dot(vt_ref[...], w_ref[...])

    @pl.when(c == pl.num_programs(1) - 1)
    def _():
        z = DN_ALPHA * h1_ref[...] + acc_ref[...].T
        out_ref[...] = _layer_norm(z, g2_ref[...], be2_ref[...])


def _peer_dense(u_bf, vt_bf, h1t, r2, e2, n1, e1, h1, g2, be2, *, ec, tt):
    E = u_bf.shape[0]
    T = h1.shape[0]
    tokrows = pl.BlockSpec((PEER_HEADS, N_KEYS, tt), lambda t, c: (0, 0, t))
    tokkeys = pl.BlockSpec((N_KEYS // 8, PEER_HEADS, 8, tt), lambda t, c: (0, 0, 0, t))
    vec = pl.BlockSpec((1, D_MODEL), lambda t, c: (0, 0))
    return pl.pallas_call(
        functools.partial(_peer_dense_kernel, ec=ec, tt=tt),
        grid=(T // tt, E // ec),
        in_specs=[
            pl.BlockSpec((ec, D_MODEL), lambda t, c: (c, 0)),
            pl.BlockSpec((D_MODEL, ec), lambda t, c: (0, c)),
            pl.BlockSpec((D_MODEL, tt), lambda t, c: (0, t)),
            tokrows, tokrows, tokkeys, tokkeys,
            pl.BlockSpec((tt, D_MODEL), lambda t, c: (t, 0)),
            vec, vec,
        ],
        out_specs=pl.BlockSpec((tt, D_MODEL), lambda t, c: (t, 0)),
        out_shape=jax.ShapeDtypeStruct((T, D_MODEL), F32),
        scratch_shapes=[
            pltpu.VMEM((D_MODEL, tt), F32),
            pltpu.VMEM((ec, min(tt, PEER_SCORE_COLS)), F32),
            pltpu.VMEM((ec, tt), BF16),
            pltpu.VMEM((PEER_HEADS, N_KEYS, tt + GATE_PAD), F32),
            pltpu.VMEM((PEER_HEADS, N_KEYS, tt + GATE_PAD), F32),
        ],
        compiler_params=pltpu.CompilerParams(
            dimension_semantics=("arbitrary", "arbitrary"), vmem_limit_bytes=VMEM_LIMIT),
        name="peer",
    )(u_bf, vt_bf, h1t, r2, e2, n1, e1, h1, g2, be2)


def _rope_tables(positions):
    T = positions.size
    half = ROT_DIM // 2
    inv_freq = ROPE_THETA ** (-jnp.arange(0, ROT_DIM, 2, dtype=F32) / ROT_DIM)
    ang = positions.reshape(T, 1).astype(F32) * inv_freq
    cos, sin = jnp.cos(ang), jnp.sin(ang)
    zeros = lambda w: jnp.zeros((T, w), F32)
    rc = jnp.concatenate([cos, cos, jnp.ones((T, HEAD_DIM - ROT_DIM), F32)], axis=1)
    rsa = jnp.concatenate([-sin, zeros(HEAD_DIM - half)], axis=1)
    rsb = jnp.concatenate([zeros(half), sin, zeros(HEAD_DIM - ROT_DIM)], axis=1)
    rep = LANES // HEAD_DIM
    return jnp.tile(rc, (1, rep)), jnp.tile(rsa, (1, rep)), jnp.tile(rsb, (1, rep))


def _tiles(seq, total_tokens):
    tm = 256 if seq % 256 == 0 else BLOCK
    ts = 512 if total_tokens % 512 == 0 else LANES
    tt = 512 if total_tokens % 512 == 0 else 256
    ec = 8 * N_KEYS
    return tm, ts, tt, ec


def kernel(x, positions, w_in, b_in, attn_sinks, conv_w, w_out, b_out, ln1_g, ln1_b,
           w_pq, sub_keys1, sub_keys2, u_experts, v_experts, ln2_g, ln2_b):
    B, S, D = x.shape
    T = B * S
    assert D == D_MODEL and S % BLOCK == 0 and T % 256 == 0
    assert w_in.shape[0] == DEPTH
    tm, ts, tt, ec = _tiles(S, T)
    row = lambda v: v.reshape(1, -1).astype(F32)

    rc, rsa, rsb = _rope_tables(positions)
    h1, h1t, s1t, s2t = _front(
        x.reshape(T, D), rc, rsa, rsb, attn_sinks[0].astype(F32),
        w_in[0].astype(BF16), row(b_in[0]), conv_w[0].astype(F32),
        w_out[0].astype(BF16), row(b_out[0]), row(ln1_g[0]), row(ln1_b[0]),
        w_pq[0].astype(BF16), sub_keys1[0].astype(BF16), sub_keys2[0].astype(BF16),
        seq=S, tm=tm)
    r2, e2, n1, e1 = _select(s1t, s2t, ts=ts)
    out = _peer_dense(u_experts[0].astype(BF16), v_experts[0].T.astype(BF16), h1t,
                r2, e2, n1, e1, h1, row(ln2_g[0]), row(ln2_b[0]), ec=ec, tt=tt)
    return out.reshape(B, S, D)
```

```python
import functools

import jax
import jax.numpy as jnp
from jax import lax
from jax.experimental import pallas as pl
from jax.experimental.pallas import tpu as pltpu

D_MODEL = 1024
HEAD_DIM = 64
N_Q_HEADS = 8
N_KV_HEADS = 2
ATTN_WIDTH = N_Q_HEADS * HEAD_DIM
KV_WIDTH = N_KV_HEADS * HEAD_DIM
BLOCK = 128
ROT_DIM = HEAD_DIM // 4
ROPE_THETA = 500000.0
CONV_WIDTH = D_MODEL - ATTN_WIDTH
CONV_K = 3
N_KEYS = 128
PEER_HEADS = 8
PEER_QDIM = 256
PEER_HALF = PEER_QDIM // 2
PEER_TOPK = 16
DEPTH = 1
DN_ALPHA = (2.0 * DEPTH) ** 0.25
LN_EPS = 1e-5

LANES = 128
BF16_TILE_ROWS = 16
PEER_DOT_ROWS = 128
PEER_SCORE_COLS = 512
F32_TILE_ROWS = 8
GATE_UNIT_KEYS = 64
GATE_PAD = 2 * LANES
VMEM_LIMIT = 56 * 1024 * 1024

_O_K = ATTN_WIDTH
_O_V = _O_K + KV_WIDTH
_O_GB = _O_V + KV_WIDTH
_O_GC = _O_GB + CONV_WIDTH
_O_XC = _O_GC + CONV_WIDTH
_O_END = _O_XC + CONV_WIDTH

BF16 = jnp.bfloat16
F32 = jnp.float32
NEG_INF = float("-inf")


def _dot(a, b):
    return jnp.dot(a, b, preferred_element_type=F32)


def _dot_nt(a, b):
    return lax.dot_general(a, b, (((1,), (1,)), ((), ())), preferred_element_type=F32)


def _layer_norm(z, g, b):
    mu = jnp.mean(z, axis=-1, keepdims=True)
    zc = z - mu
    var = jnp.mean(zc * zc, axis=-1, keepdims=True)
    return zc * lax.rsqrt(var + LN_EPS) * g + b


def _front_kernel(sinks_ref, x_ref, rc_ref, rsa_ref, rsb_ref, w_in_ref, b_in_ref,
                  conv_w_ref, w_out_ref, b_out_ref, g1_ref, be1_ref, w_pq_ref,
                  k1_ref, k2_ref,
                  h1_ref, h1t_ref, s1_ref, s2_ref,
                  kbuf, vbuf, ubuf, kvar, vvar, ybuf, *, tm, tiles_per_seq):
    step = pl.program_id(0)
    nblk = tm // BLOCK

    @pl.when(step % tiles_per_seq == 0)
    def _():
        kbuf[0:BLOCK, :] = jnp.zeros((BLOCK, KV_WIDTH), F32)
        vbuf[0:BLOCK, :] = jnp.zeros((BLOCK, KV_WIDTH), F32)
        ubuf[0:8, :] = jnp.zeros((8, CONV_WIDTH), F32)

    x = x_ref[...]
    xb = x.astype(BF16)

    def proj(lo, hi):
        return _dot(xb, w_in_ref[:, lo:hi]) + b_in_ref[:, lo:hi]

    rc = rc_ref[...]
    rsa = rsa_ref[...]
    rsb = rsb_ref[...]

    def rope(t):
        return (t * rc + pltpu.roll(t, LANES - ROT_DIM // 2, 1) * rsa
                + pltpu.roll(t, ROT_DIM // 2, 1) * rsb)

    kbuf[BLOCK:BLOCK + tm, :] = rope(proj(_O_K, _O_V))
    vbuf[BLOCK:BLOCK + tm, :] = proj(_O_V, _O_GB)

    lane = lax.broadcasted_iota(jnp.int32, (tm + BLOCK, LANES), 1)
    lo = lane < HEAD_DIM
    for buf, var in ((kbuf, kvar), (vbuf, vvar)):
        t = buf[...]
        tr = pltpu.roll(t, HEAD_DIM, 1)
        var[0] = jnp.where(lo, t, 0.0).astype(BF16)
        var[1] = jnp.where(lo, 0.0, tr).astype(BF16)
        var[2] = jnp.where(lo, tr, 0.0).astype(BF16)
        var[3] = jnp.where(lo, 0.0, t).astype(BF16)

    qi = lax.broadcasted_iota(jnp.int32, (BLOCK, 2 * BLOCK), 0)
    kj = lax.broadcasted_iota(jnp.int32, (BLOCK, 2 * BLOCK), 1)
    diff = qi + BLOCK - kj
    band = (diff >= 0) & (diff < BLOCK)

    for p in range(N_Q_HEADS // 2):
        q_slab = (rope(proj(p * LANES, (p + 1) * LANES)) * (HEAD_DIM ** -0.5)).astype(BF16)
        for blk in range(nblk):
            first = (step * nblk + blk) % (tiles_per_seq * nblk) == 0
            kmin = jnp.where(first, BLOCK, 0)
            mask = band & (kj >= kmin)
            qs = q_slab[blk * BLOCK:(blk + 1) * BLOCK, :]
            o = jnp.zeros((BLOCK, LANES), F32)
            for hh in range(2):
                h = 2 * p + hh
                var = (h // (N_Q_HEADS // N_KV_HEADS)) * 2 + hh
                kc = kvar[var, blk * BLOCK:(blk + 2) * BLOCK, :]
                vc = vvar[var, blk * BLOCK:(blk + 2) * BLOCK, :]
                s = jnp.where(mask, _dot_nt(qs, kc), NEG_INF)
                sink = sinks_ref[h]
                m = jnp.maximum(jnp.max(s, axis=-1, keepdims=True), sink)
                pe = jnp.exp(s - m)
                den = jnp.sum(pe, axis=-1, keepdims=True) + jnp.exp(sink - m)
                o = o + _dot((pe / den).astype(BF16), vc)
            ybuf[blk * BLOCK:(blk + 1) * BLOCK, p * LANES:(p + 1) * LANES] = o.astype(BF16)

    u = proj(_O_GC, _O_XC) * proj(_O_XC, _O_END)
    ubuf[8:8 + tm, :] = u
    um1 = ubuf[7:7 + tm, :]
    um2 = ubuf[6:6 + tm, :]
    cw = conv_w_ref[...]
    yc = proj(_O_GB, _O_GC) * (cw[0:1, :] * um2 + cw[1:2, :] * um1 + cw[2:3, :] * u)
    ybuf[:, ATTN_WIDTH:] = yc.astype(BF16)

    ubuf[0:8, :] = ubuf[tm:tm + 8, :]
    kbuf[0:BLOCK, :] = kbuf[tm:tm + BLOCK, :]
    vbuf[0:BLOCK, :] = vbuf[tm:tm + BLOCK, :]

    mix = _dot(ybuf[...], w_out_ref[...]) + b_out_ref[...]
    h1 = _layer_norm(DN_ALPHA * x + mix, g1_ref[...], be1_ref[...])
    h1_ref[...] = h1
    h1t_ref[...] = h1.T.astype(BF16)

    qp = _dot(h1.astype(BF16), w_pq_ref[...]).astype(BF16)
    k1 = k1_ref[...]
    k2 = k2_ref[...]
    for h in range(PEER_HEADS):
        base = h * PEER_QDIM
        s1_ref[h] = _dot_nt(k1, qp[:, base:base + PEER_HALF])
        s2_ref[h] = _dot_nt(k2, qp[:, base + PEER_HALF:base + PEER_QDIM])


def _front(xf, rc, rsa, rsb, sinks, w_in, b_in, conv_w, w_out, b_out, g1, be1, w_pq, k1, k2,
           *, seq, tm):
    T = xf.shape[0]
    proj_w = w_in.shape[1]
    tiles_per_seq = seq // tm
    full = lambda shape: pl.BlockSpec(shape, lambda i: (0,) * len(shape))
    tok = lambda w: pl.BlockSpec((tm, w), lambda i: (i, 0))
    kern = functools.partial(_front_kernel, tm=tm, tiles_per_seq=tiles_per_seq)
    return pl.pallas_call(
        kern,
        grid=(T // tm,),
        in_specs=[
            pl.BlockSpec(memory_space=pltpu.SMEM),
            tok(D_MODEL), tok(LANES), tok(LANES), tok(LANES),
            full((D_MODEL, proj_w)), full((1, proj_w)),
            full((CONV_K, CONV_WIDTH)), full((D_MODEL, D_MODEL)), full((1, D_MODEL)),
            full((1, D_MODEL)), full((1, D_MODEL)),
            full((D_MODEL, PEER_HEADS * PEER_QDIM)),
            full((N_KEYS, PEER_HALF)), full((N_KEYS, PEER_HALF)),
        ],
        out_specs=[
            tok(D_MODEL),
            pl.BlockSpec((D_MODEL, tm), lambda i: (0, i)),
            pl.BlockSpec((PEER_HEADS, N_KEYS, tm), lambda i: (0, 0, i)),
            pl.BlockSpec((PEER_HEADS, N_KEYS, tm), lambda i: (0, 0, i)),
        ],
        out_shape=[
            jax.ShapeDtypeStruct((T, D_MODEL), F32),
            jax.ShapeDtypeStruct((D_MODEL, T), BF16),
            jax.ShapeDtypeStruct((PEER_HEADS, N_KEYS, T), F32),
            jax.ShapeDtypeStruct((PEER_HEADS, N_KEYS, T), F32),
        ],
        scratch_shapes=[
            pltpu.VMEM((tm + BLOCK, KV_WIDTH), F32),
            pltpu.VMEM((tm + BLOCK, KV_WIDTH), F32),
            pltpu.VMEM((tm + 8, CONV_WIDTH), F32),
            pltpu.VMEM((4, tm + BLOCK, KV_WIDTH), BF16),
            pltpu.VMEM((4, tm + BLOCK, KV_WIDTH), BF16),
            pltpu.VMEM((tm, D_MODEL), BF16),
        ],
        compiler_params=pltpu.CompilerParams(
            dimension_semantics=("arbitrary",), vmem_limit_bytes=VMEM_LIMIT),
        name="front",
    )(sinks, xf, rc, rsa, rsb, w_in, b_in, conv_w, w_out, b_out, g1, be1, w_pq, k1, k2)


def _top16(s):
    rows = lax.broadcasted_iota(jnp.int32, s.shape, 0)
    rem = s
    rank = jnp.full(s.shape, float(PEER_TOPK), F32)
    vals = []
    for k in range(PEER_TOPK):
        m = jnp.max(rem, axis=0, keepdims=True)
        idx = jnp.min(jnp.where(rem == m, rows, N_KEYS), axis=0, keepdims=True)
        sel = rows == idx
        rem = jnp.where(sel, NEG_INF, rem)
        rank = jnp.where(sel, float(k), rank)
        vals.append(m)
    return vals, rank


_N_CAND_ROWS = PEER_TOPK + 7 * 8 + 8


def _cand_positions():
    import numpy as np
    pos = np.full((_N_CAND_ROWS,), -1, np.int32)
    for r in range(_N_CAND_ROWS):
        if r < 16:
            a, b = 0, r
        elif r < 72:
            a, b = 1 + (r - 16) // 8, (r - 16) % 8
        else:
            a, b = 8 + (r - 72), 0
        if (a + 1) * (b + 1) <= PEER_TOPK:
            pos[r] = a * PEER_TOPK + b
    return pos


def _select_exact(s1, s2, pos, valid, row16):
    v1, rank1 = _top16(s1)
    v2, rank2 = _top16(s2)

    v1m = jnp.zeros((PEER_TOPK, LANES), F32)
    v2m = jnp.zeros((PEER_TOPK, LANES), F32)
    for a in range(PEER_TOPK):
        v1m = jnp.where(row16 == a, v1[a], v1m)
        v2m = jnp.where(row16 == a, v2[a], v2m)
    blocks = [v1[0] + v2m]
    for a in range(1, 8):
        blocks.append(v1[a] + v2m[0:8, :])
    blocks.append(v1m[8:16, :] + v2[0])
    cand = jnp.where(valid, jnp.concatenate(blocks, axis=0), NEG_INF)

    rem = cand
    chosen = jnp.zeros(cand.shape, F32)
    for _ in range(PEER_TOPK):
        m = jnp.max(rem, axis=0, keepdims=True)
        idx = jnp.min(jnp.where(rem == m, pos, 1 << 20), axis=0, keepdims=True)
        sel = (pos == idx) & valid
        rem = jnp.where(sel, NEG_INF, rem)
        chosen = jnp.where(sel, 1.0, chosen)

    top = v1[0] + v2[0]
    z = jnp.sum(chosen * jnp.exp(jnp.where(valid, cand, top) - top), axis=0, keepdims=True)

    counts = [jnp.sum(chosen[0:16, :], axis=0, keepdims=True)]
    for a in range(1, 8):
        counts.append(jnp.sum(chosen[8 + 8 * a:16 + 8 * a, :], axis=0, keepdims=True))
    for a in range(8, 16):
        counts.append(chosen[64 + a:65 + a, :])
    n1 = jnp.zeros(s1.shape, F32)
    for a in range(PEER_TOPK):
        n1 = jnp.where(rank1 == float(a), counts[a], n1)
    return n1, jnp.exp(s1 - v1[0]), rank2, jnp.exp(s2 - v2[0]) / z


def _sort_pairs(n):
    pairs = []
    p = 1
    while p < n:
        k = p
        while k >= 1:
            for j in range(k % p, n - k, 2 * k):
                for i in range(min(k, n - j - k)):
                    if (i + j) // (2 * p) == (i + j + k) // (2 * p):
                        pairs.append((i + j, i + j + k))
            k //= 2
        p *= 2
    return pairs


def _hi_lo(a, b):
    if a is None:
        return b, None
    if b is None:
        return a, None
    return jnp.maximum(a, b), jnp.minimum(a, b)


def _sublane_all(x, op):
    for shift in (4, 2, 1):
        x = op(x, pltpu.roll(x, shift, 0))
    return x


def _top16_sorted(x):
    x = list(x)
    for i, j in _sort_pairs(len(x)):
        x[i], x[j] = _hi_lo(x[i], x[j])
    for shift in (4, 2, 1):
        y = [None if t is None else pltpu.roll(t, shift, 0) for t in x]
        c = [_hi_lo(x[r], y[PEER_TOPK - 1 - r])[0] for r in range(PEER_TOPK)]
        d = PEER_TOPK // 2
        while d >= 1:
            for i in range(PEER_TOPK):
                if not i & d:
                    c[i], c[i + d] = _hi_lo(c[i], c[i + d])
            d //= 2
        x = c
    return x


def _select_fast(s1, s2):
    nv = N_KEYS // 8
    t1 = [s1[8 * v:8 * v + 8, :] for v in range(nv)]
    t2 = [s2[8 * v:8 * v + 8, :] for v in range(nv)]
    v1 = _top16_sorted(t1)
    v2 = _top16_sorted(t2)
    sub = lax.broadcasted_iota(jnp.int32, (8, LANES), 0)
    one = jnp.ones((8, LANES), F32)
    zero = jnp.zeros((8, LANES), F32)

    def spread(vals):
        out = vals[0]
        for r in range(1, 8):
            out = jnp.where(sub == r, vals[r], out)
        return out

    v2lo, v2hi, v1hi = spread(v2[0:8]), spread(v2[8:16]), spread(v1[8:16])
    cand = [v1[0] + v2lo, v1[0] + v2hi]
    for a in range(1, 8):
        cand.append(jnp.where(sub < PEER_TOPK // (a + 1), v1[a] + v2lo, NEG_INF))
    cand.append(v1hi + v2[0])
    best = _top16_sorted(cand + [None] * (PEER_TOPK - len(cand)))
    tau = best[PEER_TOPK - 1]
    z = one
    for k in range(1, PEER_TOPK):
        z = z + jnp.exp(best[k] - best[0])

    count = lambda t: _sublane_all(jnp.where(t >= tau, one, zero), jnp.add)
    counts = [count(cand[0]) + count(cand[1])]
    counts += [count(cand[a + 1]) for a in range(1, 8)]
    counts += [jnp.where(v1[a] + v2[0] >= tau, one, zero) for a in range(8, PEER_TOPK)]
    total = counts[0]
    for a in range(1, PEER_TOPK):
        total = total + counts[a]
    tie = jnp.where(total != float(PEER_TOPK), one, zero)

    n1 = [zero] * nv
    rank2 = [jnp.full((8, LANES), float(PEER_TOPK), F32)] * nv
    for a in reversed(range(PEER_TOPK)):
        n1 = [jnp.where(t >= v1[a], counts[a], n) for t, n in zip(t1, n1)]
        rank2 = [jnp.where(t >= v2[a], float(a), r) for t, r in zip(t2, rank2)]
    for v, tiles in ((v1, t1), (v2, t2)):
        inside = zero
        for t in tiles:
            inside = inside + jnp.where(t >= v[PEER_TOPK - 1], one, zero)
        tie = jnp.where(_sublane_all(inside, jnp.add) != float(PEER_TOPK), one, tie)
        for a in range(PEER_TOPK - 1):
            tie = jnp.where(v[a] == v[a + 1], one, tie)

    inv_z = 1.0 / z
    cat = lambda tiles: jnp.concatenate(tiles, axis=0)
    e1 = cat([jnp.exp(t - v1[0]) for t in t1])
    e2 = cat([jnp.exp(t - v2[0]) * inv_z for t in t2])
    return cat(n1), e1, cat(rank2), e2, tie


def _select_kernel(pos_ref, s1_ref, s2_ref, r2_ref, e2_ref, n1_ref, e1_ref, *, ts):
    def group(g, carry):
        sl = pl.ds(pl.multiple_of(g * LANES, LANES), LANES)
        s1 = s1_ref[0, :, sl]
        s2 = s2_ref[0, :, sl]
        n1, e1, rank2, e2, tie = _select_fast(s1, s2)

        def exact():
            pos = pos_ref[...]
            row16 = lax.broadcasted_iota(jnp.int32, (PEER_TOPK, LANES), 0)
            return _select_exact(s1, s2, pos, pos >= 0, row16)

        n1, e1, rank2, e2 = lax.cond(jnp.max(tie) > 0.0, exact, lambda: (n1, e1, rank2, e2))
        n1_ref[:, 0, :, sl] = n1.reshape(N_KEYS // 8, 8, LANES)
        e1_ref[:, 0, :, sl] = e1.reshape(N_KEYS // 8, 8, LANES)
        r2_ref[0, :, sl] = rank2.astype(BF16)
        e2_ref[0, :, sl] = e2.astype(BF16)
        return carry

    lax.fori_loop(0, ts // LANES, group, 0)


def _select(s1t, s2t, *, ts):
    H, K, T = s1t.shape
    import numpy as np
    pos = jnp.asarray(np.tile(_cand_positions()[:, None], (1, LANES)))
    blk = pl.BlockSpec((1, K, ts), lambda t, h: (h, 0, t))
    blk2 = blk
    blk1 = pl.BlockSpec((K // 8, 1, 8, ts), lambda t, h: (0, h, 0, t))
    return pl.pallas_call(
        functools.partial(_select_kernel, ts=ts),
        grid=(T // ts, H),
        in_specs=[pl.BlockSpec((_N_CAND_ROWS, LANES), lambda t, h: (0, 0)), blk, blk],
        out_specs=[blk2, blk2, blk1, blk1],
        out_shape=[
            jax.ShapeDtypeStruct((H, K, T), BF16),
            jax.ShapeDtypeStruct((H, K, T), BF16),
            jax.ShapeDtypeStruct((K // 8, H, 8, T), F32),
            jax.ShapeDtypeStruct((K // 8, H, 8, T), F32),
        ],
        compiler_params=pltpu.CompilerParams(
            dimension_semantics=("arbitrary", "arbitrary"), vmem_limit_bytes=VMEM_LIMIT),
        name="select",
    )(pos, s1t, s2t)


def _peer_kernel(u_ref, vt_ref, xt_ref, r2_in, e2_in, n1_in, e1_in, h1_ref, g2_ref, be2_ref,
                 out_ref, acc_ref, a0_ref, a1_ref, w0_ref, w1_ref,
                 r2_ref, e2_ref, n1_ref, e1_ref, *, ec, tt, nchunks):
    s = pl.program_id(1)
    nkb = ec // N_KEYS
    assert nkb == 8
    rows = BF16_TILE_ROWS

    @pl.when(s == 0)
    def _():
        acc_ref[...] = jnp.zeros(acc_ref.shape, F32)
        for ref in (a0_ref, a1_ref, w0_ref, w1_ref):
            ref[...] = jnp.zeros(ref.shape, ref.dtype)
        r2_ref[:, :, 0:tt] = r2_in[...].astype(F32)
        e2_ref[:, :, LANES:LANES + tt] = e2_in[...].astype(F32)
        n1_ref[...] = n1_in[...]
        e1_ref[...] = e1_in[...]

    cg = jnp.clip(s - 1, 0, nchunks - 1)

    def stages(a_new, a_prev, w_new, w_old):
        rows8 = F32_TILE_ROWS
        nrg = GATE_UNIT_KEYS // rows8

        def row_tile(ref, h, ib, sl):
            return jnp.broadcast_to(ref[cg, h, ib:ib + 1, sl], (rows8, LANES))

        def gate_unit(kp, lt, part):
            lsl = slice(lt * LANES, (lt + 1) * LANES)
            keys = (2 * kp, 2 * kp + 1)
            g = [[None] * nrg for _ in keys]
            for h in range(PEER_HEADS):
                n1b = [row_tile(n1_ref, h, ib, lsl) for ib in keys]
                e1b = [row_tile(e1_ref, h, ib, lsl) for ib in keys]
                for rg in range(nrg):
                    j0 = part * GATE_UNIT_KEYS + rg * rows8
                    r2 = r2_ref[h, j0:j0 + rows8, lsl]
                    e2 = e2_ref[h, j0:j0 + rows8, (lt + 1) * LANES:(lt + 2) * LANES]
                    for k in range(2):
                        term = jnp.where(r2 < n1b[k], e2, 0.0) * e1b[k]
                        g[k][rg] = term if h == 0 else g[k][rg] + term
            for k, ib in enumerate(keys):
                for rp in range(nrg // 2):
                    e0 = ib * N_KEYS + part * GATE_UNIT_KEYS + rp * BF16_TILE_ROWS
                    esl = slice(e0, e0 + BF16_TILE_ROWS)
                    ab = a_prev[esl, lsl]
                    ge = 0.5 * ab * (1.0 + lax.erf(ab * (2.0 ** -0.5)))
                    gg = jnp.concatenate([g[k][2 * rp], g[k][2 * rp + 1]], axis=0)
                    w_new[esl, lsl] = (ge * gg).astype(BF16)

        def out_piece(p):
            psl = slice(p * PEER_DOT_ROWS, (p + 1) * PEER_DOT_ROWS)
            acc_ref[psl, :] += _dot(vt_ref[psl, :], w_old[...])

        def score_piece(p):
            psl = slice(p * PEER_DOT_ROWS, (p + 1) * PEER_DOT_ROWS)
            a_new[psl, :] = _dot(u_ref[psl, :], xt_ref[...])

        units = [(kp, lt, part) for kp in range(nkb // 2) for lt in range(tt // LANES)
                 for part in range(N_KEYS // GATE_UNIT_KEYS)]
        npieces = ec // PEER_DOT_ROWS
        pieces = [f for p in range(npieces) for f in ((out_piece, p), (score_piece, p))]
        per = len(units) // len(pieces)
        assert per * len(pieces) == len(units)
        for n, (fn, p) in enumerate(pieces):
            for kp, lt, part in units[n * per:(n + 1) * per]:
                gate_unit(kp, lt, part)
            fn(p)


    @pl.when(s % 2 == 0)
    def _():
        stages(a0_ref, a1_ref, w1_ref, w0_ref)

    @pl.when(s % 2 == 1)
    def _():
        stages(a1_ref, a0_ref, w0_ref, w1_ref)

    @pl.when(s == nchunks + 1)
    def _():
        z = DN_ALPHA * h1_ref[...] + acc_ref[...].T
        out_ref[...] = _layer_norm(z, g2_ref[...], be2_ref[...])


def _peer(u_bf, vt_bf, h1t, r2, e2, n1, e1, h1, g2, be2, *, ec, tt):
    E = u_bf.shape[0]
    T = h1.shape[0]
    nchunks = E // ec
    last = nchunks - 1
    tokrows = pl.BlockSpec((PEER_HEADS, N_KEYS, tt), lambda t, s: (0, 0, t))
    tokkeys = pl.BlockSpec((N_KEYS // 8, PEER_HEADS, 8, tt), lambda t, s: (0, 0, 0, t))
    vec = pl.BlockSpec((1, D_MODEL), lambda t, s: (0, 0))
    return pl.pallas_call(
        functools.partial(_peer_kernel, ec=ec, tt=tt, nchunks=nchunks),
        grid=(T // tt, nchunks + 2),
        in_specs=[
            pl.BlockSpec((ec, D_MODEL), lambda t, s: (jnp.minimum(s, last), 0)),
            pl.BlockSpec((D_MODEL, ec), lambda t, s: (0, jnp.clip(s - 2, 0, last))),
            pl.BlockSpec((D_MODEL, tt), lambda t, s: (0, t)),
            tokrows, tokrows, tokkeys, tokkeys,
            pl.BlockSpec((tt, D_MODEL), lambda t, s: (t, 0)),
            vec, vec,
        ],
        out_specs=pl.BlockSpec((tt, D_MODEL), lambda t, s: (t, 0)),
        out_shape=jax.ShapeDtypeStruct((T, D_MODEL), F32),
        scratch_shapes=[
            pltpu.VMEM((D_MODEL, tt), F32),
            pltpu.VMEM((ec, tt), F32),
            pltpu.VMEM((ec, tt), F32),
            pltpu.VMEM((ec, tt), BF16),
            pltpu.VMEM((ec, tt), BF16),
            pltpu.VMEM((PEER_HEADS, N_KEYS, tt + GATE_PAD), F32),
            pltpu.VMEM((PEER_HEADS, N_KEYS, tt + GATE_PAD), F32),
            pltpu.VMEM((N_KEYS // 8, PEER_HEADS, 8, tt), F32),
            pltpu.VMEM((N_KEYS // 8, PEER_HEADS, 8, tt), F32),
        ],
        compiler_params=pltpu.CompilerParams(
            dimension_semantics=("arbitrary", "arbitrary"), vmem_limit_bytes=VMEM_LIMIT),
        name="peer",
    )(u_bf, vt_bf, h1t, r2, e2, n1, e1, h1, g2, be2)


def _peer_dense_kernel(u_ref, vt_ref, xt_ref, r2_in, e2_in, n1_in, e1_in, h1_ref, g2_ref, be2_ref,
                       out_ref, acc_ref, a_ref, w_ref, r2_ref, e2_ref, *, ec, tt):
    c = pl.program_id(1)
    nkb = ec // N_KEYS
    assert nkb == 8
    rows = F32_TILE_ROWS
    nrg = GATE_UNIT_KEYS // rows

    @pl.when(c == 0)
    def _():
        acc_ref[...] = jnp.zeros(acc_ref.shape, F32)
        r2_ref[:, :, 0:tt] = r2_in[...].astype(F32)
        e2_ref[:, :, LANES:LANES + tt] = e2_in[...].astype(F32)

    ta = a_ref.shape[1]

    def row_tile(ref, h, ib, sl):
        return jnp.broadcast_to(ref[c, h, ib:ib + 1, sl], (rows, LANES))

    def gate_unit(kp, lt, part):
        lsl = slice(lt * LANES, (lt + 1) * LANES)
        keys = (2 * kp, 2 * kp + 1)
        g = [[None] * nrg for _ in keys]
        for h in range(PEER_HEADS):
            n1b = [row_tile(n1_in, h, ib, lsl) for ib in keys]
            e1b = [row_tile(e1_in, h, ib, lsl) for ib in keys]
            for rg in range(nrg):
                j0 = part * GATE_UNIT_KEYS + rg * rows
                r2 = r2_ref[h, j0:j0 + rows, lsl]
                e2 = e2_ref[h, j0:j0 + rows, (lt + 1) * LANES:(lt + 2) * LANES]
                for k in range(2):
                    term = jnp.where(r2 < n1b[k], e2, 0.0) * e1b[k]
                    g[k][rg] = term if h == 0 else g[k][rg] + term
        for k, ib in enumerate(keys):
            for rp in range(nrg // 2):
                e0 = ib * N_KEYS + part * GATE_UNIT_KEYS + rp * BF16_TILE_ROWS
                esl = slice(e0, e0 + BF16_TILE_ROWS)
                ab = a_ref[esl, (lt * LANES) % ta:(lt * LANES) % ta + LANES]
                ge = 0.5 * ab * (1.0 + lax.erf(ab * (2.0 ** -0.5)))
                gg = jnp.concatenate([g[k][2 * rp], g[k][2 * rp + 1]], axis=0)
                w_ref[esl, lsl] = (ge * gg).astype(BF16)

    for nt in range(tt // ta):
        a_ref[...] = _dot(u_ref[...], xt_ref[:, nt * ta:(nt + 1) * ta])
        for lt in range(nt * ta // LANES, (nt + 1) * ta // LANES):
            for kp in range(nkb // 2):
                for part in range(N_KEYS // GATE_UNIT_KEYS):
                    gate_unit(kp, lt, part)

    acc_ref[...] += _dot(vt_ref[...], w_ref[...])

    @pl.when(c == pl.num_programs(1) - 1)
    def _():
        z = DN_ALPHA * h1_ref[...] + acc_ref[...].T
        out_ref[...] = _layer_norm(z, g2_ref[...], be2_ref[...])


def _peer_dense(u_bf, vt_bf, h1t, r2, e2, n1, e1, h1, g2, be2, *, ec, tt):
    E = u_bf.shape[0]
    T = h1.shape[0]
    tokrows = pl.BlockSpec((PEER_HEADS, N_KEYS, tt), lambda t, c: (0, 0, t))
    tokkeys = pl.BlockSpec((N_KEYS // 8, PEER_HEADS, 8, tt), lambda t, c: (0, 0, 0, t))
    vec = pl.BlockSpec((1, D_MODEL), lambda t, c: (0, 0))
    return pl.pallas_call(
        functools.partial(_peer_dense_kernel, ec=ec, tt=tt),
        grid=(T // tt, E // ec),
        in_specs=[
            pl.BlockSpec((ec, D_MODEL), lambda t, c: (c, 0)),
            pl.BlockSpec((D_MODEL, ec), lambda t, c: (0, c)),
            pl.BlockSpec((D_MODEL, tt), lambda t, c: (0, t)),
            tokrows, tokrows, tokkeys, tokkeys,
            pl.BlockSpec((tt, D_MODEL), lambda t, c: (t, 0)),
            vec, vec,
        ],
        out_specs=pl.BlockSpec((tt, D_MODEL), lambda t, c: (t, 0)),
        out_shape=jax.ShapeDtypeStruct((T, D_MODEL), F32),
        scratch_shapes=[
            pltpu.VMEM((D_MODEL, tt), F32),
            pltpu.VMEM((ec, min(tt, PEER_SCORE_COLS)), F32),
            pltpu.VMEM((ec, tt), BF16),
            pltpu.VMEM((PEER_HEADS, N_KEYS, tt + GATE_PAD), F32),
            pltpu.VMEM((PEER_HEADS, N_KEYS, tt + GATE_PAD), F32),
        ],
        compiler_params=pltpu.CompilerParams(
            dimension_semantics=("arbitrary", "arbitrary"), vmem_limit_bytes=VMEM_LIMIT),
        name="peer",
    )(u_bf, vt_bf, h1t, r2, e2, n1, e1, h1, g2, be2)


def _rope_tables(positions):
    T = positions.size
    half = ROT_DIM // 2
    inv_freq = ROPE_THETA ** (-jnp.arange(0, ROT_DIM, 2, dtype=F32) / ROT_DIM)
    ang = positions.reshape(T, 1).astype(F32) * inv_freq
    cos, sin = jnp.cos(ang), jnp.sin(ang)
    zeros = lambda w: jnp.zeros((T, w), F32)
    rc = jnp.concatenate([cos, cos, jnp.ones((T, HEAD_DIM - ROT_DIM), F32)], axis=1)
    rsa = jnp.concatenate([-sin, zeros(HEAD_DIM - half)], axis=1)
    rsb = jnp.concatenate([zeros(half), sin, zeros(HEAD_DIM - ROT_DIM)], axis=1)
    rep = LANES // HEAD_DIM
    return jnp.tile(rc, (1, rep)), jnp.tile(rsa, (1, rep)), jnp.tile(rsb, (1, rep))


def _tiles(seq, total_tokens):
    tm = 256 if seq % 256 == 0 else BLOCK
    ts = 512 if total_tokens % 512 == 0 else LANES
    tt = 512 if total_tokens % 512 == 0 else 256
    ec = 8 * N_KEYS
    return tm, ts, tt, ec


def kernel(x, positions, w_in, b_in, attn_sinks, conv_w, w_out, b_out, ln1_g, ln1_b,
           w_pq, sub_keys1, sub_keys2, u_experts, v_experts, ln2_g, ln2_b):
    B, S, D = x.shape
    T = B * S
    assert D == D_MODEL and S % BLOCK == 0 and T % 256 == 0
    assert w_in.shape[0] == DEPTH
    tm, ts, tt, ec = _tiles(S, T)
    row = lambda v: v.reshape(1, -1).astype(F32)

    rc, rsa, rsb = _rope_tables(positions)
    h1, h1t, s1t, s2t = _front(
        x.reshape(T, D), rc, rsa, rsb, attn_sinks[0].astype(F32),
        w_in[0].astype(BF16), row(b_in[0]), conv_w[0].astype(F32),
        w_out[0].astype(BF16), row(b_out[0]), row(ln1_g[0]), row(ln1_b[0]),
        w_pq[0].astype(BF16), sub_keys1[0].astype(BF16), sub_keys2[0].astype(BF16),
        seq=S, tm=tm)
    r2, e2, n1, e1 = _select(s1t, s2t, ts=ts)
    out = _peer(u_experts[0].astype(BF16), v_experts[0].T.astype(BF16), h1t,
                r2, e2, n1, e1, h1, row(ln2_g[0]), row(ln2_b[0]), ec=ec, tt=tt)
    return out.reshape(B, S, D)
```

```python
import functools

import jax
import jax.numpy as jnp
from jax import lax
from jax.experimental import pallas as pl
from jax.experimental.pallas import tpu as pltpu

D_MODEL = 1024
HEAD_DIM = 64
N_Q_HEADS = 8
N_KV_HEADS = 2
ATTN_WIDTH = N_Q_HEADS * HEAD_DIM
KV_WIDTH = N_KV_HEADS * HEAD_DIM
BLOCK = 128
ROT_DIM = HEAD_DIM // 4
ROPE_THETA = 500000.0
CONV_WIDTH = D_MODEL - ATTN_WIDTH
CONV_K = 3
N_KEYS = 128
PEER_HEADS = 8
PEER_QDIM = 256
PEER_HALF = PEER_QDIM // 2
PEER_TOPK = 16
DEPTH = 1
DN_ALPHA = (2.0 * DEPTH) ** 0.25
LN_EPS = 1e-5

LANES = 128
BF16_TILE_ROWS = 16
PEER_DOT_ROWS = 128
PEER_SCORE_COLS = 512
F32_TILE_ROWS = 8
BF16_ROWS = 8
BF16_COLS = 256
GATE_UNIT_KEYS = 64
GATE_PAD = 2 * LANES
VMEM_LIMIT = 56 * 1024 * 1024

_O_K = ATTN_WIDTH
_O_V = _O_K + KV_WIDTH
_O_GB = _O_V + KV_WIDTH
_O_GC = _O_GB + CONV_WIDTH
_O_XC = _O_GC + CONV_WIDTH
_O_END = _O_XC + CONV_WIDTH

BF16 = jnp.bfloat16
F32 = jnp.float32
NEG_INF = float("-inf")


def _dot(a, b):
    return jnp.dot(a, b, preferred_element_type=F32)


def _dot_nt(a, b):
    return lax.dot_general(a, b, (((1,), (1,)), ((), ())), preferred_element_type=F32)


def _layer_norm(z, g, b):
    mu = jnp.mean(z, axis=-1, keepdims=True)
    zc = z - mu
    var = jnp.mean(zc * zc, axis=-1, keepdims=True)
    return zc * lax.rsqrt(var + LN_EPS) * g + b


def _front_kernel(sinks_ref, x_ref, rc_ref, rsa_ref, rsb_ref, w_in_ref, b_in_ref,
                  conv_w_ref, w_out_ref, b_out_ref, g1_ref, be1_ref, w_pq_ref,
                  k1_ref, k2_ref,
                  h1_ref, h1t_ref, s1_ref, s2_ref,
                  kbuf, vbuf, ubuf, kvar, vvar, ybuf, *, tm, tiles_per_seq):
    step = pl.program_id(0)
    nblk = tm // BLOCK

    @pl.when(step % tiles_per_seq == 0)
    def _():
        kbuf[0:BLOCK, :] = jnp.zeros((BLOCK, KV_WIDTH), F32)
        vbuf[0:BLOCK, :] = jnp.zeros((BLOCK, KV_WIDTH), F32)
        ubuf[0:8, :] = jnp.zeros((8, CONV_WIDTH), F32)

    x = x_ref[...]
    xb = x.astype(BF16)

    def proj(lo, hi):
        return _dot(xb, w_in_ref[:, lo:hi]) + b_in_ref[:, lo:hi]

    rc = rc_ref[...]
    rsa = rsa_ref[...]
    rsb = rsb_ref[...]

    def rope(t):
        return (t * rc + pltpu.roll(t, LANES - ROT_DIM // 2, 1) * rsa
                + pltpu.roll(t, ROT_DIM // 2, 1) * rsb)

    kbuf[BLOCK:BLOCK + tm, :] = rope(proj(_O_K, _O_V))
    vbuf[BLOCK:BLOCK + tm, :] = proj(_O_V, _O_GB)

    lane = lax.broadcasted_iota(jnp.int32, (tm + BLOCK, LANES), 1)
    lo = lane < HEAD_DIM
    for buf, var in ((kbuf, kvar), (vbuf, vvar)):
        t = buf[...]
        tr = pltpu.roll(t, HEAD_DIM, 1)
        var[0] = jnp.where(lo, t, 0.0).astype(BF16)
        var[1] = jnp.where(lo, 0.0, tr).astype(BF16)
        var[2] = jnp.where(lo, tr, 0.0).astype(BF16)
        var[3] = jnp.where(lo, 0.0, t).astype(BF16)

    qi = lax.broadcasted_iota(jnp.int32, (BLOCK, 2 * BLOCK), 0)
    kj = lax.broadcasted_iota(jnp.int32, (BLOCK, 2 * BLOCK), 1)
    diff = qi + BLOCK - kj
    band = (diff >= 0) & (diff < BLOCK)

    for p in range(N_Q_HEADS // 2):
        q_slab = (rope(proj(p * LANES, (p + 1) * LANES)) * (HEAD_DIM ** -0.5)).astype(BF16)
        for blk in range(nblk):
            first = (step * nblk + blk) % (tiles_per_seq * nblk) == 0
            kmin = jnp.where(first, BLOCK, 0)
            mask = band & (kj >= kmin)
            qs = q_slab[blk * BLOCK:(blk + 1) * BLOCK, :]
            o = jnp.zeros((BLOCK, LANES), F32)
            for hh in range(2):
                h = 2 * p + hh
                var = (h // (N_Q_HEADS // N_KV_HEADS)) * 2 + hh
                kc = kvar[var, blk * BLOCK:(blk + 2) * BLOCK, :]
                vc = vvar[var, blk * BLOCK:(blk + 2) * BLOCK, :]
                s = jnp.where(mask, _dot_nt(qs, kc), NEG_INF)
                sink = sinks_ref[h]
                m = jnp.maximum(jnp.max(s, axis=-1, keepdims=True), sink)
                pe = jnp.exp(s - m)
                den = jnp.sum(pe, axis=-1, keepdims=True) + jnp.exp(sink - m)
                o = o + _dot((pe / den).astype(BF16), vc)
            ybuf[blk * BLOCK:(blk + 1) * BLOCK, p * LANES:(p + 1) * LANES] = o.astype(BF16)

    u = proj(_O_GC, _O_XC) * proj(_O_XC, _O_END)
    ubuf[8:8 + tm, :] = u
    um1 = ubuf[7:7 + tm, :]
    um2 = ubuf[6:6 + tm, :]
    cw = conv_w_ref[...]
    yc = proj(_O_GB, _O_GC) * (cw[0:1, :] * um2 + cw[1:2, :] * um1 + cw[2:3, :] * u)
    ybuf[:, ATTN_WIDTH:] = yc.astype(BF16)

    ubuf[0:8, :] = ubuf[tm:tm + 8, :]
    kbuf[0:BLOCK, :] = kbuf[tm:tm + BLOCK, :]
    vbuf[0:BLOCK, :] = vbuf[tm:tm + BLOCK, :]

    mix = _dot(ybuf[...], w_out_ref[...]) + b_out_ref[...]
    h1 = _layer_norm(DN_ALPHA * x + mix, g1_ref[...], be1_ref[...])
    h1_ref[...] = h1
    h1t_ref[...] = h1.T.astype(BF16)

    qp = _dot(h1.astype(BF16), w_pq_ref[...]).astype(BF16)
    k1 = k1_ref[...]
    k2 = k2_ref[...]
    for h in range(PEER_HEADS):
        base = h * PEER_QDIM
        s1_ref[h] = _dot_nt(k1, qp[:, base:base + PEER_HALF])
        s2_ref[h] = _dot_nt(k2, qp[:, base + PEER_HALF:base + PEER_QDIM])


def _front(xf, rc, rsa, rsb, sinks, w_in, b_in, conv_w, w_out, b_out, g1, be1, w_pq, k1, k2,
           *, seq, tm):
    T = xf.shape[0]
    proj_w = w_in.shape[1]
    tiles_per_seq = seq // tm
    full = lambda shape: pl.BlockSpec(shape, lambda i: (0,) * len(shape))
    tok = lambda w: pl.BlockSpec((tm, w), lambda i: (i, 0))
    kern = functools.partial(_front_kernel, tm=tm, tiles_per_seq=tiles_per_seq)
    return pl.pallas_call(
        kern,
        grid=(T // tm,),
        in_specs=[
            pl.BlockSpec(memory_space=pltpu.SMEM),
            tok(D_MODEL), tok(LANES), tok(LANES), tok(LANES),
            full((D_MODEL, proj_w)), full((1, proj_w)),
            full((CONV_K, CONV_WIDTH)), full((D_MODEL, D_MODEL)), full((1, D_MODEL)),
            full((1, D_MODEL)), full((1, D_MODEL)),
            full((D_MODEL, PEER_HEADS * PEER_QDIM)),
            full((N_KEYS, PEER_HALF)), full((N_KEYS, PEER_HALF)),
        ],
        out_specs=[
            tok(D_MODEL),
            pl.BlockSpec((D_MODEL, tm), lambda i: (0, i)),
            pl.BlockSpec((PEER_HEADS, N_KEYS, tm), lambda i: (0, 0, i)),
            pl.BlockSpec((PEER_HEADS, N_KEYS, tm), lambda i: (0, 0, i)),
        ],
        out_shape=[
            jax.ShapeDtypeStruct((T, D_MODEL), F32),
            jax.ShapeDtypeStruct((D_MODEL, T), BF16),
            jax.ShapeDtypeStruct((PEER_HEADS, N_KEYS, T), F32),
            jax.ShapeDtypeStruct((PEER_HEADS, N_KEYS, T), F32),
        ],
        scratch_shapes=[
            pltpu.VMEM((tm + BLOCK, KV_WIDTH), F32),
            pltpu.VMEM((tm + BLOCK, KV_WIDTH), F32),
            pltpu.VMEM((tm + 8, CONV_WIDTH), F32),
            pltpu.VMEM((4, tm + BLOCK, KV_WIDTH), BF16),
            pltpu.VMEM((4, tm + BLOCK, KV_WIDTH), BF16),
            pltpu.VMEM((tm, D_MODEL), BF16),
        ],
        compiler_params=pltpu.CompilerParams(
            dimension_semantics=("arbitrary",), vmem_limit_bytes=VMEM_LIMIT),
        name="front",
    )(sinks, xf, rc, rsa, rsb, w_in, b_in, conv_w, w_out, b_out, g1, be1, w_pq, k1, k2)


def _top16(s):
    rows = lax.broadcasted_iota(jnp.int32, s.shape, 0)
    rem = s
    rank = jnp.full(s.shape, float(PEER_TOPK), F32)
    vals = []
    for k in range(PEER_TOPK):
        m = jnp.max(rem, axis=0, keepdims=True)
        idx = jnp.min(jnp.where(rem == m, rows, N_KEYS), axis=0, keepdims=True)
        sel = rows == idx
        rem = jnp.where(sel, NEG_INF, rem)
        rank = jnp.where(sel, float(k), rank)
        vals.append(m)
    return vals, rank


_N_CAND_ROWS = PEER_TOPK + 7 * 8 + 8


def _cand_positions():
    import numpy as np
    pos = np.full((_N_CAND_ROWS,), -1, np.int32)
    for r in range(_N_CAND_ROWS):
        if r < 16:
            a, b = 0, r
        elif r < 72:
            a, b = 1 + (r - 16) // 8, (r - 16) % 8
        else:
            a, b = 8 + (r - 72), 0
        if (a + 1) * (b + 1) <= PEER_TOPK:
            pos[r] = a * PEER_TOPK + b
    return pos


def _select_exact(s1, s2, pos, valid, row16):
    v1, rank1 = _top16(s1)
    v2, rank2 = _top16(s2)

    v1m = jnp.zeros((PEER_TOPK, LANES), F32)
    v2m = jnp.zeros((PEER_TOPK, LANES), F32)
    for a in range(PEER_TOPK):
        v1m = jnp.where(row16 == a, v1[a], v1m)
        v2m = jnp.where(row16 == a, v2[a], v2m)
    blocks = [v1[0] + v2m]
    for a in range(1, 8):
        blocks.append(v1[a] + v2m[0:8, :])
    blocks.append(v1m[8:16, :] + v2[0])
    cand = jnp.where(valid, jnp.concatenate(blocks, axis=0), NEG_INF)

    rem = cand
    chosen = jnp.zeros(cand.shape, F32)
    for _ in range(PEER_TOPK):
        m = jnp.max(rem, axis=0, keepdims=True)
        idx = jnp.min(jnp.where(rem == m, pos, 1 << 20), axis=0, keepdims=True)
        sel = (pos == idx) & valid
        rem = jnp.where(sel, NEG_INF, rem)
        chosen = jnp.where(sel, 1.0, chosen)

    top = v1[0] + v2[0]
    z = jnp.sum(chosen * jnp.exp(jnp.where(valid, cand, top) - top), axis=0, keepdims=True)

    counts = [jnp.sum(chosen[0:16, :], axis=0, keepdims=True)]
    for a in range(1, 8):
        counts.append(jnp.sum(chosen[8 + 8 * a:16 + 8 * a, :], axis=0, keepdims=True))
    for a in range(8, 16):
        counts.append(chosen[64 + a:65 + a, :])
    n1 = jnp.zeros(s1.shape, F32)
    for a in range(PEER_TOPK):
        n1 = jnp.where(rank1 == float(a), counts[a], n1)
    return n1, jnp.exp(s1 - v1[0]), rank2, jnp.exp(s2 - v2[0]) / z


def _sort_pairs(n):
    pairs = []
    p = 1
    while p < n:
        k = p
        while k >= 1:
            for j in range(k % p, n - k, 2 * k):
                for i in range(min(k, n - j - k)):
                    if (i + j) // (2 * p) == (i + j + k) // (2 * p):
                        pairs.append((i + j, i + j + k))
            k //= 2
        p *= 2
    return pairs


def _hi_lo(a, b):
    if a is None:
        return b, None
    if b is None:
        return a, None
    return jnp.maximum(a, b), jnp.minimum(a, b)


def _sublane_all(x, op):
    for shift in (4, 2, 1):
        x = op(x, pltpu.roll(x, shift, 0))
    return x


def _top16_sorted(x):
    x = list(x)
    for i, j in _sort_pairs(len(x)):
        x[i], x[j] = _hi_lo(x[i], x[j])
    for shift in (4, 2, 1):
        y = [None if t is None else pltpu.roll(t, shift, 0) for t in x]
        c = [_hi_lo(x[r], y[PEER_TOPK - 1 - r])[0] for r in range(PEER_TOPK)]
        d = PEER_TOPK // 2
        while d >= 1:
            for i in range(PEER_TOPK):
                if not i & d:
                    c[i], c[i + d] = _hi_lo(c[i], c[i + d])
            d //= 2
        x = c
    return x


def _select_fast(s1, s2):
    nv = N_KEYS // 8
    t1 = [s1[8 * v:8 * v + 8, :] for v in range(nv)]
    t2 = [s2[8 * v:8 * v + 8, :] for v in range(nv)]
    v1 = _top16_sorted(t1)
    v2 = _top16_sorted(t2)
    sub = lax.broadcasted_iota(jnp.int32, (8, LANES), 0)
    one = jnp.ones((8, LANES), F32)
    zero = jnp.zeros((8, LANES), F32)

    def spread(vals):
        out = vals[0]
        for r in range(1, 8):
            out = jnp.where(sub == r, vals[r], out)
        return out

    v2lo, v2hi, v1hi = spread(v2[0:8]), spread(v2[8:16]), spread(v1[8:16])
    cand = [v1[0] + v2lo, v1[0] + v2hi]
    for a in range(1, 8):
        cand.append(jnp.where(sub < PEER_TOPK // (a + 1), v1[a] + v2lo, NEG_INF))
    cand.append(v1hi + v2[0])
    best = _top16_sorted(cand + [None] * (PEER_TOPK - len(cand)))
    tau = best[PEER_TOPK - 1]
    z = one
    for k in range(1, PEER_TOPK):
        z = z + jnp.exp(best[k] - best[0])

    count = lambda t: _sublane_all(jnp.where(t >= tau, one, zero), jnp.add)
    counts = [count(cand[0]) + count(cand[1])]
    counts += [count(cand[a + 1]) for a in range(1, 8)]
    counts += [jnp.where(v1[a] + v2[0] >= tau, one, zero) for a in range(8, PEER_TOPK)]
    total = counts[0]
    for a in range(1, PEER_TOPK):
        total = total + counts[a]
    tie = jnp.where(total != float(PEER_TOPK), one, zero)

    n1 = [zero] * nv
    rank2 = [jnp.full((8, LANES), float(PEER_TOPK), F32)] * nv
    for a in reversed(range(PEER_TOPK)):
        n1 = [jnp.where(t >= v1[a], counts[a], n) for t, n in zip(t1, n1)]
        rank2 = [jnp.where(t >= v2[a], float(a), r) for t, r in zip(t2, rank2)]
    for v, tiles in ((v1, t1), (v2, t2)):
        inside = zero
        for t in tiles:
            inside = inside + jnp.where(t >= v[PEER_TOPK - 1], one, zero)
        tie = jnp.where(_sublane_all(inside, jnp.add) != float(PEER_TOPK), one, tie)
        for a in range(PEER_TOPK - 1):
            tie = jnp.where(v[a] == v[a + 1], one, tie)

    inv_z = 1.0 / z
    cat = lambda tiles: jnp.concatenate(tiles, axis=0)
    e1 = cat([jnp.exp(t - v1[0]) for t in t1])
    e2 = cat([jnp.exp(t - v2[0]) * inv_z for t in t2])
    return cat(n1), e1, cat(rank2), e2, tie


def _select_kernel(pos_ref, s1_ref, s2_ref, r2_ref, e2_ref, n1_ref, e1_ref, *, ts):
    def group(g, carry):
        sl = pl.ds(pl.multiple_of(g * LANES, LANES), LANES)
        s1 = s1_ref[0, :, sl]
        s2 = s2_ref[0, :, sl]
        n1, e1, rank2, e2, tie = _select_fast(s1, s2)

        def exact():
            pos = pos_ref[...]
            row16 = lax.broadcasted_iota(jnp.int32, (PEER_TOPK, LANES), 0)
            return _select_exact(s1, s2, pos, pos >= 0, row16)

        n1, e1, rank2, e2 = lax.cond(jnp.max(tie) > 0.0, exact, lambda: (n1, e1, rank2, e2))
        n1_ref[:, 0, :, sl] = n1.reshape(N_KEYS // 8, 8, LANES)
        e1_ref[:, 0, :, sl] = e1.reshape(N_KEYS // 8, 8, LANES)
        r2_ref[0, :, sl] = rank2.astype(BF16)
        e2_ref[0, :, sl] = e2.astype(BF16)
        return carry

    lax.fori_loop(0, ts // LANES, group, 0)


def _select(s1t, s2t, *, ts):
    H, K, T = s1t.shape
    import numpy as np
    pos = jnp.asarray(np.tile(_cand_positions()[:, None], (1, LANES)))
    blk = pl.BlockSpec((1, K, ts), lambda t, h: (h, 0, t))
    blk2 = blk
    blk1 = pl.BlockSpec((K // 8, 1, 8, ts), lambda t, h: (0, h, 0, t))
    return pl.pallas_call(
        functools.partial(_select_kernel, ts=ts),
        grid=(T // ts, H),
        in_specs=[pl.BlockSpec((_N_CAND_ROWS, LANES), lambda t, h: (0, 0)), blk, blk],
        out_specs=[blk2, blk2, blk1, blk1],
        out_shape=[
            jax.ShapeDtypeStruct((H, K, T), BF16),
            jax.ShapeDtypeStruct((H, K, T), BF16),
            jax.ShapeDtypeStruct((K // 8, H, 8, T), F32),
            jax.ShapeDtypeStruct((K // 8, H, 8, T), F32),
        ],
        compiler_params=pltpu.CompilerParams(
            dimension_semantics=("arbitrary", "arbitrary"), vmem_limit_bytes=VMEM_LIMIT),
        name="select",
    )(pos, s1t, s2t)


def _peer_kernel(u_ref, vt_ref, xt_ref, r2_in, e2_in, n1_in, e1_in, h1_ref, g2_ref, be2_ref,
                 out_ref, acc_ref, a0_ref, a1_ref, w0_ref, w1_ref,
                 r2_ref, e2_ref, n1_ref, e1_ref, *, ec, tt, nchunks):
    s = pl.program_id(1)
    nkb = ec // N_KEYS
    assert nkb == 8
    rows = BF16_TILE_ROWS

    @pl.when(s == 0)
    def _():
        acc_ref[...] = jnp.zeros(acc_ref.shape, F32)
        for ref in (a0_ref, a1_ref, w0_ref, w1_ref):
            ref[...] = jnp.zeros(ref.shape, ref.dtype)
        r2_ref[:, :, 0:tt] = r2_in[...].astype(F32)
        e2_ref[:, :, LANES:LANES + tt] = e2_in[...].astype(F32)
        n1_ref[...] = n1_in[...]
        e1_ref[...] = e1_in[...]

    cg = jnp.clip(s - 1, 0, nchunks - 1)

    def stages(a_new, a_prev, w_new, w_old):
        rows8 = F32_TILE_ROWS
        nrg = GATE_UNIT_KEYS // rows8

        def row_tile(ref, h, ib, sl):
            return jnp.broadcast_to(ref[cg, h, ib:ib + 1, sl], (rows8, LANES))

        def gate_unit(kp, lt, part):
            lsl = slice(lt * LANES, (lt + 1) * LANES)
            keys = (2 * kp, 2 * kp + 1)
            g = [[None] * nrg for _ in keys]
            for h in range(PEER_HEADS):
                n1b = [row_tile(n1_ref, h, ib, lsl) for ib in keys]
                e1b = [row_tile(e1_ref, h, ib, lsl) for ib in keys]
                for rg in range(nrg):
                    j0 = part * GATE_UNIT_KEYS + rg * rows8
                    r2 = r2_ref[h, j0:j0 + rows8, lsl]
                    e2 = e2_ref[h, j0:j0 + rows8, (lt + 1) * LANES:(lt + 2) * LANES]
                    for k in range(2):
                        term = jnp.where(r2 < n1b[k], e2, 0.0) * e1b[k]
                        g[k][rg] = term if h == 0 else g[k][rg] + term
            for k, ib in enumerate(keys):
                for rp in range(nrg // 2):
                    e0 = ib * N_KEYS + part * GATE_UNIT_KEYS + rp * BF16_TILE_ROWS
                    esl = slice(e0, e0 + BF16_TILE_ROWS)
                    ab = a_prev[esl, lsl]
                    ge = 0.5 * ab * (1.0 + lax.erf(ab * (2.0 ** -0.5)))
                    gg = jnp.concatenate([g[k][2 * rp], g[k][2 * rp + 1]], axis=0)
                    w_new[esl, lsl] = (ge * gg).astype(BF16)

        def out_piece(p):
            psl = slice(p * PEER_DOT_ROWS, (p + 1) * PEER_DOT_ROWS)
            acc_ref[psl, :] += _dot(vt_ref[psl, :], w_old[...])

        def score_piece(p):
            psl = slice(p * PEER_DOT_ROWS, (p + 1) * PEER_DOT_ROWS)
            a_new[psl, :] = _dot(u_ref[psl, :], xt_ref[...])

        units = [(kp, lt, part) for kp in range(nkb // 2) for lt in range(tt // LANES)
                 for part in range(N_KEYS // GATE_UNIT_KEYS)]
        npieces = ec // PEER_DOT_ROWS
        pieces = [f for p in range(npieces) for f in ((out_piece, p), (score_piece, p))]
        per = len(units) // len(pieces)
        assert per * len(pieces) == len(units)
        for n, (fn, p) in enumerate(pieces):
            for kp, lt, part in units[n * per:(n + 1) * per]:
                gate_unit(kp, lt, part)
            fn(p)


    @pl.when(s % 2 == 0)
    def _():
        stages(a0_ref, a1_ref, w1_ref, w0_ref)

    @pl.when(s % 2 == 1)
    def _():
        stages(a1_ref, a0_ref, w0_ref, w1_ref)

    @pl.when(s == nchunks + 1)
    def _():
        z = DN_ALPHA * h1_ref[...] + acc_ref[...].T
        out_ref[...] = _layer_norm(z, g2_ref[...], be2_ref[...])


def _peer(u_bf, vt_bf, h1t, r2, e2, n1, e1, h1, g2, be2, *, ec, tt):
    E = u_bf.shape[0]
    T = h1.shape[0]
    nchunks = E // ec
    last = nchunks - 1
    tokrows = pl.BlockSpec((PEER_HEADS, N_KEYS, tt), lambda t, s: (0, 0, t))
    tokkeys = pl.BlockSpec((N_KEYS // 8, PEER_HEADS, 8, tt), lambda t, s: (0, 0, 0, t))
    vec = pl.BlockSpec((1, D_MODEL), lambda t, s: (0, 0))
    return pl.pallas_call(
        functools.partial(_peer_kernel, ec=ec, tt=tt, nchunks=nchunks),
        grid=(T // tt, nchunks + 2),
        in_specs=[
            pl.BlockSpec((ec, D_MODEL), lambda t, s: (jnp.minimum(s, last), 0)),
            pl.BlockSpec((D_MODEL, ec), lambda t, s: (0, jnp.clip(s - 2, 0, last))),
            pl.BlockSpec((D_MODEL, tt), lambda t, s: (0, t)),
            tokrows, tokrows, tokkeys, tokkeys,
            pl.BlockSpec((tt, D_MODEL), lambda t, s: (t, 0)),
            vec, vec,
        ],
        out_specs=pl.BlockSpec((tt, D_MODEL), lambda t, s: (t, 0)),
        out_shape=jax.ShapeDtypeStruct((T, D_MODEL), F32),
        scratch_shapes=[
            pltpu.VMEM((D_MODEL, tt), F32),
            pltpu.VMEM((ec, tt), F32),
            pltpu.VMEM((ec, tt), F32),
            pltpu.VMEM((ec, tt), BF16),
            pltpu.VMEM((ec, tt), BF16),
            pltpu.VMEM((PEER_HEADS, N_KEYS, tt + GATE_PAD), F32),
            pltpu.VMEM((PEER_HEADS, N_KEYS, tt + GATE_PAD), F32),
            pltpu.VMEM((N_KEYS // 8, PEER_HEADS, 8, tt), F32),
            pltpu.VMEM((N_KEYS // 8, PEER_HEADS, 8, tt), F32),
        ],
        compiler_params=pltpu.CompilerParams(
            dimension_semantics=("arbitrary", "arbitrary"), vmem_limit_bytes=VMEM_LIMIT),
        name="peer",
    )(u_bf, vt_bf, h1t, r2, e2, n1, e1, h1, g2, be2)


def _peer_dense_kernel(u_ref, vt_ref, xt_ref, r2_in, e2_in, n1_in, e1_in, h1_ref, g2_ref, be2_ref,
                       out_ref, acc_ref, a_ref, w_ref, r2_ref, e2_ref, *, ec, tt):
    c = pl.program_id(1)
    nkb = ec // N_KEYS
    assert nkb == 8
    rows = F32_TILE_ROWS
    nrg = GATE_UNIT_KEYS // rows

    @pl.when(c == 0)
    def _():
        acc_ref[...] = jnp.zeros(acc_ref.shape, F32)
        r2_ref[:, :, 0:tt] = r2_in[...].astype(F32)
        e2_ref[:, :, LANES:LANES + tt] = e2_in[...].astype(F32)

    ta = a_ref.shape[1]

    def row_tile(ref, h, ib, sl):
        return jnp.broadcast_to(ref[c, h, ib:ib + 1, sl], (rows, LANES))

    def gate_unit(kp, lt, part):
        lsl = slice(lt * LANES, (lt + 1) * LANES)
        keys = (2 * kp, 2 * kp + 1)
        g = [[None] * nrg for _ in keys]
        for h in range(PEER_HEADS):
            n1b = [row_tile(n1_in, h, ib, lsl) for ib in keys]
            e1b = [row_tile(e1_in, h, ib, lsl) for ib in keys]
            for rg in range(nrg):
                j0 = part * GATE_UNIT_KEYS + rg * rows
                r2 = r2_ref[h, j0:j0 + rows, lsl]
                e2 = e2_ref[h, j0:j0 + rows, (lt + 1) * LANES:(lt + 2) * LANES]
                for k in range(2):
                    term = jnp.where(r2 < n1b[k], e2, 0.0) * e1b[k]
                    g[k][rg] = term if h == 0 else g[k][rg] + term
        for k, ib in enumerate(keys):
            for rp in range(nrg // 2):
                e0 = ib * N_KEYS + part * GATE_UNIT_KEYS + rp * BF16_TILE_ROWS
                esl = slice(e0, e0 + BF16_TILE_ROWS)
                ab = a_ref[esl, (lt * LANES) % ta:(lt * LANES) % ta + LANES]
                ge = 0.5 * ab * (1.0 + lax.erf(ab * (2.0 ** -0.5)))
                gg = jnp.concatenate([g[k][2 * rp], g[k][2 * rp + 1]], axis=0)
                w_ref[esl, lsl] = (ge * gg).astype(BF16)

    for nt in range(tt // ta):
        a_ref[...] = _dot(u_ref[...], xt_ref[:, nt * ta:(nt + 1) * ta])
        for lt in range(nt * ta // LANES, (nt + 1) * ta // LANES):
            for kp in range(nkb // 2):
                for part in range(N_KEYS // GATE_UNIT_KEYS):
                    gate_unit(kp, lt, part)

    acc_ref[...] += _dot(vt_ref[...], w_ref[...])

    @pl.when(c == pl.num_programs(1) - 1)
    def _():
        z = DN_ALPHA * h1_ref[...] + acc_ref[...].T
        out_ref[...] = _layer_norm(z, g2_ref[...], be2_ref[...])


def _peer_dense(u_bf, vt_bf, h1t, r2, e2, n1, e1, h1, g2, be2, *, ec, tt):
    E = u_bf.shape[0]
    T = h1.shape[0]
    tokrows = pl.BlockSpec((PEER_HEADS, N_KEYS, tt), lambda t, c: (0, 0, t))
    tokkeys = pl.BlockSpec((N_KEYS // 8, PEER_HEADS, 8, tt), lambda t, c: (0, 0, 0, t))
    vec = pl.BlockSpec((1, D_MODEL), lambda t, c: (0, 0))
    return pl.pallas_call(
        functools.partial(_peer_dense_kernel, ec=ec, tt=tt),
        grid=(T // tt, E // ec),
        in_specs=[
            pl.BlockSpec((ec, D_MODEL), lambda t, c: (c, 0)),
            pl.BlockSpec((D_MODEL, ec), lambda t, c: (0, c)),
            pl.BlockSpec((D_MODEL, tt), lambda t, c: (0, t)),
            tokrows, tokrows, tokkeys, tokkeys,
            pl.BlockSpec((tt, D_MODEL), lambda t, c: (t, 0)),
            vec, vec,
        ],
        out_specs=pl.BlockSpec((tt, D_MODEL), lambda t, c: (t, 0)),
        out_shape=jax.ShapeDtypeStruct((T, D_MODEL), F32),
        scratch_shapes=[
            pltpu.VMEM((D_MODEL, tt), F32),
            pltpu.VMEM((ec, min(tt, PEER_SCORE_COLS)), F32),
            pltpu.VMEM((ec, tt), BF16),
            pltpu.VMEM((PEER_HEADS, N_KEYS, tt + GATE_PAD), F32),
            pltpu.VMEM((PEER_HEADS, N_KEYS, tt + GATE_PAD), F32),
        ],
        compiler_params=pltpu.CompilerParams(
            dimension_semantics=("arbitrary", "arbitrary"), vmem_limit_bytes=VMEM_LIMIT),
        name="peer",
    )(u_bf, vt_bf, h1t, r2, e2, n1, e1, h1, g2, be2)


def _peer_slab_kernel(u_ref, vt_ref, xt_ref, r2_ref, e2_ref, n1_ref, e1_ref, h1_ref, g2_ref, be2_ref,
                      out_ref, acc_ref, w_ref, *, ec, tt):
    c = pl.program_id(1)
    nkb = ec // N_KEYS
    assert nkb == 8

    @pl.when(c == 0)
    def _():
        acc_ref[...] = jnp.zeros(acc_ref.shape, F32)

    def row_bf16(ref, h, ib, sl):
        return jnp.broadcast_to(ref[c, h, ib:ib + 1, sl], (BF16_ROWS, BF16_COLS)).astype(BF16)

    rows = BF16_ROWS
    for nt in range(tt // BF16_COLS):
        nsl = slice(nt * BF16_COLS, (nt + 1) * BF16_COLS)
        a = _dot(u_ref[...], xt_ref[:, nsl])
        for ib in range(nkb):
            n1b = [row_bf16(n1_ref, h, ib, nsl) for h in range(PEER_HEADS)]
            e1b = [row_bf16(e1_ref, h, ib, nsl) for h in range(PEER_HEADS)]
            for rg in range(N_KEYS // rows):
                ksl = slice(rg * rows, (rg + 1) * rows)
                g = jnp.zeros((rows, BF16_COLS), BF16)
                for h in range(PEER_HEADS):
                    e2 = e2_ref[h, ksl, nsl]
                    g = g + jnp.where(r2_ref[h, ksl, nsl] < n1b[h], e2, jnp.zeros_like(e2)) * e1b[h]
                esl = slice(ib * N_KEYS + rg * rows, ib * N_KEYS + (rg + 1) * rows)
                ab = a[esl, :]
                ge = 0.5 * ab * (1.0 + lax.erf(ab * (2.0 ** -0.5)))
                w_ref[esl, nsl] = ge.astype(BF16) * g
        acc_ref[:, nsl] += _dot(vt_ref[...], w_ref[:, nsl])

    @pl.when(c == pl.num_programs(1) - 1)
    def _():
        z = DN_ALPHA * h1_ref[...] + acc_ref[...].T
        out_ref[...] = _layer_norm(z, g2_ref[...], be2_ref[...])


def _peer_slab(u_bf, vt_bf, h1t, r2, e2, n1, e1, h1, g2, be2, *, ec, tt):
    E = u_bf.shape[0]
    T = h1.shape[0]
    tokrows = pl.BlockSpec((PEER_HEADS, N_KEYS, tt), lambda t, c: (0, 0, t))
    tokkeys = pl.BlockSpec((N_KEYS // 8, PEER_HEADS, 8, tt), lambda t, c: (0, 0, 0, t))
    vec = pl.BlockSpec((1, D_MODEL), lambda t, c: (0, 0))
    return pl.pallas_call(
        functools.partial(_peer_slab_kernel, ec=ec, tt=tt),
        grid=(T // tt, E // ec),
        in_specs=[
            pl.BlockSpec((ec, D_MODEL), lambda t, c: (c, 0)),
            pl.BlockSpec((D_MODEL, ec), lambda t, c: (0, c)),
            pl.BlockSpec((D_MODEL, tt), lambda t, c: (0, t)),
            tokrows, tokrows, tokkeys, tokkeys,
            pl.BlockSpec((tt, D_MODEL), lambda t, c: (t, 0)),
            vec, vec,
        ],
        out_specs=pl.BlockSpec((tt, D_MODEL), lambda t, c: (t, 0)),
        out_shape=jax.ShapeDtypeStruct((T, D_MODEL), F32),
        scratch_shapes=[
            pltpu.VMEM((D_MODEL, tt), F32),
            pltpu.VMEM((ec, tt), BF16),
        ],
        compiler_params=pltpu.CompilerParams(
            dimension_semantics=("arbitrary", "arbitrary"), vmem_limit_bytes=VMEM_LIMIT),
        name="peer",
    )(u_bf, vt_bf, h1t, r2, e2, n1, e1, h1, g2, be2)


def _rope_tables(positions):
    T = positions.size
    half = ROT_DIM // 2
    inv_freq = ROPE_THETA ** (-jnp.arange(0, ROT_DIM, 2, dtype=F32) / ROT_DIM)
    ang = positions.reshape(T, 1).astype(F32) * inv_freq
    cos, sin = jnp.cos(ang), jnp.sin(ang)
    zeros = lambda w: jnp.zeros((T, w), F32)
    rc = jnp.concatenate([cos, cos, jnp.ones((T, HEAD_DIM - ROT_DIM), F32)], axis=1)
    rsa = jnp.concatenate([-sin, zeros(HEAD_DIM - half)], axis=1)
    rsb = jnp.concatenate([zeros(half), sin, zeros(HEAD_DIM - ROT_DIM)], axis=1)
    rep = LANES // HEAD_DIM
    return jnp.tile(rc, (1, rep)), jnp.tile(rsa, (1, rep)), jnp.tile(rsb, (1, rep))


def _tiles(seq, total_tokens):
    tm = 256 if seq % 256 == 0 else BLOCK
    ts = 512 if total_tokens % 512 == 0 else LANES
    tt = 512 if total_tokens % 512 == 0 else 256
    ec = 8 * N_KEYS
    return tm, ts, tt, ec


def kernel(x, positions, w_in, b_in, attn_sinks, conv_w, w_out, b_out, ln1_g, ln1_b,
           w_pq, sub_keys1, sub_keys2, u_experts, v_experts, ln2_g, ln2_b):
    B, S, D = x.shape
    T = B * S
    assert D == D_MODEL and S % BLOCK == 0 and T % 256 == 0
    assert w_in.shape[0] == DEPTH
    tm, ts, tt, ec = _tiles(S, T)
    row = lambda v: v.reshape(1, -1).astype(F32)

    rc, rsa, rsb = _rope_tables(positions)
    h1, h1t, s1t, s2t = _front(
        x.reshape(T, D), rc, rsa, rsb, attn_sinks[0].astype(F32),
        w_in[0].astype(BF16), row(b_in[0]), conv_w[0].astype(F32),
        w_out[0].astype(BF16), row(b_out[0]), row(ln1_g[0]), row(ln1_b[0]),
        w_pq[0].astype(BF16), sub_keys1[0].astype(BF16), sub_keys2[0].astype(BF16),
        seq=S, tm=tm)
    r2, e2, n1, e1 = _select(s1t, s2t, ts=ts)
    out = _peer_slab(u_experts[0].astype(BF16), v_experts[0].T.astype(BF16), h1t,
                r2, e2, n1, e1, h1, row(ln2_g[0]), row(ln2_b[0]), ec=ec, tt=tt)
    return out.reshape(B, S, D)
```

```python
import functools

import jax
import jax.numpy as jnp
from jax import lax
from jax.experimental import pallas as pl
from jax.experimental.pallas import tpu as pltpu

D_MODEL = 1024
HEAD_DIM = 64
N_Q_HEADS = 8
N_KV_HEADS = 2
ATTN_WIDTH = N_Q_HEADS * HEAD_DIM
KV_WIDTH = N_KV_HEADS * HEAD_DIM
BLOCK = 128
ROT_DIM = HEAD_DIM // 4
ROPE_THETA = 500000.0
CONV_WIDTH = D_MODEL - ATTN_WIDTH
CONV_K = 3
N_KEYS = 128
PEER_HEADS = 8
PEER_QDIM = 256
PEER_HALF = PEER_QDIM // 2
PEER_TOPK = 16
DEPTH = 1
DN_ALPHA = (2.0 * DEPTH) ** 0.25
LN_EPS = 1e-5

LANES = 128
BF16_TILE_ROWS = 16
PEER_DOT_ROWS = 128
PEER_SCORE_COLS = 512
F32_TILE_ROWS = 8
BF16_ROWS = 8
BF16_COLS = 256
GATE_UNIT_KEYS = 64
GATE_PAD = 2 * LANES
VMEM_LIMIT = 56 * 1024 * 1024

_O_K = ATTN_WIDTH
_O_V = _O_K + KV_WIDTH
_O_GB = _O_V + KV_WIDTH
_O_GC = _O_GB + CONV_WIDTH
_O_XC = _O_GC + CONV_WIDTH
_O_END = _O_XC + CONV_WIDTH

BF16 = jnp.bfloat16
F32 = jnp.float32
NEG_INF = float("-inf")


def _dot(a, b):
    return jnp.dot(a, b, preferred_element_type=F32)


def _dot_nt(a, b):
    return lax.dot_general(a, b, (((1,), (1,)), ((), ())), preferred_element_type=F32)


def _layer_norm(z, g, b):
    mu = jnp.mean(z, axis=-1, keepdims=True)
    zc = z - mu
    var = jnp.mean(zc * zc, axis=-1, keepdims=True)
    return zc * lax.rsqrt(var + LN_EPS) * g + b


def _front_kernel(sinks_ref, x_ref, rc_ref, rsa_ref, rsb_ref, w_in_ref, b_in_ref,
                  conv_w_ref, w_out_ref, b_out_ref, g1_ref, be1_ref, w_pq_ref,
                  k1_ref, k2_ref,
                  h1_ref, h1t_ref, s1_ref, s2_ref,
                  kbuf, vbuf, ubuf, kvar, vvar, ybuf, *, tm, tiles_per_seq):
    step = pl.program_id(0)
    nblk = tm // BLOCK

    @pl.when(step % tiles_per_seq == 0)
    def _():
        kbuf[0:BLOCK, :] = jnp.zeros((BLOCK, KV_WIDTH), F32)
        vbuf[0:BLOCK, :] = jnp.zeros((BLOCK, KV_WIDTH), F32)
        ubuf[0:8, :] = jnp.zeros((8, CONV_WIDTH), F32)

    x = x_ref[...]
    xb = x.astype(BF16)

    def proj(lo, hi):
        return _dot(xb, w_in_ref[:, lo:hi]) + b_in_ref[:, lo:hi]

    rc = rc_ref[...]
    rsa = rsa_ref[...]
    rsb = rsb_ref[...]

    def rope(t):
        return (t * rc + pltpu.roll(t, LANES - ROT_DIM // 2, 1) * rsa
                + pltpu.roll(t, ROT_DIM // 2, 1) * rsb)

    kbuf[BLOCK:BLOCK + tm, :] = rope(proj(_O_K, _O_V))
    vbuf[BLOCK:BLOCK + tm, :] = proj(_O_V, _O_GB)

    lane = lax.broadcasted_iota(jnp.int32, (tm + BLOCK, LANES), 1)
    lo = lane < HEAD_DIM
    for buf, var in ((kbuf, kvar), (vbuf, vvar)):
        t = buf[...]
        tr = pltpu.roll(t, HEAD_DIM, 1)
        var[0] = jnp.where(lo, t, 0.0).astype(BF16)
        var[1] = jnp.where(lo, 0.0, tr).astype(BF16)
        var[2] = jnp.where(lo, tr, 0.0).astype(BF16)
        var[3] = jnp.where(lo, 0.0, t).astype(BF16)

    qi = lax.broadcasted_iota(jnp.int32, (BLOCK, 2 * BLOCK), 0)
    kj = lax.broadcasted_iota(jnp.int32, (BLOCK, 2 * BLOCK), 1)
    diff = qi + BLOCK - kj
    band = (diff >= 0) & (diff < BLOCK)

    for p in range(N_Q_HEADS // 2):
        q_slab = (rope(proj(p * LANES, (p + 1) * LANES)) * (HEAD_DIM ** -0.5)).astype(BF16)
        for blk in range(nblk):
            first = (step * nblk + blk) % (tiles_per_seq * nblk) == 0
            kmin = jnp.where(first, BLOCK, 0)
            mask = band & (kj >= kmin)
            qs = q_slab[blk * BLOCK:(blk + 1) * BLOCK, :]
            o = jnp.zeros((BLOCK, LANES), F32)
            for hh in range(2):
                h = 2 * p + hh
                var = (h // (N_Q_HEADS // N_KV_HEADS)) * 2 + hh
                kc = kvar[var, blk * BLOCK:(blk + 2) * BLOCK, :]
                vc = vvar[var, blk * BLOCK:(blk + 2) * BLOCK, :]
                s = jnp.where(mask, _dot_nt(qs, kc), NEG_INF)
                sink = sinks_ref[h]
                m = jnp.maximum(jnp.max(s, axis=-1, keepdims=True), sink)
                pe = jnp.exp(s - m)
                den = jnp.sum(pe, axis=-1, keepdims=True) + jnp.exp(sink - m)
                o = o + _dot((pe / den).astype(BF16), vc)
            ybuf[blk * BLOCK:(blk + 1) * BLOCK, p * LANES:(p + 1) * LANES] = o.astype(BF16)

    u = proj(_O_GC, _O_XC) * proj(_O_XC, _O_END)
    ubuf[8:8 + tm, :] = u
    um1 = ubuf[7:7 + tm, :]
    um2 = ubuf[6:6 + tm, :]
    cw = conv_w_ref[...]
    yc = proj(_O_GB, _O_GC) * (cw[0:1, :] * um2 + cw[1:2, :] * um1 + cw[2:3, :] * u)
    ybuf[:, ATTN_WIDTH:] = yc.astype(BF16)

    ubuf[0:8, :] = ubuf[tm:tm + 8, :]
    kbuf[0:BLOCK, :] = kbuf[tm:tm + BLOCK, :]
    vbuf[0:BLOCK, :] = vbuf[tm:tm + BLOCK, :]

    mix = _dot(ybuf[...], w_out_ref[...]) + b_out_ref[...]
    h1 = _layer_norm(DN_ALPHA * x + mix, g1_ref[...], be1_ref[...])
    h1_ref[...] = h1
    h1t_ref[...] = h1.T.astype(BF16)

    qp = _dot(h1.astype(BF16), w_pq_ref[...]).astype(BF16)
    k1 = k1_ref[...]
    k2 = k2_ref[...]
    for h in range(PEER_HEADS):
        base = h * PEER_QDIM
        s1_ref[h] = _dot_nt(k1, qp[:, base:base + PEER_HALF])
        s2_ref[h] = _dot_nt(k2, qp[:, base + PEER_HALF:base + PEER_QDIM])


def _front(xf, rc, rsa, rsb, sinks, w_in, b_in, conv_w, w_out, b_out, g1, be1, w_pq, k1, k2,
           *, seq, tm):
    T = xf.shape[0]
    proj_w = w_in.shape[1]
    tiles_per_seq = seq // tm
    full = lambda shape: pl.BlockSpec(shape, lambda i: (0,) * len(shape))
    tok = lambda w: pl.BlockSpec((tm, w), lambda i: (i, 0))
    kern = functools.partial(_front_kernel, tm=tm, tiles_per_seq=tiles_per_seq)
    return pl.pallas_call(
        kern,
        grid=(T // tm,),
        in_specs=[
            pl.BlockSpec(memory_space=pltpu.SMEM),
            tok(D_MODEL), tok(LANES), tok(LANES), tok(LANES),
            full((D_MODEL, proj_w)), full((1, proj_w)),
            full((CONV_K, CONV_WIDTH)), full((D_MODEL, D_MODEL)), full((1, D_MODEL)),
            full((1, D_MODEL)), full((1, D_MODEL)),
            full((D_MODEL, PEER_HEADS * PEER_QDIM)),
            full((N_KEYS, PEER_HALF)), full((N_KEYS, PEER_HALF)),
        ],
        out_specs=[
            tok(D_MODEL),
            pl.BlockSpec((D_MODEL, tm), lambda i: (0, i)),
            pl.BlockSpec((PEER_HEADS, N_KEYS, tm), lambda i: (0, 0, i)),
            pl.BlockSpec((PEER_HEADS, N_KEYS, tm), lambda i: (0, 0, i)),
        ],
        out_shape=[
            jax.ShapeDtypeStruct((T, D_MODEL), F32),
            jax.ShapeDtypeStruct((D_MODEL, T), BF16),
            jax.ShapeDtypeStruct((PEER_HEADS, N_KEYS, T), F32),
            jax.ShapeDtypeStruct((PEER_HEADS, N_KEYS, T), F32),
        ],
        scratch_shapes=[
            pltpu.VMEM((tm + BLOCK, KV_WIDTH), F32),
            pltpu.VMEM((tm + BLOCK, KV_WIDTH), F32),
            pltpu.VMEM((tm + 8, CONV_WIDTH), F32),
            pltpu.VMEM((4, tm + BLOCK, KV_WIDTH), BF16),
            pltpu.VMEM((4, tm + BLOCK, KV_WIDTH), BF16),
            pltpu.VMEM((tm, D_MODEL), BF16),
        ],
        compiler_params=pltpu.CompilerParams(
            dimension_semantics=("arbitrary",), vmem_limit_bytes=VMEM_LIMIT),
        name="front",
    )(sinks, xf, rc, rsa, rsb, w_in, b_in, conv_w, w_out, b_out, g1, be1, w_pq, k1, k2)


def _top16(s):
    rows = lax.broadcasted_iota(jnp.int32, s.shape, 0)
    rem = s
    rank = jnp.full(s.shape, float(PEER_TOPK), F32)
    vals = []
    for k in range(PEER_TOPK):
        m = jnp.max(rem, axis=0, keepdims=True)
        idx = jnp.min(jnp.where(rem == m, rows, N_KEYS), axis=0, keepdims=True)
        sel = rows == idx
        rem = jnp.where(sel, NEG_INF, rem)
        rank = jnp.where(sel, float(k), rank)
        vals.append(m)
    return vals, rank


_N_CAND_ROWS = PEER_TOPK + 7 * 8 + 8


def _cand_positions():
    import numpy as np
    pos = np.full((_N_CAND_ROWS,), -1, np.int32)
    for r in range(_N_CAND_ROWS):
        if r < 16:
            a, b = 0, r
        elif r < 72:
            a, b = 1 + (r - 16) // 8, (r - 16) % 8
        else:
            a, b = 8 + (r - 72), 0
        if (a + 1) * (b + 1) <= PEER_TOPK:
            pos[r] = a * PEER_TOPK + b
    return pos


def _select_exact(s1, s2, pos, valid, row16):
    v1, rank1 = _top16(s1)
    v2, rank2 = _top16(s2)

    v1m = jnp.zeros((PEER_TOPK, LANES), F32)
    v2m = jnp.zeros((PEER_TOPK, LANES), F32)
    for a in range(PEER_TOPK):
        v1m = jnp.where(row16 == a, v1[a], v1m)
        v2m = jnp.where(row16 == a, v2[a], v2m)
    blocks = [v1[0] + v2m]
    for a in range(1, 8):
        blocks.append(v1[a] + v2m[0:8, :])
    blocks.append(v1m[8:16, :] + v2[0])
    cand = jnp.where(valid, jnp.concatenate(blocks, axis=0), NEG_INF)

    rem = cand
    chosen = jnp.zeros(cand.shape, F32)
    for _ in range(PEER_TOPK):
        m = jnp.max(rem, axis=0, keepdims=True)
        idx = jnp.min(jnp.where(rem == m, pos, 1 << 20), axis=0, keepdims=True)
        sel = (pos == idx) & valid
        rem = jnp.where(sel, NEG_INF, rem)
        chosen = jnp.where(sel, 1.0, chosen)

    top = v1[0] + v2[0]
    z = jnp.sum(chosen * jnp.exp(jnp.where(valid, cand, top) - top), axis=0, keepdims=True)

    counts = [jnp.sum(chosen[0:16, :], axis=0, keepdims=True)]
    for a in range(1, 8):
        counts.append(jnp.sum(chosen[8 + 8 * a:16 + 8 * a, :], axis=0, keepdims=True))
    for a in range(8, 16):
        counts.append(chosen[64 + a:65 + a, :])
    n1 = jnp.zeros(s1.shape, F32)
    for a in range(PEER_TOPK):
        n1 = jnp.where(rank1 == float(a), counts[a], n1)
    return n1, jnp.exp(s1 - v1[0]), rank2, jnp.exp(s2 - v2[0]) / z


def _sort_pairs(n):
    pairs = []
    p = 1
    while p < n:
        k = p
        while k >= 1:
            for j in range(k % p, n - k, 2 * k):
                for i in range(min(k, n - j - k)):
                    if (i + j) // (2 * p) == (i + j + k) // (2 * p):
                        pairs.append((i + j, i + j + k))
            k //= 2
        p *= 2
    return pairs


def _hi_lo(a, b):
    if a is None:
        return b, None
    if b is None:
        return a, None
    return jnp.maximum(a, b), jnp.minimum(a, b)


def _sublane_all(x, op):
    for shift in (4, 2, 1):
        x = op(x, pltpu.roll(x, shift, 0))
    return x


def _top16_sorted(x):
    x = list(x)
    for i, j in _sort_pairs(len(x)):
        x[i], x[j] = _hi_lo(x[i], x[j])
    for shift in (4, 2, 1):
        y = [None if t is None else pltpu.roll(t, shift, 0) for t in x]
        c = [_hi_lo(x[r], y[PEER_TOPK - 1 - r])[0] for r in range(PEER_TOPK)]
        d = PEER_TOPK // 2
        while d >= 1:
            for i in range(PEER_TOPK):
                if not i & d:
                    c[i], c[i + d] = _hi_lo(c[i], c[i + d])
            d //= 2
        x = c
    return x


def _select_fast(s1, s2):
    nv = N_KEYS // 8
    t1 = [s1[8 * v:8 * v + 8, :] for v in range(nv)]
    t2 = [s2[8 * v:8 * v + 8, :] for v in range(nv)]
    v1 = _top16_sorted(t1)
    v2 = _top16_sorted(t2)
    sub = lax.broadcasted_iota(jnp.int32, (8, LANES), 0)
    one = jnp.ones((8, LANES), F32)
    zero = jnp.zeros((8, LANES), F32)

    def spread(vals):
        out = vals[0]
        for r in range(1, 8):
            out = jnp.where(sub == r, vals[r], out)
        return out

    v2lo, v2hi, v1hi = spread(v2[0:8]), spread(v2[8:16]), spread(v1[8:16])
    cand = [v1[0] + v2lo, v1[0] + v2hi]
    for a in range(1, 8):
        cand.append(jnp.where(sub < PEER_TOPK // (a + 1), v1[a] + v2lo, NEG_INF))
    cand.append(v1hi + v2[0])
    best = _top16_sorted(cand + [None] * (PEER_TOPK - len(cand)))
    tau = best[PEER_TOPK - 1]
    z = one
    for k in range(1, PEER_TOPK):
        z = z + jnp.exp(best[k] - best[0])

    count = lambda t: _sublane_all(jnp.where(t >= tau, one, zero), jnp.add)
    counts = [count(cand[0]) + count(cand[1])]
    counts += [count(cand[a + 1]) for a in range(1, 8)]
    counts += [jnp.where(v1[a] + v2[0] >= tau, one, zero) for a in range(8, PEER_TOPK)]
    total = counts[0]
    for a in range(1, PEER_TOPK):
        total = total + counts[a]
    tie = jnp.where(total != float(PEER_TOPK), one, zero)

    n1 = [zero] * nv
    rank2 = [jnp.full((8, LANES), float(PEER_TOPK), F32)] * nv
    for a in reversed(range(PEER_TOPK)):
        n1 = [jnp.where(t >= v1[a], counts[a], n) for t, n in zip(t1, n1)]
        rank2 = [jnp.where(t >= v2[a], float(a), r) for t, r in zip(t2, rank2)]
    for v, tiles in ((v1, t1), (v2, t2)):
        inside = zero
        for t in tiles:
            inside = inside + jnp.where(t >= v[PEER_TOPK - 1], one, zero)
        tie = jnp.where(_sublane_all(inside, jnp.add) != float(PEER_TOPK), one, tie)
        for a in range(PEER_TOPK - 1):
            tie = jnp.where(v[a] == v[a + 1], one, tie)

    inv_z = 1.0 / z
    cat = lambda tiles: jnp.concatenate(tiles, axis=0)
    e1 = cat([jnp.exp(t - v1[0]) for t in t1])
    e2 = cat([jnp.exp(t - v2[0]) * inv_z for t in t2])
    return cat(n1), e1, cat(rank2), e2, tie


def _select_kernel(pos_ref, s1_ref, s2_ref, r2_ref, e2_ref, n1_ref, e1_ref, *, ts):
    def group(g, carry):
        sl = pl.ds(pl.multiple_of(g * LANES, LANES), LANES)
        s1 = s1_ref[0, :, sl]
        s2 = s2_ref[0, :, sl]
        n1, e1, rank2, e2, tie = _select_fast(s1, s2)

        def exact():
            pos = pos_ref[...]
            row16 = lax.broadcasted_iota(jnp.int32, (PEER_TOPK, LANES), 0)
            return _select_exact(s1, s2, pos, pos >= 0, row16)

        n1, e1, rank2, e2 = lax.cond(jnp.max(tie) > 0.0, exact, lambda: (n1, e1, rank2, e2))
        n1_ref[:, 0, :, sl] = n1.reshape(N_KEYS // 8, 8, LANES)
        e1_ref[:, 0, :, sl] = e1.reshape(N_KEYS // 8, 8, LANES)
        r2_ref[0, :, sl] = rank2.astype(BF16)
        e2_ref[0, :, sl] = e2.astype(BF16)
        return carry

    lax.fori_loop(0, ts // LANES, group, 0)


def _select(s1t, s2t, *, ts):
    H, K, T = s1t.shape
    import numpy as np
    pos = jnp.asarray(np.tile(_cand_positions()[:, None], (1, LANES)))
    blk = pl.BlockSpec((1, K, ts), lambda t, h: (h, 0, t))
    blk2 = blk
    blk1 = pl.BlockSpec((K // 8, 1, 8, ts), lambda t, h: (0, h, 0, t))
    return pl.pallas_call(
        functools.partial(_select_kernel, ts=ts),
        grid=(T // ts, H),
        in_specs=[pl.BlockSpec((_N_CAND_ROWS, LANES), lambda t, h: (0, 0)), blk, blk],
        out_specs=[blk2, blk2, blk1, blk1],
        out_shape=[
            jax.ShapeDtypeStruct((H, K, T), BF16),
            jax.ShapeDtypeStruct((H, K, T), BF16),
            jax.ShapeDtypeStruct((K // 8, H, 8, T), F32),
            jax.ShapeDtypeStruct((K // 8, H, 8, T), F32),
        ],
        compiler_params=pltpu.CompilerParams(
            dimension_semantics=("arbitrary", "arbitrary"), vmem_limit_bytes=VMEM_LIMIT),
        name="select",
    )(pos, s1t, s2t)


def _peer_kernel(u_ref, vt_ref, xt_ref, r2_in, e2_in, n1_in, e1_in, h1_ref, g2_ref, be2_ref,
                 out_ref, acc_ref, a0_ref, a1_ref, w0_ref, w1_ref,
                 r2_ref, e2_ref, n1_ref, e1_ref, *, ec, tt, nchunks):
    s = pl.program_id(1)
    nkb = ec // N_KEYS
    assert nkb == 8
    rows = BF16_TILE_ROWS

    @pl.when(s == 0)
    def _():
        acc_ref[...] = jnp.zeros(acc_ref.shape, F32)
        for ref in (a0_ref, a1_ref, w0_ref, w1_ref):
            ref[...] = jnp.zeros(ref.shape, ref.dtype)
        r2_ref[:, :, 0:tt] = r2_in[...].astype(F32)
        e2_ref[:, :, LANES:LANES + tt] = e2_in[...].astype(F32)
        n1_ref[...] = n1_in[...]
        e1_ref[...] = e1_in[...]

    cg = jnp.clip(s - 1, 0, nchunks - 1)

    def stages(a_new, a_prev, w_new, w_old):
        rows8 = F32_TILE_ROWS
        nrg = GATE_UNIT_KEYS // rows8

        def row_tile(ref, h, ib, sl):
            return jnp.broadcast_to(ref[cg, h, ib:ib + 1, sl], (rows8, LANES))

        def gate_unit(kp, lt, part):
            lsl = slice(lt * LANES, (lt + 1) * LANES)
            keys = (2 * kp, 2 * kp + 1)
            g = [[None] * nrg for _ in keys]
            for h in range(PEER_HEADS):
                n1b = [row_tile(n1_ref, h, ib, lsl) for ib in keys]
                e1b = [row_tile(e1_ref, h, ib, lsl) for ib in keys]
                for rg in range(nrg):
                    j0 = part * GATE_UNIT_KEYS + rg * rows8
                    r2 = r2_ref[h, j0:j0 + rows8, lsl]
                    e2 = e2_ref[h, j0:j0 + rows8, (lt + 1) * LANES:(lt + 2) * LANES]
                    for k in range(2):
                        term = jnp.where(r2 < n1b[k], e2, 0.0) * e1b[k]
                        g[k][rg] = term if h == 0 else g[k][rg] + term
            for k, ib in enumerate(keys):
                for rp in range(nrg // 2):
                    e0 = ib * N_KEYS + part * GATE_UNIT_KEYS + rp * BF16_TILE_ROWS
                    esl = slice(e0, e0 + BF16_TILE_ROWS)
                    ab = a_prev[esl, lsl]
                    ge = 0.5 * ab * (1.0 + lax.erf(ab * (2.0 ** -0.5)))
                    gg = jnp.concatenate([g[k][2 * rp], g[k][2 * rp + 1]], axis=0)
                    w_new[esl, lsl] = (ge * gg).astype(BF16)

        def out_piece(p):
            psl = slice(p * PEER_DOT_ROWS, (p + 1) * PEER_DOT_ROWS)
            acc_ref[psl, :] += _dot(vt_ref[psl, :], w_old[...])

        def score_piece(p):
            psl = slice(p * PEER_DOT_ROWS, (p + 1) * PEER_DOT_ROWS)
            a_new[psl, :] = _dot(u_ref[psl, :], xt_ref[...])

        units = [(kp, lt, part) for kp in range(nkb // 2) for lt in range(tt // LANES)
                 for part in range(N_KEYS // GATE_UNIT_KEYS)]
        npieces = ec // PEER_DOT_ROWS
        pieces = [f for p in range(npieces) for f in ((out_piece, p), (score_piece, p))]
        per = len(units) // len(pieces)
        assert per * len(pieces) == len(units)
        for n, (fn, p) in enumerate(pieces):
            for kp, lt, part in units[n * per:(n + 1) * per]:
                gate_unit(kp, lt, part)
            fn(p)


    @pl.when(s % 2 == 0)
    def _():
        stages(a0_ref, a1_ref, w1_ref, w0_ref)

    @pl.when(s % 2 == 1)
    def _():
        stages(a1_ref, a0_ref, w0_ref, w1_ref)

    @pl.when(s == nchunks + 1)
    def _():
        z = DN_ALPHA * h1_ref[...] + acc_ref[...].T
        out_ref[...] = _layer_norm(z, g2_ref[...], be2_ref[...])


def _peer(u_bf, vt_bf, h1t, r2, e2, n1, e1, h1, g2, be2, *, ec, tt):
    E = u_bf.shape[0]
    T = h1.shape[0]
    nchunks = E // ec
    last = nchunks - 1
    tokrows = pl.BlockSpec((PEER_HEADS, N_KEYS, tt), lambda t, s: (0, 0, t))
    tokkeys = pl.BlockSpec((N_KEYS // 8, PEER_HEADS, 8, tt), lambda t, s: (0, 0, 0, t))
    vec = pl.BlockSpec((1, D_MODEL), lambda t, s: (0, 0))
    return pl.pallas_call(
        functools.partial(_peer_kernel, ec=ec, tt=tt, nchunks=nchunks),
        grid=(T // tt, nchunks + 2),
        in_specs=[
            pl.BlockSpec((ec, D_MODEL), lambda t, s: (jnp.minimum(s, last), 0)),
            pl.BlockSpec((D_MODEL, ec), lambda t, s: (0, jnp.clip(s - 2, 0, last))),
            pl.BlockSpec((D_MODEL, tt), lambda t, s: (0, t)),
            tokrows, tokrows, tokkeys, tokkeys,
            pl.BlockSpec((tt, D_MODEL), lambda t, s: (t, 0)),
            vec, vec,
        ],
        out_specs=pl.BlockSpec((tt, D_MODEL), lambda t, s: (t, 0)),
        out_shape=jax.ShapeDtypeStruct((T, D_MODEL), F32),
        scratch_shapes=[
            pltpu.VMEM((D_MODEL, tt), F32),
            pltpu.VMEM((ec, tt), F32),
            pltpu.VMEM((ec, tt), F32),
            pltpu.VMEM((ec, tt), BF16),
            pltpu.VMEM((ec, tt), BF16),
            pltpu.VMEM((PEER_HEADS, N_KEYS, tt + GATE_PAD), F32),
            pltpu.VMEM((PEER_HEADS, N_KEYS, tt + GATE_PAD), F32),
            pltpu.VMEM((N_KEYS // 8, PEER_HEADS, 8, tt), F32),
            pltpu.VMEM((N_KEYS // 8, PEER_HEADS, 8, tt), F32),
        ],
        compiler_params=pltpu.CompilerParams(
            dimension_semantics=("arbitrary", "arbitrary"), vmem_limit_bytes=VMEM_LIMIT),
        name="peer",
    )(u_bf, vt_bf, h1t, r2, e2, n1, e1, h1, g2, be2)


def _peer_dense_kernel(u_ref, vt_ref, xt_ref, r2_in, e2_in, n1_in, e1_in, h1_ref, g2_ref, be2_ref,
                       out_ref, acc_ref, a_ref, w_ref, r2_ref, e2_ref, *, ec, tt):
    c = pl.program_id(1)
    nkb = ec // N_KEYS
    assert nkb == 8
    rows = F32_TILE_ROWS
    nrg = GATE_UNIT_KEYS // rows

    @pl.when(c == 0)
    def _():
        acc_ref[...] = jnp.zeros(acc_ref.shape, F32)
        r2_ref[:, :, 0:tt] = r2_in[...].astype(F32)
        e2_ref[:, :, LANES:LANES + tt] = e2_in[...].astype(F32)

    ta = a_ref.shape[1]

    def row_tile(ref, h, ib, sl):
        return jnp.broadcast_to(ref[c, h, ib:ib + 1, sl], (rows, LANES))

    def gate_unit(kp, lt, part):
        lsl = slice(lt * LANES, (lt + 1) * LANES)
        keys = (2 * kp, 2 * kp + 1)
        g = [[None] * nrg for _ in keys]
        for h in range(PEER_HEADS):
            n1b = [row_tile(n1_in, h, ib, lsl) for ib in keys]
            e1b = [row_tile(e1_in, h, ib, lsl) for ib in keys]
            for rg in range(nrg):
                j0 = part * GATE_UNIT_KEYS + rg * rows
                r2 = r2_ref[h, j0:j0 + rows, lsl]
                e2 = e2_ref[h, j0:j0 + rows, (lt + 1) * LANES:(lt + 2) * LANES]
                for k in range(2):
                    term = jnp.where(r2 < n1b[k], e2, 0.0) * e1b[k]
                    g[k][rg] = term if h == 0 else g[k][rg] + term
        for k, ib in enumerate(keys):
            for rp in range(nrg // 2):
                e0 = ib * N_KEYS + part * GATE_UNIT_KEYS + rp * BF16_TILE_ROWS
                esl = slice(e0, e0 + BF16_TILE_ROWS)
                ab = a_ref[esl, (lt * LANES) % ta:(lt * LANES) % ta + LANES]
                ge = 0.5 * ab * (1.0 + lax.erf(ab * (2.0 ** -0.5)))
                gg = jnp.concatenate([g[k][2 * rp], g[k][2 * rp + 1]], axis=0)
                w_ref[esl, lsl] = (ge * gg).astype(BF16)

    for nt in range(tt // ta):
        a_ref[...] = _dot(u_ref[...], xt_ref[:, nt * ta:(nt + 1) * ta])
        for lt in range(nt * ta // LANES, (nt + 1) * ta // LANES):
            for kp in range(nkb // 2):
                for part in range(N_KEYS // GATE_UNIT_KEYS):
                    gate_unit(kp, lt, part)

    acc_ref[...] += _dot(vt_ref[...], w_ref[...])

    @pl.when(c == pl.num_programs(1) - 1)
    def _():
        z = DN_ALPHA * h1_ref[...] + acc_ref[...].T
        out_ref[...] = _layer_norm(z, g2_ref[...], be2_ref[...])


def _peer_dense(u_bf, vt_bf, h1t, r2, e2, n1, e1, h1, g2, be2, *, ec, tt):
    E = u_bf.shape[0]
    T = h1.shape[0]
    tokrows = pl.BlockSpec((PEER_HEADS, N_KEYS, tt), lambda t, c: (0, 0, t))
    tokkeys = pl.BlockSpec((N_KEYS // 8, PEER_HEADS, 8, tt), lambda t, c: (0, 0, 0, t))
    vec = pl.BlockSpec((1, D_MODEL), lambda t, c: (0, 0))
    return pl.pallas_call(
        functools.partial(_peer_dense_kernel, ec=ec, tt=tt),
        grid=(T // tt, E // ec),
        in_specs=[
            pl.BlockSpec((ec, D_MODEL), lambda t, c: (c, 0)),
            pl.BlockSpec((D_MODEL, ec), lambda t, c: (0, c)),
            pl.BlockSpec((D_MODEL, tt), lambda t, c: (0, t)),
            tokrows, tokrows, tokkeys, tokkeys,
            pl.BlockSpec((tt, D_MODEL), lambda t, c: (t, 0)),
            vec, vec,
        ],
        out_specs=pl.BlockSpec((tt, D_MODEL), lambda t, c: (t, 0)),
        out_shape=jax.ShapeDtypeStruct((T, D_MODEL), F32),
        scratch_shapes=[
            pltpu.VMEM((D_MODEL, tt), F32),
            pltpu.VMEM((ec, min(tt, PEER_SCORE_COLS)), F32),
            pltpu.VMEM((ec, tt), BF16),
            pltpu.VMEM((PEER_HEADS, N_KEYS, tt + GATE_PAD), F32),
            pltpu.VMEM((PEER_HEADS, N_KEYS, tt + GATE_PAD), F32),
        ],
        compiler_params=pltpu.CompilerParams(
            dimension_semantics=("arbitrary", "arbitrary"), vmem_limit_bytes=VMEM_LIMIT),
        name="peer",
    )(u_bf, vt_bf, h1t, r2, e2, n1, e1, h1, g2, be2)


def _peer_slab_kernel(u_ref, vt_ref, xt_ref, r2_ref, e2_ref, n1_ref, e1_ref, h1_ref, g2_ref, be2_ref,
                      out_ref, acc_ref, w_ref, *, ec, tt):
    c = pl.program_id(1)
    nkb = ec // N_KEYS
    assert nkb == 8

    @pl.when(c == 0)
    def _():
        acc_ref[...] = jnp.zeros(acc_ref.shape, F32)

    def row_bf16(ref, h, ib, sl):
        return jnp.broadcast_to(ref[c, h, ib:ib + 1, sl], (BF16_ROWS, BF16_COLS)).astype(BF16)

    rows = BF16_ROWS
    for nt in range(tt // BF16_COLS):
        nsl = slice(nt * BF16_COLS, (nt + 1) * BF16_COLS)
        a = _dot(u_ref[...], xt_ref[:, nsl])
        for ib in range(nkb):
            n1b = [row_bf16(n1_ref, h, ib, nsl) for h in range(PEER_HEADS)]
            e1b = [row_bf16(e1_ref, h, ib, nsl) for h in range(PEER_HEADS)]
            for rg in range(N_KEYS // rows):
                ksl = slice(rg * rows, (rg + 1) * rows)
                g = None
                for h in range(PEER_HEADS):
                    e2 = e2_ref[h, ksl, nsl]
                    term = jnp.where(r2_ref[h, ksl, nsl] < n1b[h], e2, jnp.zeros_like(e2)) * e1b[h]
                    g = term if g is None else g + term
                esl = slice(ib * N_KEYS + rg * rows, ib * N_KEYS + (rg + 1) * rows)
                ab = a[esl, :]
                ge = 0.5 * ab * (1.0 + lax.erf(ab * (2.0 ** -0.5)))
                w_ref[esl, nsl] = ge.astype(BF16) * g
        acc_ref[:, nsl] += _dot(vt_ref[...], w_ref[:, nsl])

    @pl.when(c == pl.num_programs(1) - 1)
    def _():
        z = DN_ALPHA * h1_ref[...] + acc_ref[...].T
        out_ref[...] = _layer_norm(z, g2_ref[...], be2_ref[...])


def _peer_slab(u_bf, vt_bf, h1t, r2, e2, n1, e1, h1, g2, be2, *, ec, tt):
    E = u_bf.shape[0]
    T = h1.shape[0]
    tokrows = pl.BlockSpec((PEER_HEADS, N_KEYS, tt), lambda t, c: (0, 0, t))
    tokkeys = pl.BlockSpec((N_KEYS // 8, PEER_HEADS, 8, tt), lambda t, c: (0, 0, 0, t))
    vec = pl.BlockSpec((1, D_MODEL), lambda t, c: (0, 0))
    return pl.pallas_call(
        functools.partial(_peer_slab_kernel, ec=ec, tt=tt),
        grid=(T // tt, E // ec),
        in_specs=[
            pl.BlockSpec((ec, D_MODEL), lambda t, c: (c, 0)),
            pl.BlockSpec((D_MODEL, ec), lambda t, c: (0, c)),
            pl.BlockSpec((D_MODEL, tt), lambda t, c: (0, t)),
            tokrows, tokrows, tokkeys, tokkeys,
            pl.BlockSpec((tt, D_MODEL), lambda t, c: (t, 0)),
            vec, vec,
        ],
        out_specs=pl.BlockSpec((tt, D_MODEL), lambda t, c: (t, 0)),
        out_shape=jax.ShapeDtypeStruct((T, D_MODEL), F32),
        scratch_shapes=[
            pltpu.VMEM((D_MODEL, tt), F32),
            pltpu.VMEM((ec, tt), BF16),
        ],
        compiler_params=pltpu.CompilerParams(
            dimension_semantics=("arbitrary", "arbitrary"), vmem_limit_bytes=VMEM_LIMIT),
        name="peer",
    )(u_bf, vt_bf, h1t, r2, e2, n1, e1, h1, g2, be2)


def _rope_tables(positions):
    T = positions.size
    half = ROT_DIM // 2
    inv_freq = ROPE_THETA ** (-jnp.arange(0, ROT_DIM, 2, dtype=F32) / ROT_DIM)
    ang = positions.reshape(T, 1).astype(F32) * inv_freq
    cos, sin = jnp.cos(ang), jnp.sin(ang)
    zeros = lambda w: jnp.zeros((T, w), F32)
    rc = jnp.concatenate([cos, cos, jnp.ones((T, HEAD_DIM - ROT_DIM), F32)], axis=1)
    rsa = jnp.concatenate([-sin, zeros(HEAD_DIM - half)], axis=1)
    rsb = jnp.concatenate([zeros(half), sin, zeros(HEAD_DIM - ROT_DIM)], axis=1)
    rep = LANES // HEAD_DIM
    return jnp.tile(rc, (1, rep)), jnp.tile(rsa, (1, rep)), jnp.tile(rsb, (1, rep))


def _tiles(seq, total_tokens):
    tm = next(c for c in (512, 256, BLOCK) if seq % c == 0)
    ts = next(c for c in (1024, 512, LANES) if total_tokens % c == 0)
    tt = 512 if total_tokens % 512 == 0 else 256
    ec = 8 * N_KEYS
    return tm, ts, tt, ec


def kernel(x, positions, w_in, b_in, attn_sinks, conv_w, w_out, b_out, ln1_g, ln1_b,
           w_pq, sub_keys1, sub_keys2, u_experts, v_experts, ln2_g, ln2_b):
    B, S, D = x.shape
    T = B * S
    assert D == D_MODEL and S % BLOCK == 0 and T % 256 == 0
    assert w_in.shape[0] == DEPTH
    tm, ts, tt, ec = _tiles(S, T)
    row = lambda v: v.reshape(1, -1).astype(F32)

    rc, rsa, rsb = _rope_tables(positions)
    h1, h1t, s1t, s2t = _front(
        x.reshape(T, D), rc, rsa, rsb, attn_sinks[0].astype(F32),
        w_in[0].astype(BF16), row(b_in[0]), conv_w[0].astype(F32),
        w_out[0].astype(BF16), row(b_out[0]), row(ln1_g[0]), row(ln1_b[0]),
        w_pq[0].astype(BF16), sub_keys1[0].astype(BF16), sub_keys2[0].astype(BF16),
        seq=S, tm=tm)
    r2, e2, n1, e1 = _select(s1t, s2t, ts=ts)
    out = _peer_slab(u_experts[0].astype(BF16), v_experts[0].T.astype(BF16), h1t,
                r2, e2, n1, e1, h1, row(ln2_g[0]), row(ln2_b[0]), ec=ec, tt=tt)
    return out.reshape(B, S, D)
```

```python
import functools

import jax
import jax.numpy as jnp
from jax import lax
from jax.experimental import pallas as pl
from jax.experimental.pallas import tpu as pltpu

D_MODEL = 1024
HEAD_DIM = 64
N_Q_HEADS = 8
N_KV_HEADS = 2
ATTN_WIDTH = N_Q_HEADS * HEAD_DIM
KV_WIDTH = N_KV_HEADS * HEAD_DIM
BLOCK = 128
ROT_DIM = HEAD_DIM // 4
ROPE_THETA = 500000.0
CONV_WIDTH = D_MODEL - ATTN_WIDTH
CONV_K = 3
N_KEYS = 128
PEER_HEADS = 8
PEER_QDIM = 256
PEER_HALF = PEER_QDIM // 2
PEER_TOPK = 16
DEPTH = 1
DN_ALPHA = (2.0 * DEPTH) ** 0.25
LN_EPS = 1e-5

LANES = 128
BF16_TILE_ROWS = 16
PEER_DOT_ROWS = 128
PEER_SCORE_COLS = 512
F32_TILE_ROWS = 8
BF16_ROWS = 8
BF16_COLS = 256
GATE_UNIT_KEYS = 64
GATE_PAD = 2 * LANES
VMEM_LIMIT = 56 * 1024 * 1024

_O_K = ATTN_WIDTH
_O_V = _O_K + KV_WIDTH
_O_GB = _O_V + KV_WIDTH
_O_GC = _O_GB + CONV_WIDTH
_O_XC = _O_GC + CONV_WIDTH
_O_END = _O_XC + CONV_WIDTH

BF16 = jnp.bfloat16
F32 = jnp.float32
NEG_INF = float("-inf")


def _dot(a, b):
    return jnp.dot(a, b, preferred_element_type=F32)


def _dot_nt(a, b):
    return lax.dot_general(a, b, (((1,), (1,)), ((), ())), preferred_element_type=F32)


def _layer_norm(z, g, b):
    mu = jnp.mean(z, axis=-1, keepdims=True)
    zc = z - mu
    var = jnp.mean(zc * zc, axis=-1, keepdims=True)
    return zc * lax.rsqrt(var + LN_EPS) * g + b


def _front_kernel(sinks_ref, x_ref, rc_ref, rsa_ref, rsb_ref, w_in_ref, b_in_ref,
                  conv_w_ref, w_out_ref, b_out_ref, g1_ref, be1_ref, w_pq_ref,
                  k1_ref, k2_ref,
                  h1_ref, h1t_ref, s1_ref, s2_ref,
                  kbuf, vbuf, ubuf, kvar, vvar, ybuf, *, tm, tiles_per_seq):
    step = pl.program_id(0)
    nblk = tm // BLOCK

    @pl.when(step % tiles_per_seq == 0)
    def _():
        kbuf[0:BLOCK, :] = jnp.zeros((BLOCK, KV_WIDTH), F32)
        vbuf[0:BLOCK, :] = jnp.zeros((BLOCK, KV_WIDTH), F32)
        ubuf[0:8, :] = jnp.zeros((8, CONV_WIDTH), F32)

    x = x_ref[...]
    xb = x.astype(BF16)

    def proj(lo, hi):
        return _dot(xb, w_in_ref[:, lo:hi]) + b_in_ref[:, lo:hi]

    rc = rc_ref[...]
    rsa = rsa_ref[...]
    rsb = rsb_ref[...]

    def rope(t):
        return (t * rc + pltpu.roll(t, LANES - ROT_DIM // 2, 1) * rsa
                + pltpu.roll(t, ROT_DIM // 2, 1) * rsb)

    kbuf[BLOCK:BLOCK + tm, :] = rope(proj(_O_K, _O_V))
    vbuf[BLOCK:BLOCK + tm, :] = proj(_O_V, _O_GB)

    lane = lax.broadcasted_iota(jnp.int32, (tm + BLOCK, LANES), 1)
    lo = lane < HEAD_DIM
    for buf, var in ((kbuf, kvar), (vbuf, vvar)):
        t = buf[...]
        tr = pltpu.roll(t, HEAD_DIM, 1)
        var[0] = jnp.where(lo, t, 0.0).astype(BF16)
        var[1] = jnp.where(lo, 0.0, tr).astype(BF16)
        var[2] = jnp.where(lo, tr, 0.0).astype(BF16)
        var[3] = jnp.where(lo, 0.0, t).astype(BF16)

    qi = lax.broadcasted_iota(jnp.int32, (BLOCK, 2 * BLOCK), 0)
    kj = lax.broadcasted_iota(jnp.int32, (BLOCK, 2 * BLOCK), 1)
    diff = qi + BLOCK - kj
    band = (diff >= 0) & (diff < BLOCK)

    for p in range(N_Q_HEADS // 2):
        q_slab = (rope(proj(p * LANES, (p + 1) * LANES)) * (HEAD_DIM ** -0.5)).astype(BF16)
        for blk in range(nblk):
            first = (step * nblk + blk) % (tiles_per_seq * nblk) == 0
            kmin = jnp.where(first, BLOCK, 0)
            mask = band & (kj >= kmin)
            qs = q_slab[blk * BLOCK:(blk + 1) * BLOCK, :]
            o = jnp.zeros((BLOCK, LANES), F32)
            for hh in range(2):
                h = 2 * p + hh
                var = (h // (N_Q_HEADS // N_KV_HEADS)) * 2 + hh
                kc = kvar[var, blk * BLOCK:(blk + 2) * BLOCK, :]
                vc = vvar[var, blk * BLOCK:(blk + 2) * BLOCK, :]
                s = jnp.where(mask, _dot_nt(qs, kc), NEG_INF)
                sink = sinks_ref[h]
                m = jnp.maximum(jnp.max(s, axis=-1, keepdims=True), sink)
                pe = jnp.exp(s - m)
                den = jnp.sum(pe, axis=-1, keepdims=True) + jnp.exp(sink - m)
                o = o + _dot((pe / den).astype(BF16), vc)
            ybuf[blk * BLOCK:(blk + 1) * BLOCK, p * LANES:(p + 1) * LANES] = o.astype(BF16)

    u = proj(_O_GC, _O_XC) * proj(_O_XC, _O_END)
    ubuf[8:8 + tm, :] = u
    um1 = ubuf[7:7 + tm, :]
    um2 = ubuf[6:6 + tm, :]
    cw = conv_w_ref[...]
    yc = proj(_O_GB, _O_GC) * (cw[0:1, :] * um2 + cw[1:2, :] * um1 + cw[2:3, :] * u)
    ybuf[:, ATTN_WIDTH:] = yc.astype(BF16)

    ubuf[0:8, :] = ubuf[tm:tm + 8, :]
    kbuf[0:BLOCK, :] = kbuf[tm:tm + BLOCK, :]
    vbuf[0:BLOCK, :] = vbuf[tm:tm + BLOCK, :]

    mix = _dot(ybuf[...], w_out_ref[...]) + b_out_ref[...]
    h1 = _layer_norm(DN_ALPHA * x + mix, g1_ref[...], be1_ref[...])
    h1_ref[...] = h1
    h1t_ref[...] = h1.T.astype(BF16)

    qp = _dot(h1.astype(BF16), w_pq_ref[...]).astype(BF16)
    k1 = k1_ref[...]
    k2 = k2_ref[...]
    for h in range(PEER_HEADS):
        base = h * PEER_QDIM
        s1_ref[h] = _dot_nt(k1, qp[:, base:base + PEER_HALF])
        s2_ref[h] = _dot_nt(k2, qp[:, base + PEER_HALF:base + PEER_QDIM])


def _front(xf, rc, rsa, rsb, sinks, w_in, b_in, conv_w, w_out, b_out, g1, be1, w_pq, k1, k2,
           *, seq, tm):
    T = xf.shape[0]
    proj_w = w_in.shape[1]
    tiles_per_seq = seq // tm
    full = lambda shape: pl.BlockSpec(shape, lambda i: (0,) * len(shape))
    tok = lambda w: pl.BlockSpec((tm, w), lambda i: (i, 0))
    kern = functools.partial(_front_kernel, tm=tm, tiles_per_seq=tiles_per_seq)
    return pl.pallas_call(
        kern,
        grid=(T // tm,),
        in_specs=[
            pl.BlockSpec(memory_space=pltpu.SMEM),
            tok(D_MODEL), tok(LANES), tok(LANES), tok(LANES),
            full((D_MODEL, proj_w)), full((1, proj_w)),
            full((CONV_K, CONV_WIDTH)), full((D_MODEL, D_MODEL)), full((1, D_MODEL)),
            full((1, D_MODEL)), full((1, D_MODEL)),
            full((D_MODEL, PEER_HEADS * PEER_QDIM)),
            full((N_KEYS, PEER_HALF)), full((N_KEYS, PEER_HALF)),
        ],
        out_specs=[
            tok(D_MODEL),
            pl.BlockSpec((D_MODEL, tm), lambda i: (0, i)),
            pl.BlockSpec((PEER_HEADS, N_KEYS, tm), lambda i: (0, 0, i)),
            pl.BlockSpec((PEER_HEADS, N_KEYS, tm), lambda i: (0, 0, i)),
        ],
        out_shape=[
            jax.ShapeDtypeStruct((T, D_MODEL), F32),
            jax.ShapeDtypeStruct((D_MODEL, T), BF16),
            jax.ShapeDtypeStruct((PEER_HEADS, N_KEYS, T), F32),
            jax.ShapeDtypeStruct((PEER_HEADS, N_KEYS, T), F32),
        ],
        scratch_shapes=[
            pltpu.VMEM((tm + BLOCK, KV_WIDTH), F32),
            pltpu.VMEM((tm + BLOCK, KV_WIDTH), F32),
            pltpu.VMEM((tm + 8, CONV_WIDTH), F32),
            pltpu.VMEM((4, tm + BLOCK, KV_WIDTH), BF16),
            pltpu.VMEM((4, tm + BLOCK, KV_WIDTH), BF16),
            pltpu.VMEM((tm, D_MODEL), BF16),
        ],
        compiler_params=pltpu.CompilerParams(
            dimension_semantics=("arbitrary",), vmem_limit_bytes=VMEM_LIMIT),
        name="front",
    )(sinks, xf, rc, rsa, rsb, w_in, b_in, conv_w, w_out, b_out, g1, be1, w_pq, k1, k2)


def _top16(s):
    rows = lax.broadcasted_iota(jnp.int32, s.shape, 0)
    rem = s
    rank = jnp.full(s.shape, float(PEER_TOPK), F32)
    vals = []
    for k in range(PEER_TOPK):
        m = jnp.max(rem, axis=0, keepdims=True)
        idx = jnp.min(jnp.where(rem == m, rows, N_KEYS), axis=0, keepdims=True)
        sel = rows == idx
        rem = jnp.where(sel, NEG_INF, rem)
        rank = jnp.where(sel, float(k), rank)
        vals.append(m)
    return vals, rank


_N_CAND_ROWS = PEER_TOPK + 7 * 8 + 8


def _cand_positions():
    import numpy as np
    pos = np.full((_N_CAND_ROWS,), -1, np.int32)
    for r in range(_N_CAND_ROWS):
        if r < 16:
            a, b = 0, r
        elif r < 72:
            a, b = 1 + (r - 16) // 8, (r - 16) % 8
        else:
            a, b = 8 + (r - 72), 0
        if (a + 1) * (b + 1) <= PEER_TOPK:
            pos[r] = a * PEER_TOPK + b
    return pos


def _select_exact(s1, s2, pos, valid, row16):
    v1, rank1 = _top16(s1)
    v2, rank2 = _top16(s2)

    v1m = jnp.zeros((PEER_TOPK, LANES), F32)
    v2m = jnp.zeros((PEER_TOPK, LANES), F32)
    for a in range(PEER_TOPK):
        v1m = jnp.where(row16 == a, v1[a], v1m)
        v2m = jnp.where(row16 == a, v2[a], v2m)
    blocks = [v1[0] + v2m]
    for a in range(1, 8):
        blocks.append(v1[a] + v2m[0:8, :])
    blocks.append(v1m[8:16, :] + v2[0])
    cand = jnp.where(valid, jnp.concatenate(blocks, axis=0), NEG_INF)

    rem = cand
    chosen = jnp.zeros(cand.shape, F32)
    for _ in range(PEER_TOPK):
        m = jnp.max(rem, axis=0, keepdims=True)
        idx = jnp.min(jnp.where(rem == m, pos, 1 << 20), axis=0, keepdims=True)
        sel = (pos == idx) & valid
        rem = jnp.where(sel, NEG_INF, rem)
        chosen = jnp.where(sel, 1.0, chosen)

    top = v1[0] + v2[0]
    z = jnp.sum(chosen * jnp.exp(jnp.where(valid, cand, top) - top), axis=0, keepdims=True)

    counts = [jnp.sum(chosen[0:16, :], axis=0, keepdims=True)]
    for a in range(1, 8):
        counts.append(jnp.sum(chosen[8 + 8 * a:16 + 8 * a, :], axis=0, keepdims=True))
    for a in range(8, 16):
        counts.append(chosen[64 + a:65 + a, :])
    n1 = jnp.zeros(s1.shape, F32)
    for a in range(PEER_TOPK):
        n1 = jnp.where(rank1 == float(a), counts[a], n1)
    return n1, jnp.exp(s1 - v1[0]), rank2, jnp.exp(s2 - v2[0]) / z


def _sort_pairs(n):
    pairs = []
    p = 1
    while p < n:
        k = p
        while k >= 1:
            for j in range(k % p, n - k, 2 * k):
                for i in range(min(k, n - j - k)):
                    if (i + j) // (2 * p) == (i + j + k) // (2 * p):
                        pairs.append((i + j, i + j + k))
            k //= 2
        p *= 2
    return pairs


def _hi_lo(a, b):
    if a is None:
        return b, None
    if b is None:
        return a, None
    return jnp.maximum(a, b), jnp.minimum(a, b)


def _sublane_all(x, op):
    for shift in (4, 2, 1):
        x = op(x, pltpu.roll(x, shift, 0))
    return x


def _top16_sorted(x):
    x = list(x)
    for i, j in _sort_pairs(len(x)):
        x[i], x[j] = _hi_lo(x[i], x[j])
    for shift in (4, 2, 1):
        y = [None if t is None else pltpu.roll(t, shift, 0) for t in x]
        c = [_hi_lo(x[r], y[PEER_TOPK - 1 - r])[0] for r in range(PEER_TOPK)]
        d = PEER_TOPK // 2
        while d >= 1:
            for i in range(PEER_TOPK):
                if not i & d:
                    c[i], c[i + d] = _hi_lo(c[i], c[i + d])
            d //= 2
        x = c
    return x


def _select_fast(s1, s2):
    nv = N_KEYS // 8
    t1 = [s1[8 * v:8 * v + 8, :] for v in range(nv)]
    t2 = [s2[8 * v:8 * v + 8, :] for v in range(nv)]
    v1 = _top16_sorted(t1)
    v2 = _top16_sorted(t2)
    sub = lax.broadcasted_iota(jnp.int32, (8, LANES), 0)
    one = jnp.ones((8, LANES), F32)
    zero = jnp.zeros((8, LANES), F32)

    def spread(vals):
        out = vals[0]
        for r in range(1, 8):
            out = jnp.where(sub == r, vals[r], out)
        return out

    v2lo, v2hi, v1hi = spread(v2[0:8]), spread(v2[8:16]), spread(v1[8:16])
    cand = [v1[0] + v2lo, v1[0] + v2hi]
    for a in range(1, 8):
        cand.append(jnp.where(sub < PEER_TOPK // (a + 1), v1[a] + v2lo, NEG_INF))
    cand.append(v1hi + v2[0])
    best = _top16_sorted(cand + [None] * (PEER_TOPK - len(cand)))
    tau = best[PEER_TOPK - 1]
    z = one
    for k in range(1, PEER_TOPK):
        z = z + jnp.exp(best[k] - best[0])

    count = lambda t: _sublane_all(jnp.where(t >= tau, one, zero), jnp.add)
    counts = [count(cand[0]) + count(cand[1])]
    counts += [count(cand[a + 1]) for a in range(1, 8)]
    counts += [jnp.where(v1[a] + v2[0] >= tau, one, zero) for a in range(8, PEER_TOPK)]
    total = counts[0]
    for a in range(1, PEER_TOPK):
        total = total + counts[a]
    tie = jnp.where(total != float(PEER_TOPK), one, zero)

    n1 = [zero] * nv
    rank2 = [jnp.full((8, LANES), float(PEER_TOPK), F32)] * nv
    for a in reversed(range(PEER_TOPK)):
        n1 = [jnp.where(t >= v1[a], counts[a], n) for t, n in zip(t1, n1)]
        rank2 = [jnp.where(t >= v2[a], float(a), r) for t, r in zip(t2, rank2)]
    for v, tiles in ((v1, t1), (v2, t2)):
        inside = zero
        for t in tiles:
            inside = inside + jnp.where(t >= v[PEER_TOPK - 1], one, zero)
        tie = jnp.where(_sublane_all(inside, jnp.add) != float(PEER_TOPK), one, tie)
        for a in range(PEER_TOPK - 1):
            tie = jnp.where(v[a] == v[a + 1], one, tie)

    inv_z = 1.0 / z
    cat = lambda tiles: jnp.concatenate(tiles, axis=0)
    e1 = cat([jnp.exp(t - v1[0]) for t in t1])
    e2 = cat([jnp.exp(t - v2[0]) * inv_z for t in t2])
    return cat(n1), e1, cat(rank2), e2, tie


def _select_kernel(pos_ref, s1_ref, s2_ref, r2_ref, e2_ref, n1_ref, e1_ref, *, ts):
    def group(g, carry):
        sl = pl.ds(pl.multiple_of(g * LANES, LANES), LANES)
        s1 = s1_ref[0, :, sl]
        s2 = s2_ref[0, :, sl]
        n1, e1, rank2, e2, tie = _select_fast(s1, s2)

        def exact():
            pos = pos_ref[...]
            row16 = lax.broadcasted_iota(jnp.int32, (PEER_TOPK, LANES), 0)
            return _select_exact(s1, s2, pos, pos >= 0, row16)

        n1, e1, rank2, e2 = lax.cond(jnp.max(tie) > 0.0, exact, lambda: (n1, e1, rank2, e2))
        n1_ref[:, 0, :, sl] = n1.reshape(N_KEYS // 8, 8, LANES)
        e1_ref[:, 0, :, sl] = e1.reshape(N_KEYS // 8, 8, LANES)
        r2_ref[0, :, sl] = rank2.astype(BF16)
        e2_ref[0, :, sl] = e2.astype(BF16)
        return carry

    lax.fori_loop(0, ts // LANES, group, 0)


def _select(s1t, s2t, *, ts):
    H, K, T = s1t.shape
    import numpy as np
    pos = jnp.asarray(np.tile(_cand_positions()[:, None], (1, LANES)))
    blk = pl.BlockSpec((1, K, ts), lambda t, h: (h, 0, t))
    blk2 = blk
    blk1 = pl.BlockSpec((K // 8, 1, 8, ts), lambda t, h: (0, h, 0, t))
    return pl.pallas_call(
        functools.partial(_select_kernel, ts=ts),
        grid=(T // ts, H),
        in_specs=[pl.BlockSpec((_N_CAND_ROWS, LANES), lambda t, h: (0, 0)), blk, blk],
        out_specs=[blk2, blk2, blk1, blk1],
        out_shape=[
            jax.ShapeDtypeStruct((H, K, T), BF16),
            jax.ShapeDtypeStruct((H, K, T), BF16),
            jax.ShapeDtypeStruct((K // 8, H, 8, T), F32),
            jax.ShapeDtypeStruct((K // 8, H, 8, T), F32),
        ],
        compiler_params=pltpu.CompilerParams(
            dimension_semantics=("arbitrary", "arbitrary"), vmem_limit_bytes=VMEM_LIMIT),
        name="select",
    )(pos, s1t, s2t)


def _peer_kernel(u_ref, vt_ref, xt_ref, r2_in, e2_in, n1_in, e1_in, h1_ref, g2_ref, be2_ref,
                 out_ref, acc_ref, a0_ref, a1_ref, w0_ref, w1_ref,
                 r2_ref, e2_ref, n1_ref, e1_ref, *, ec, tt, nchunks):
    s = pl.program_id(1)
    nkb = ec // N_KEYS
    assert nkb == 8
    rows = BF16_TILE_ROWS

    @pl.when(s == 0)
    def _():
        acc_ref[...] = jnp.zeros(acc_ref.shape, F32)
        for ref in (a0_ref, a1_ref, w0_ref, w1_ref):
            ref[...] = jnp.zeros(ref.shape, ref.dtype)
        r2_ref[:, :, 0:tt] = r2_in[...].astype(F32)
        e2_ref[:, :, LANES:LANES + tt] = e2_in[...].astype(F32)
        n1_ref[...] = n1_in[...]
        e1_ref[...] = e1_in[...]

    cg = jnp.clip(s - 1, 0, nchunks - 1)

    def stages(a_new, a_prev, w_new, w_old):
        rows8 = F32_TILE_ROWS
        nrg = GATE_UNIT_KEYS // rows8

        def row_tile(ref, h, ib, sl):
            return jnp.broadcast_to(ref[cg, h, ib:ib + 1, sl], (rows8, LANES))

        def gate_unit(kp, lt, part):
            lsl = slice(lt * LANES, (lt + 1) * LANES)
            keys = (2 * kp, 2 * kp + 1)
            g = [[None] * nrg for _ in keys]
            for h in range(PEER_HEADS):
                n1b = [row_tile(n1_ref, h, ib, lsl) for ib in keys]
                e1b = [row_tile(e1_ref, h, ib, lsl) for ib in keys]
                for rg in range(nrg):
                    j0 = part * GATE_UNIT_KEYS + rg * rows8
                    r2 = r2_ref[h, j0:j0 + rows8, lsl]
                    e2 = e2_ref[h, j0:j0 + rows8, (lt + 1) * LANES:(lt + 2) * LANES]
                    for k in range(2):
                        term = jnp.where(r2 < n1b[k], e2, 0.0) * e1b[k]
                        g[k][rg] = term if h == 0 else g[k][rg] + term
            for k, ib in enumerate(keys):
                for rp in range(nrg // 2):
                    e0 = ib * N_KEYS + part * GATE_UNIT_KEYS + rp * BF16_TILE_ROWS
                    esl = slice(e0, e0 + BF16_TILE_ROWS)
                    ab = a_prev[esl, lsl]
                    ge = 0.5 * ab * (1.0 + lax.erf(ab * (2.0 ** -0.5)))
                    gg = jnp.concatenate([g[k][2 * rp], g[k][2 * rp + 1]], axis=0)
                    w_new[esl, lsl] = (ge * gg).astype(BF16)

        def out_piece(p):
            psl = slice(p * PEER_DOT_ROWS, (p + 1) * PEER_DOT_ROWS)
            acc_ref[psl, :] += _dot(vt_ref[psl, :], w_old[...])

        def score_piece(p):
            psl = slice(p * PEER_DOT_ROWS, (p + 1) * PEER_DOT_ROWS)
            a_new[psl, :] = _dot(u_ref[psl, :], xt_ref[...])

        units = [(kp, lt, part) for kp in range(nkb // 2) for lt in range(tt // LANES)
                 for part in range(N_KEYS // GATE_UNIT_KEYS)]
        npieces = ec // PEER_DOT_ROWS
        pieces = [f for p in range(npieces) for f in ((out_piece, p), (score_piece, p))]
        per = len(units) // len(pieces)
        assert per * len(pieces) == len(units)
        for n, (fn, p) in enumerate(pieces):
            for kp, lt, part in units[n * per:(n + 1) * per]:
                gate_unit(kp, lt, part)
            fn(p)


    @pl.when(s % 2 == 0)
    def _():
        stages(a0_ref, a1_ref, w1_ref, w0_ref)

    @pl.when(s % 2 == 1)
    def _():
        stages(a1_ref, a0_ref, w0_ref, w1_ref)

    @pl.when(s == nchunks + 1)
    def _():
        z = DN_ALPHA * h1_ref[...] + acc_ref[...].T
        out_ref[...] = _layer_norm(z, g2_ref[...], be2_ref[...])


def _peer(u_bf, vt_bf, h1t, r2, e2, n1, e1, h1, g2, be2, *, ec, tt):
    E = u_bf.shape[0]
    T = h1.shape[0]
    nchunks = E // ec
    last = nchunks - 1
    tokrows = pl.BlockSpec((PEER_HEADS, N_KEYS, tt), lambda t, s: (0, 0, t))
    tokkeys = pl.BlockSpec((N_KEYS // 8, PEER_HEADS, 8, tt), lambda t, s: (0, 0, 0, t))
    vec = pl.BlockSpec((1, D_MODEL), lambda t, s: (0, 0))
    return pl.pallas_call(
        functools.partial(_peer_kernel, ec=ec, tt=tt, nchunks=nchunks),
        grid=(T // tt, nchunks + 2),
        in_specs=[
            pl.BlockSpec((ec, D_MODEL), lambda t, s: (jnp.minimum(s, last), 0)),
            pl.BlockSpec((D_MODEL, ec), lambda t, s: (0, jnp.clip(s - 2, 0, last))),
            pl.BlockSpec((D_MODEL, tt), lambda t, s: (0, t)),
            tokrows, tokrows, tokkeys, tokkeys,
            pl.BlockSpec((tt, D_MODEL), lambda t, s: (t, 0)),
            vec, vec,
        ],
        out_specs=pl.BlockSpec((tt, D_MODEL), lambda t, s: (t, 0)),
        out_shape=jax.ShapeDtypeStruct((T, D_MODEL), F32),
        scratch_shapes=[
            pltpu.VMEM((D_MODEL, tt), F32),
            pltpu.VMEM((ec, tt), F32),
            pltpu.VMEM((ec, tt), F32),
            pltpu.VMEM((ec, tt), BF16),
            pltpu.VMEM((ec, tt), BF16),
            pltpu.VMEM((PEER_HEADS, N_KEYS, tt + GATE_PAD), F32),
            pltpu.VMEM((PEER_HEADS, N_KEYS, tt + GATE_PAD), F32),
            pltpu.VMEM((N_KEYS // 8, PEER_HEADS, 8, tt), F32),
            pltpu.VMEM((N_KEYS // 8, PEER_HEADS, 8, tt), F32),
        ],
        compiler_params=pltpu.CompilerParams(
            dimension_semantics=("arbitrary", "arbitrary"), vmem_limit_bytes=VMEM_LIMIT),
        name="peer",
    )(u_bf, vt_bf, h1t, r2, e2, n1, e1, h1, g2, be2)


def _peer_dense_kernel(u_ref, vt_ref, xt_ref, r2_in, e2_in, n1_in, e1_in, h1_ref, g2_ref, be2_ref,
                       out_ref, acc_ref, a_ref, w_ref, r2_ref, e2_ref, *, ec, tt):
    c = pl.program_id(1)
    nkb = ec // N_KEYS
    assert nkb == 8
    rows = F32_TILE_ROWS
    nrg = GATE_UNIT_KEYS // rows

    @pl.when(c == 0)
    def _():
        acc_ref[...] = jnp.zeros(acc_ref.shape, F32)
        r2_ref[:, :, 0:tt] = r2_in[...].astype(F32)
        e2_ref[:, :, LANES:LANES + tt] = e2_in[...].astype(F32)

    ta = a_ref.shape[1]

    def row_tile(ref, h, ib, sl):
        return jnp.broadcast_to(ref[c, h, ib:ib + 1, sl], (rows, LANES))

    def gate_unit(kp, lt, part):
        lsl = slice(lt * LANES, (lt + 1) * LANES)
        keys = (2 * kp, 2 * kp + 1)
        g = [[None] * nrg for _ in keys]
        for h in range(PEER_HEADS):
            n1b = [row_tile(n1_in, h, ib, lsl) for ib in keys]
            e1b = [row_tile(e1_in, h, ib, lsl) for ib in keys]
            for rg in range(nrg):
                j0 = part * GATE_UNIT_KEYS + rg * rows
                r2 = r2_ref[h, j0:j0 + rows, lsl]
                e2 = e2_ref[h, j0:j0 + rows, (lt + 1) * LANES:(lt + 2) * LANES]
                for k in range(2):
                    term = jnp.where(r2 < n1b[k], e2, 0.0) * e1b[k]
                    g[k][rg] = term if h == 0 else g[k][rg] + term
        for k, ib in enumerate(keys):
            for rp in range(nrg // 2):
                e0 = ib * N_KEYS + part * GATE_UNIT_KEYS + rp * BF16_TILE_ROWS
                esl = slice(e0, e0 + BF16_TILE_ROWS)
                ab = a_ref[esl, (lt * LANES) % ta:(lt * LANES) % ta + LANES]
                ge = 0.5 * ab * (1.0 + lax.erf(ab * (2.0 ** -0.5)))
                gg = jnp.concatenate([g[k][2 * rp], g[k][2 * rp + 1]], axis=0)
                w_ref[esl, lsl] = (ge * gg).astype(BF16)

    for nt in range(tt // ta):
        a_ref[...] = _dot(u_ref[...], xt_ref[:, nt * ta:(nt + 1) * ta])
        for lt in range(nt * ta // LANES, (nt + 1) * ta // LANES):
            for kp in range(nkb // 2):
                for part in range(N_KEYS // GATE_UNIT_KEYS):
                    gate_unit(kp, lt, part)

    acc_ref[...] += _dot(vt_ref[...], w_ref[...])

    @pl.when(c == pl.num_programs(1) - 1)
    def _():
        z = DN_ALPHA * h1_ref[...] + acc_ref[...].T
        out_ref[...] = _layer_norm(z, g2_ref[...], be2_ref[...])


def _peer_dense(u_bf, vt_bf, h1t, r2, e2, n1, e1, h1, g2, be2, *, ec, tt):
    E = u_bf.shape[0]
    T = h1.shape[0]
    tokrows = pl.BlockSpec((PEER_HEADS, N_KEYS, tt), lambda t, c: (0, 0, t))
    tokkeys = pl.BlockSpec((N_KEYS // 8, PEER_HEADS, 8, tt), lambda t, c: (0, 0, 0, t))
    vec = pl.BlockSpec((1, D_MODEL), lambda t, c: (0, 0))
    return pl.pallas_call(
        functools.partial(_peer_dense_kernel, ec=ec, tt=tt),
        grid=(T // tt, E // ec),
        in_specs=[
            pl.BlockSpec((ec, D_MODEL), lambda t, c: (c, 0)),
            pl.BlockSpec((D_MODEL, ec), lambda t, c: (0, c)),
            pl.BlockSpec((D_MODEL, tt), lambda t, c: (0, t)),
            tokrows, tokrows, tokkeys, tokkeys,
            pl.BlockSpec((tt, D_MODEL), lambda t, c: (t, 0)),
            vec, vec,
        ],
        out_specs=pl.BlockSpec((tt, D_MODEL), lambda t, c: (t, 0)),
        out_shape=jax.ShapeDtypeStruct((T, D_MODEL), F32),
        scratch_shapes=[
            pltpu.VMEM((D_MODEL, tt), F32),
            pltpu.VMEM((ec, min(tt, PEER_SCORE_COLS)), F32),
            pltpu.VMEM((ec, tt), BF16),
            pltpu.VMEM((PEER_HEADS, N_KEYS, tt + GATE_PAD), F32),
            pltpu.VMEM((PEER_HEADS, N_KEYS, tt + GATE_PAD), F32),
        ],
        compiler_params=pltpu.CompilerParams(
            dimension_semantics=("arbitrary", "arbitrary"), vmem_limit_bytes=VMEM_LIMIT),
        name="peer",
    )(u_bf, vt_bf, h1t, r2, e2, n1, e1, h1, g2, be2)


def _peer_slab_kernel(u_ref, vt_ref, xt_ref, r2_ref, e2_ref, n1_ref, e1_ref, h1_ref, g2_ref, be2_ref,
                      out_ref, acc_ref, w_ref, a_ref, *, ec, tt):
    c = pl.program_id(1)
    nkb = ec // N_KEYS
    assert nkb == 8

    @pl.when(c == 0)
    def _():
        acc_ref[...] = jnp.zeros(acc_ref.shape, F32)

    def row_bf16(ref, h, ib, sl):
        return jnp.broadcast_to(ref[c, h, ib:ib + 1, sl], (BF16_ROWS, BF16_COLS)).astype(BF16)

    rows = BF16_ROWS
    a_ref[...] = _dot(u_ref[...], xt_ref[...])
    for nt in range(tt // BF16_COLS):
        nsl = slice(nt * BF16_COLS, (nt + 1) * BF16_COLS)
        for ib in range(nkb):
            n1b = [row_bf16(n1_ref, h, ib, nsl) for h in range(PEER_HEADS)]
            e1b = [row_bf16(e1_ref, h, ib, nsl) for h in range(PEER_HEADS)]
            for rg in range(N_KEYS // rows):
                ksl = slice(rg * rows, (rg + 1) * rows)
                g = None
                for h in range(PEER_HEADS):
                    e2 = e2_ref[h, ksl, nsl]
                    term = jnp.where(r2_ref[h, ksl, nsl] < n1b[h], e2, jnp.zeros_like(e2)) * e1b[h]
                    g = term if g is None else g + term
                esl = slice(ib * N_KEYS + rg * rows, ib * N_KEYS + (rg + 1) * rows)
                ab = a_ref[esl, nsl]
                ge = 0.5 * ab * (1.0 + lax.erf(ab * (2.0 ** -0.5)))
                w_ref[esl, nsl] = ge.astype(BF16) * g
    acc_ref[...] += _dot(vt_ref[...], w_ref[...])

    @pl.when(c == pl.num_programs(1) - 1)
    def _():
        z = DN_ALPHA * h1_ref[...] + acc_ref[...].T
        out_ref[...] = _layer_norm(z, g2_ref[...], be2_ref[...])


def _peer_slab(u_bf, vt_bf, h1t, r2, e2, n1, e1, h1, g2, be2, *, ec, tt):
    E = u_bf.shape[0]
    T = h1.shape[0]
    tokrows = pl.BlockSpec((PEER_HEADS, N_KEYS, tt), lambda t, c: (0, 0, t))
    tokkeys = pl.BlockSpec((N_KEYS // 8, PEER_HEADS, 8, tt), lambda t, c: (0, 0, 0, t))
    vec = pl.BlockSpec((1, D_MODEL), lambda t, c: (0, 0))
    return pl.pallas_call(
        functools.partial(_peer_slab_kernel, ec=ec, tt=tt),
        grid=(T // tt, E // ec),
        in_specs=[
            pl.BlockSpec((ec, D_MODEL), lambda t, c: (c, 0)),
            pl.BlockSpec((D_MODEL, ec), lambda t, c: (0, c)),
            pl.BlockSpec((D_MODEL, tt), lambda t, c: (0, t)),
            tokrows, tokrows, tokkeys, tokkeys,
            pl.BlockSpec((tt, D_MODEL), lambda t, c: (t, 0)),
            vec, vec,
        ],
        out_specs=pl.BlockSpec((tt, D_MODEL), lambda t, c: (t, 0)),
        out_shape=jax.ShapeDtypeStruct((T, D_MODEL), F32),
        scratch_shapes=[
            pltpu.VMEM((D_MODEL, tt), F32),
            pltpu.VMEM((ec, tt), BF16),
            pltpu.VMEM((ec, tt), F32),
        ],
        compiler_params=pltpu.CompilerParams(
            dimension_semantics=("arbitrary", "arbitrary"), vmem_limit_bytes=VMEM_LIMIT),
        name="peer",
    )(u_bf, vt_bf, h1t, r2, e2, n1, e1, h1, g2, be2)


def _rope_tables(positions):
    T = positions.size
    half = ROT_DIM // 2
    inv_freq = ROPE_THETA ** (-jnp.arange(0, ROT_DIM, 2, dtype=F32) / ROT_DIM)
    ang = positions.reshape(T, 1).astype(F32) * inv_freq
    cos, sin = jnp.cos(ang), jnp.sin(ang)
    zeros = lambda w: jnp.zeros((T, w), F32)
    rc = jnp.concatenate([cos, cos, jnp.ones((T, HEAD_DIM - ROT_DIM), F32)], axis=1)
    rsa = jnp.concatenate([-sin, zeros(HEAD_DIM - half)], axis=1)
    rsb = jnp.concatenate([zeros(half), sin, zeros(HEAD_DIM - ROT_DIM)], axis=1)
    rep = LANES // HEAD_DIM
    return jnp.tile(rc, (1, rep)), jnp.tile(rsa, (1, rep)), jnp.tile(rsb, (1, rep))


def _tiles(seq, total_tokens):
    tm = next(c for c in (512, 256, BLOCK) if seq % c == 0)
    ts = next(c for c in (1024, 512, LANES) if total_tokens % c == 0)
    tt = 512 if total_tokens % 512 == 0 else 256
    ec = 8 * N_KEYS
    return tm, ts, tt, ec


def kernel(x, positions, w_in, b_in, attn_sinks, conv_w, w_out, b_out, ln1_g, ln1_b,
           w_pq, sub_keys1, sub_keys2, u_experts, v_experts, ln2_g, ln2_b):
    B, S, D = x.shape
    T = B * S
    assert D == D_MODEL and S % BLOCK == 0 and T % 256 == 0
    assert w_in.shape[0] == DEPTH
    tm, ts, tt, ec = _tiles(S, T)
    row = lambda v: v.reshape(1, -1).astype(F32)

    rc, rsa, rsb = _rope_tables(positions)
    h1, h1t, s1t, s2t = _front(
        x.reshape(T, D), rc, rsa, rsb, attn_sinks[0].astype(F32),
        w_in[0].astype(BF16), row(b_in[0]), conv_w[0].astype(F32),
        w_out[0].astype(BF16), row(b_out[0]), row(ln1_g[0]), row(ln1_b[0]),
        w_pq[0].astype(BF16), sub_keys1[0].astype(BF16), sub_keys2[0].astype(BF16),
        seq=S, tm=tm)
    r2, e2, n1, e1 = _select(s1t, s2t, ts=ts)
    out = _peer_slab(u_experts[0].astype(BF16), v_experts[0].T.astype(BF16), h1t,
                r2, e2, n1, e1, h1, row(ln2_g[0]), row(ln2_b[0]), ec=ec, tt=tt)
    return out.reshape(B, S, D)
```

```python
import functools

import jax
import jax.numpy as jnp
from jax import lax
from jax.experimental import pallas as pl
from jax.experimental.pallas import tpu as pltpu

D_MODEL = 1024
HEAD_DIM = 64
N_Q_HEADS = 8
N_KV_HEADS = 2
ATTN_WIDTH = N_Q_HEADS * HEAD_DIM
KV_WIDTH = N_KV_HEADS * HEAD_DIM
BLOCK = 128
ROT_DIM = HEAD_DIM // 4
ROPE_THETA = 500000.0
CONV_WIDTH = D_MODEL - ATTN_WIDTH
CONV_K = 3
N_KEYS = 128
PEER_HEADS = 8
PEER_QDIM = 256
PEER_HALF = PEER_QDIM // 2
PEER_TOPK = 16
DEPTH = 1
DN_ALPHA = (2.0 * DEPTH) ** 0.25
LN_EPS = 1e-5

LANES = 128
BF16_TILE_ROWS = 16
PEER_DOT_ROWS = 128
PEER_SCORE_COLS = 512
F32_TILE_ROWS = 8
BF16_ROWS = 8
BF16_COLS = 256
GATE_UNIT_KEYS = 64
GATE_PAD = 2 * LANES
VMEM_LIMIT = 56 * 1024 * 1024

_O_K = ATTN_WIDTH
_O_V = _O_K + KV_WIDTH
_O_GB = _O_V + KV_WIDTH
_O_GC = _O_GB + CONV_WIDTH
_O_XC = _O_GC + CONV_WIDTH
_O_END = _O_XC + CONV_WIDTH

BF16 = jnp.bfloat16
F32 = jnp.float32
NEG_INF = float("-inf")


def _dot(a, b):
    return jnp.dot(a, b, preferred_element_type=F32)


def _dot_nt(a, b):
    return lax.dot_general(a, b, (((1,), (1,)), ((), ())), preferred_element_type=F32)


def _layer_norm(z, g, b):
    mu = jnp.mean(z, axis=-1, keepdims=True)
    zc = z - mu
    var = jnp.mean(zc * zc, axis=-1, keepdims=True)
    return zc * lax.rsqrt(var + LN_EPS) * g + b


def _front_kernel(sinks_ref, x_ref, rc_ref, rsa_ref, rsb_ref, w_in_ref, b_in_ref,
                  conv_w_ref, w_out_ref, b_out_ref, g1_ref, be1_ref, w_pq_ref,
                  k1_ref, k2_ref,
                  h1_ref, h1t_ref, s1_ref, s2_ref,
                  kbuf, vbuf, ubuf, kvar, vvar, ybuf, *, tm, tiles_per_seq):
    step = pl.program_id(0)
    nblk = tm // BLOCK

    @pl.when(step % tiles_per_seq == 0)
    def _():
        kbuf[0:BLOCK, :] = jnp.zeros((BLOCK, KV_WIDTH), F32)
        vbuf[0:BLOCK, :] = jnp.zeros((BLOCK, KV_WIDTH), F32)
        ubuf[0:8, :] = jnp.zeros((8, CONV_WIDTH), F32)

    x = x_ref[...]
    xb = x.astype(BF16)

    def proj(lo, hi):
        return _dot(xb, w_in_ref[:, lo:hi]) + b_in_ref[:, lo:hi]

    rc = rc_ref[...]
    rsa = rsa_ref[...]
    rsb = rsb_ref[...]

    def rope(t):
        return (t * rc + pltpu.roll(t, LANES - ROT_DIM // 2, 1) * rsa
                + pltpu.roll(t, ROT_DIM // 2, 1) * rsb)

    kbuf[BLOCK:BLOCK + tm, :] = rope(proj(_O_K, _O_V))
    vbuf[BLOCK:BLOCK + tm, :] = proj(_O_V, _O_GB)

    lane = lax.broadcasted_iota(jnp.int32, (tm + BLOCK, LANES), 1)
    lo = lane < HEAD_DIM
    for buf, var in ((kbuf, kvar), (vbuf, vvar)):
        t = buf[...]
        tr = pltpu.roll(t, HEAD_DIM, 1)
        var[0] = jnp.where(lo, t, 0.0).astype(BF16)
        var[1] = jnp.where(lo, 0.0, tr).astype(BF16)
        var[2] = jnp.where(lo, tr, 0.0).astype(BF16)
        var[3] = jnp.where(lo, 0.0, t).astype(BF16)

    qi = lax.broadcasted_iota(jnp.int32, (BLOCK, 2 * BLOCK), 0)
    kj = lax.broadcasted_iota(jnp.int32, (BLOCK, 2 * BLOCK), 1)
    diff = qi + BLOCK - kj
    band = (diff >= 0) & (diff < BLOCK)

    for p in range(N_Q_HEADS // 2):
        q_slab = (rope(proj(p * LANES, (p + 1) * LANES)) * (HEAD_DIM ** -0.5)).astype(BF16)
        for blk in range(nblk):
            first = (step * nblk + blk) % (tiles_per_seq * nblk) == 0
            kmin = jnp.where(first, BLOCK, 0)
            mask = band & (kj >= kmin)
            qs = q_slab[blk * BLOCK:(blk + 1) * BLOCK, :]
            o = jnp.zeros((BLOCK, LANES), F32)
            for hh in range(2):
                h = 2 * p + hh
                var = (h // (N_Q_HEADS // N_KV_HEADS)) * 2 + hh
                kc = kvar[var, blk * BLOCK:(blk + 2) * BLOCK, :]
                vc = vvar[var, blk * BLOCK:(blk + 2) * BLOCK, :]
                s = jnp.where(mask, _dot_nt(qs, kc), NEG_INF)
                sink = sinks_ref[h]
                m = jnp.maximum(jnp.max(s, axis=-1, keepdims=True), sink)
                pe = jnp.exp(s - m)
                den = jnp.sum(pe, axis=-1, keepdims=True) + jnp.exp(sink - m)
                o = o + _dot((pe / den).astype(BF16), vc)
            ybuf[blk * BLOCK:(blk + 1) * BLOCK, p * LANES:(p + 1) * LANES] = o.astype(BF16)

    u = proj(_O_GC, _O_XC) * proj(_O_XC, _O_END)
    ubuf[8:8 + tm, :] = u
    um1 = ubuf[7:7 + tm, :]
    um2 = ubuf[6:6 + tm, :]
    cw = conv_w_ref[...]
    yc = proj(_O_GB, _O_GC) * (cw[0:1, :] * um2 + cw[1:2, :] * um1 + cw[2:3, :] * u)
    ybuf[:, ATTN_WIDTH:] = yc.astype(BF16)

    ubuf[0:8, :] = ubuf[tm:tm + 8, :]
    kbuf[0:BLOCK, :] = kbuf[tm:tm + BLOCK, :]
    vbuf[0:BLOCK, :] = vbuf[tm:tm + BLOCK, :]

    mix = _dot(ybuf[...], w_out_ref[...]) + b_out_ref[...]
    h1 = _layer_norm(DN_ALPHA * x + mix, g1_ref[...], be1_ref[...])
    h1_ref[...] = h1
    h1t_ref[...] = h1.T.astype(BF16)

    qp = _dot(h1.astype(BF16), w_pq_ref[...]).astype(BF16)
    k1 = k1_ref[...]
    k2 = k2_ref[...]
    for h in range(PEER_HEADS):
        base = h * PEER_QDIM
        s1_ref[h] = _dot_nt(k1, qp[:, base:base + PEER_HALF])
        s2_ref[h] = _dot_nt(k2, qp[:, base + PEER_HALF:base + PEER_QDIM])


def _front(xf, rc, rsa, rsb, sinks, w_in, b_in, conv_w, w_out, b_out, g1, be1, w_pq, k1, k2,
           *, seq, tm):
    T = xf.shape[0]
    proj_w = w_in.shape[1]
    tiles_per_seq = seq // tm
    full = lambda shape: pl.BlockSpec(shape, lambda i: (0,) * len(shape))
    tok = lambda w: pl.BlockSpec((tm, w), lambda i: (i, 0))
    kern = functools.partial(_front_kernel, tm=tm, tiles_per_seq=tiles_per_seq)
    return pl.pallas_call(
        kern,
        grid=(T // tm,),
        in_specs=[
            pl.BlockSpec(memory_space=pltpu.SMEM),
            tok(D_MODEL), tok(LANES), tok(LANES), tok(LANES),
            full((D_MODEL, proj_w)), full((1, proj_w)),
            full((CONV_K, CONV_WIDTH)), full((D_MODEL, D_MODEL)), full((1, D_MODEL)),
            full((1, D_MODEL)), full((1, D_MODEL)),
            full((D_MODEL, PEER_HEADS * PEER_QDIM)),
            full((N_KEYS, PEER_HALF)), full((N_KEYS, PEER_HALF)),
        ],
        out_specs=[
            tok(D_MODEL),
            pl.BlockSpec((D_MODEL, tm), lambda i: (0, i)),
            pl.BlockSpec((PEER_HEADS, N_KEYS, tm), lambda i: (0, 0, i)),
            pl.BlockSpec((PEER_HEADS, N_KEYS, tm), lambda i: (0, 0, i)),
        ],
        out_shape=[
            jax.ShapeDtypeStruct((T, D_MODEL), F32),
            jax.ShapeDtypeStruct((D_MODEL, T), BF16),
            jax.ShapeDtypeStruct((PEER_HEADS, N_KEYS, T), F32),
            jax.ShapeDtypeStruct((PEER_HEADS, N_KEYS, T), F32),
        ],
        scratch_shapes=[
            pltpu.VMEM((tm + BLOCK, KV_WIDTH), F32),
            pltpu.VMEM((tm + BLOCK, KV_WIDTH), F32),
            pltpu.VMEM((tm + 8, CONV_WIDTH), F32),
            pltpu.VMEM((4, tm + BLOCK, KV_WIDTH), BF16),
            pltpu.VMEM((4, tm + BLOCK, KV_WIDTH), BF16),
            pltpu.VMEM((tm, D_MODEL), BF16),
        ],
        compiler_params=pltpu.CompilerParams(
            dimension_semantics=("arbitrary",), vmem_limit_bytes=VMEM_LIMIT),
        name="front",
    )(sinks, xf, rc, rsa, rsb, w_in, b_in, conv_w, w_out, b_out, g1, be1, w_pq, k1, k2)


def _top16(s):
    rows = lax.broadcasted_iota(jnp.int32, s.shape, 0)
    rem = s
    rank = jnp.full(s.shape, float(PEER_TOPK), F32)
    vals = []
    for k in range(PEER_TOPK):
        m = jnp.max(rem, axis=0, keepdims=True)
        idx = jnp.min(jnp.where(rem == m, rows, N_KEYS), axis=0, keepdims=True)
        sel = rows == idx
        rem = jnp.where(sel, NEG_INF, rem)
        rank = jnp.where(sel, float(k), rank)
        vals.append(m)
    return vals, rank


_N_CAND_ROWS = PEER_TOPK + 7 * 8 + 8


def _cand_positions():
    import numpy as np
    pos = np.full((_N_CAND_ROWS,), -1, np.int32)
    for r in range(_N_CAND_ROWS):
        if r < 16:
            a, b = 0, r
        elif r < 72:
            a, b = 1 + (r - 16) // 8, (r - 16) % 8
        else:
            a, b = 8 + (r - 72), 0
        if (a + 1) * (b + 1) <= PEER_TOPK:
            pos[r] = a * PEER_TOPK + b
    return pos


def _select_exact(s1, s2, pos, valid, row16):
    v1, rank1 = _top16(s1)
    v2, rank2 = _top16(s2)

    v1m = jnp.zeros((PEER_TOPK, LANES), F32)
    v2m = jnp.zeros((PEER_TOPK, LANES), F32)
    for a in range(PEER_TOPK):
        v1m = jnp.where(row16 == a, v1[a], v1m)
        v2m = jnp.where(row16 == a, v2[a], v2m)
    blocks = [v1[0] + v2m]
    for a in range(1, 8):
        blocks.append(v1[a] + v2m[0:8, :])
    blocks.append(v1m[8:16, :] + v2[0])
    cand = jnp.where(valid, jnp.concatenate(blocks, axis=0), NEG_INF)

    rem = cand
    chosen = jnp.zeros(cand.shape, F32)
    for _ in range(PEER_TOPK):
        m = jnp.max(rem, axis=0, keepdims=True)
        idx = jnp.min(jnp.where(rem == m, pos, 1 << 20), axis=0, keepdims=True)
        sel = (pos == idx) & valid
        rem = jnp.where(sel, NEG_INF, rem)
        chosen = jnp.where(sel, 1.0, chosen)

    top = v1[0] + v2[0]
    z = jnp.sum(chosen * jnp.exp(jnp.where(valid, cand, top) - top), axis=0, keepdims=True)

    counts = [jnp.sum(chosen[0:16, :], axis=0, keepdims=True)]
    for a in range(1, 8):
        counts.append(jnp.sum(chosen[8 + 8 * a:16 + 8 * a, :], axis=0, keepdims=True))
    for a in range(8, 16):
        counts.append(chosen[64 + a:65 + a, :])
    n1 = jnp.zeros(s1.shape, F32)
    for a in range(PEER_TOPK):
        n1 = jnp.where(rank1 == float(a), counts[a], n1)
    return n1, jnp.exp(s1 - v1[0]), rank2, jnp.exp(s2 - v2[0]) / z


def _sort_pairs(n):
    pairs = []
    p = 1
    while p < n:
        k = p
        while k >= 1:
            for j in range(k % p, n - k, 2 * k):
                for i in range(min(k, n - j - k)):
                    if (i + j) // (2 * p) == (i + j + k) // (2 * p):
                        pairs.append((i + j, i + j + k))
            k //= 2
        p *= 2
    return pairs


def _hi_lo(a, b):
    if a is None:
        return b, None
    if b is None:
        return a, None
    return jnp.maximum(a, b), jnp.minimum(a, b)


def _sublane_all(x, op):
    for shift in (4, 2, 1):
        x = op(x, pltpu.roll(x, shift, 0))
    return x


def _top16_sorted(x):
    x = list(x)
    for i, j in _sort_pairs(len(x)):
        x[i], x[j] = _hi_lo(x[i], x[j])
    for shift in (4, 2, 1):
        y = [None if t is None else pltpu.roll(t, shift, 0) for t in x]
        c = [_hi_lo(x[r], y[PEER_TOPK - 1 - r])[0] for r in range(PEER_TOPK)]
        d = PEER_TOPK // 2
        while d >= 1:
            for i in range(PEER_TOPK):
                if not i & d:
                    c[i], c[i + d] = _hi_lo(c[i], c[i + d])
            d //= 2
        x = c
    return x


def _select_fast(s1, s2):
    nv = N_KEYS // 8
    t1 = [s1[8 * v:8 * v + 8, :] for v in range(nv)]
    t2 = [s2[8 * v:8 * v + 8, :] for v in range(nv)]
    v1 = _top16_sorted(t1)
    v2 = _top16_sorted(t2)
    sub = lax.broadcasted_iota(jnp.int32, (8, LANES), 0)
    one = jnp.ones((8, LANES), F32)
    zero = jnp.zeros((8, LANES), F32)

    def spread(vals):
        out = vals[0]
        for r in range(1, 8):
            out = jnp.where(sub == r, vals[r], out)
        return out

    v2lo, v2hi, v1hi = spread(v2[0:8]), spread(v2[8:16]), spread(v1[8:16])
    cand = [v1[0] + v2lo, v1[0] + v2hi]
    for a in range(1, 8):
        cand.append(jnp.where(sub < PEER_TOPK // (a + 1), v1[a] + v2lo, NEG_INF))
    cand.append(v1hi + v2[0])
    best = _top16_sorted(cand + [None] * (PEER_TOPK - len(cand)))
    tau = best[PEER_TOPK - 1]
    z = one
    for k in range(1, PEER_TOPK):
        z = z + jnp.exp(best[k] - best[0])

    count = lambda t: _sublane_all(jnp.where(t >= tau, one, zero), jnp.add)
    counts = [count(cand[0]) + count(cand[1])]
    counts += [count(cand[a + 1]) for a in range(1, 8)]
    counts += [jnp.where(v1[a] + v2[0] >= tau, one, zero) for a in range(8, PEER_TOPK)]
    total = counts[0]
    for a in range(1, PEER_TOPK):
        total = total + counts[a]
    tie = jnp.where(total != float(PEER_TOPK), one, zero)

    n1 = [zero] * nv
    rank2 = [jnp.full((8, LANES), float(PEER_TOPK), F32)] * nv
    for a in reversed(range(PEER_TOPK)):
        n1 = [jnp.where(t >= v1[a], counts[a], n) for t, n in zip(t1, n1)]
        rank2 = [jnp.where(t >= v2[a], float(a), r) for t, r in zip(t2, rank2)]
    for v, tiles in ((v1, t1), (v2, t2)):
        inside = zero
        for t in tiles:
            inside = inside + jnp.where(t >= v[PEER_TOPK - 1], one, zero)
        tie = jnp.where(_sublane_all(inside, jnp.add) != float(PEER_TOPK), one, tie)
        for a in range(PEER_TOPK - 1):
            tie = jnp.where(v[a] == v[a + 1], one, tie)

    inv_z = 1.0 / z
    cat = lambda tiles: jnp.concatenate(tiles, axis=0)
    e1 = cat([jnp.exp(t - v1[0]) for t in t1])
    e2 = cat([jnp.exp(t - v2[0]) * inv_z for t in t2])
    return cat(n1), e1, cat(rank2), e2, tie


def _select_kernel(pos_ref, s1_ref, s2_ref, r2_ref, e2_ref, n1_ref, e1_ref, *, ts):
    def group(g, carry):
        sl = pl.ds(pl.multiple_of(g * LANES, LANES), LANES)
        s1 = s1_ref[0, :, sl]
        s2 = s2_ref[0, :, sl]
        n1, e1, rank2, e2, tie = _select_fast(s1, s2)

        def exact():
            pos = pos_ref[...]
            row16 = lax.broadcasted_iota(jnp.int32, (PEER_TOPK, LANES), 0)
            return _select_exact(s1, s2, pos, pos >= 0, row16)

        n1, e1, rank2, e2 = lax.cond(jnp.max(tie) > 0.0, exact, lambda: (n1, e1, rank2, e2))
        n1_ref[:, 0, :, sl] = n1.reshape(N_KEYS // 8, 8, LANES)
        e1_ref[:, 0, :, sl] = e1.reshape(N_KEYS // 8, 8, LANES)
        r2_ref[0, :, sl] = rank2.astype(BF16)
        e2_ref[0, :, sl] = e2.astype(BF16)
        return carry

    lax.fori_loop(0, ts // LANES, group, 0)


def _select(s1t, s2t, *, ts):
    H, K, T = s1t.shape
    import numpy as np
    pos = jnp.asarray(np.tile(_cand_positions()[:, None], (1, LANES)))
    blk = pl.BlockSpec((1, K, ts), lambda t, h: (h, 0, t))
    blk2 = blk
    blk1 = pl.BlockSpec((K // 8, 1, 8, ts), lambda t, h: (0, h, 0, t))
    return pl.pallas_call(
        functools.partial(_select_kernel, ts=ts),
        grid=(T // ts, H),
        in_specs=[pl.BlockSpec((_N_CAND_ROWS, LANES), lambda t, h: (0, 0)), blk, blk],
        out_specs=[blk2, blk2, blk1, blk1],
        out_shape=[
            jax.ShapeDtypeStruct((H, K, T), BF16),
            jax.ShapeDtypeStruct((H, K, T), BF16),
            jax.ShapeDtypeStruct((K // 8, H, 8, T), F32),
            jax.ShapeDtypeStruct((K // 8, H, 8, T), F32),
        ],
        compiler_params=pltpu.CompilerParams(
            dimension_semantics=("arbitrary", "arbitrary"), vmem_limit_bytes=VMEM_LIMIT),
        name="select",
    )(pos, s1t, s2t)


def _peer_kernel(u_ref, vt_ref, xt_ref, r2_in, e2_in, n1_in, e1_in, h1_ref, g2_ref, be2_ref,
                 out_ref, acc_ref, a0_ref, a1_ref, w0_ref, w1_ref,
                 r2_ref, e2_ref, n1_ref, e1_ref, *, ec, tt, nchunks):
    s = pl.program_id(1)
    nkb = ec // N_KEYS
    assert nkb == 8
    rows = BF16_TILE_ROWS

    @pl.when(s == 0)
    def _():
        acc_ref[...] = jnp.zeros(acc_ref.shape, F32)
        for ref in (a0_ref, a1_ref, w0_ref, w1_ref):
            ref[...] = jnp.zeros(ref.shape, ref.dtype)

    cg =jnp.clip(s - 1, 0, nchunks - 1)

    def stages(a_new, a_prev, w_new, w_old):
        def row_bf16(ref, h, ib, sl):
            return jnp.broadcast_to(ref[cg, h, ib:ib + 1, sl], (BF16_ROWS, BF16_COLS)).astype(BF16)

        def gate_unit(ib, nt):
            nsl = slice(nt * BF16_COLS, (nt + 1) * BF16_COLS)
            n1b = [row_bf16(n1_in, h, ib, nsl) for h in range(PEER_HEADS)]
            e1b = [row_bf16(e1_in, h, ib, nsl) for h in range(PEER_HEADS)]
            for rg in range(N_KEYS // BF16_ROWS):
                ksl = slice(rg * BF16_ROWS, (rg + 1) * BF16_ROWS)
                g = None
                for h in range(PEER_HEADS):
                    e2 = e2_in[h, ksl, nsl]
                    term = jnp.where(r2_in[h, ksl, nsl] < n1b[h], e2, jnp.zeros_like(e2)) * e1b[h]
                    g = term if g is None else g + term
                esl = slice(ib * N_KEYS + rg * BF16_ROWS, ib * N_KEYS + (rg + 1) * BF16_ROWS)
                ab = a_prev[esl, nsl]
                ge = 0.5 * ab * (1.0 + lax.erf(ab * (2.0 ** -0.5)))
                w_new[esl, nsl] = ge.astype(BF16) * g

        def out_piece(p):
            psl = slice(p * PEER_DOT_ROWS, (p + 1) * PEER_DOT_ROWS)
            acc_ref[psl, :] += _dot(vt_ref[psl, :], w_old[...])

        def score_piece(p):
            psl = slice(p * PEER_DOT_ROWS, (p + 1) * PEER_DOT_ROWS)
            a_new[psl, :] = _dot(u_ref[psl, :], xt_ref[...])

        units = [(ib, nt) for ib in range(nkb) for nt in range(tt // BF16_COLS)]
        npieces = ec // PEER_DOT_ROWS
        pieces = [f for p in range(npieces) for f in ((out_piece, p), (score_piece, p))]
        per = len(units) // len(pieces)
        assert per * len(pieces) == len(units)
        for n, (fn, p) in enumerate(pieces):
            for ib, nt in units[n * per:(n + 1) * per]:
                gate_unit(ib, nt)
            fn(p)


    @pl.when(s % 2 == 0)
    def _():
        stages(a0_ref, a1_ref, w1_ref, w0_ref)

    @pl.when(s % 2 == 1)
    def _():
        stages(a1_ref, a0_ref, w0_ref, w1_ref)

    @pl.when(s == nchunks + 1)
    def _():
        z = DN_ALPHA * h1_ref[...] + acc_ref[...].T
        out_ref[...] = _layer_norm(z, g2_ref[...], be2_ref[...])


def _peer(u_bf, vt_bf, h1t, r2, e2, n1, e1, h1, g2, be2, *, ec, tt):
    E = u_bf.shape[0]
    T = h1.shape[0]
    nchunks = E // ec
    last = nchunks - 1
    tokrows = pl.BlockSpec((PEER_HEADS, N_KEYS, tt), lambda t, s: (0, 0, t))
    tokkeys = pl.BlockSpec((N_KEYS // 8, PEER_HEADS, 8, tt), lambda t, s: (0, 0, 0, t))
    vec = pl.BlockSpec((1, D_MODEL), lambda t, s: (0, 0))
    return pl.pallas_call(
        functools.partial(_peer_kernel, ec=ec, tt=tt, nchunks=nchunks),
        grid=(T // tt, nchunks + 2),
        in_specs=[
            pl.BlockSpec((ec, D_MODEL), lambda t, s: (jnp.minimum(s, last), 0)),
            pl.BlockSpec((D_MODEL, ec), lambda t, s: (0, jnp.clip(s - 2, 0, last))),
            pl.BlockSpec((D_MODEL, tt), lambda t, s: (0, t)),
            tokrows, tokrows, tokkeys, tokkeys,
            pl.BlockSpec((tt, D_MODEL), lambda t, s: (t, 0)),
            vec, vec,
        ],
        out_specs=pl.BlockSpec((tt, D_MODEL), lambda t, s: (t, 0)),
        out_shape=jax.ShapeDtypeStruct((T, D_MODEL), F32),
        scratch_shapes=[
            pltpu.VMEM((D_MODEL, tt), F32),
            pltpu.VMEM((ec, tt), F32),
            pltpu.VMEM((ec, tt), F32),
            pltpu.VMEM((ec, tt), BF16),
            pltpu.VMEM((ec, tt), BF16),
            pltpu.VMEM((PEER_HEADS, N_KEYS, tt + GATE_PAD), F32),
            pltpu.VMEM((PEER_HEADS, N_KEYS, tt + GATE_PAD), F32),
            pltpu.VMEM((N_KEYS // 8, PEER_HEADS, 8, tt), F32),
            pltpu.VMEM((N_KEYS // 8, PEER_HEADS, 8, tt), F32),
        ],
        compiler_params=pltpu.CompilerParams(
            dimension_semantics=("arbitrary", "arbitrary"), vmem_limit_bytes=VMEM_LIMIT),
        name="peer",
    )(u_bf, vt_bf, h1t, r2, e2, n1, e1, h1, g2, be2)


def _peer_dense_kernel(u_ref, vt_ref, xt_ref, r2_in, e2_in, n1_in, e1_in, h1_ref, g2_ref, be2_ref,
                       out_ref, acc_ref, a_ref, w_ref, r2_ref, e2_ref, *, ec, tt):
    c = pl.program_id(1)
    nkb = ec // N_KEYS
    assert nkb == 8
    rows = F32_TILE_ROWS
    nrg = GATE_UNIT_KEYS // rows

    @pl.when(c == 0)
    def _():
        acc_ref[...] = jnp.zeros(acc_ref.shape, F32)
        r2_ref[:, :, 0:tt] = r2_in[...].astype(F32)
        e2_ref[:, :, LANES:LANES + tt] = e2_in[...].astype(F32)

    ta = a_ref.shape[1]

    def row_tile(ref, h, ib, sl):
        return jnp.broadcast_to(ref[c, h, ib:ib + 1, sl], (rows, LANES))

    def gate_unit(kp, lt, part):
        lsl = slice(lt * LANES, (lt + 1) * LANES)
        keys = (2 * kp, 2 * kp + 1)
        g = [[None] * nrg for _ in keys]
        for h in range(PEER_HEADS):
            n1b = [row_tile(n1_in, h, ib, lsl) for ib in keys]
            e1b = [row_tile(e1_in, h, ib, lsl) for ib in keys]
            for rg in range(nrg):
                j0 = part * GATE_UNIT_KEYS + rg * rows
                r2 = r2_ref[h, j0:j0 + rows, lsl]
                e2 = e2_ref[h, j0:j0 + rows, (lt + 1) * LANES:(lt + 2) * LANES]
                for k in range(2):
                    term = jnp.where(r2 < n1b[k], e2, 0.0) * e1b[k]
                    g[k][rg] = term if h == 0 else g[k][rg] + term
        for k, ib in enumerate(keys):
            for rp in range(nrg // 2):
                e0 = ib * N_KEYS + part * GATE_UNIT_KEYS + rp * BF16_TILE_ROWS
                esl = slice(e0, e0 + BF16_TILE_ROWS)
                ab = a_ref[esl, (lt * LANES) % ta:(lt * LANES) % ta + LANES]
                ge = 0.5 * ab * (1.0 + lax.erf(ab * (2.0 ** -0.5)))
                gg = jnp.concatenate([g[k][2 * rp], g[k][2 * rp + 1]], axis=0)
                w_ref[esl, lsl] = (ge * gg).astype(BF16)

    for nt in range(tt // ta):
        a_ref[...] = _dot(u_ref[...], xt_ref[:, nt * ta:(nt + 1) * ta])
        for lt in range(nt * ta // LANES, (nt + 1) * ta // LANES):
            for kp in range(nkb // 2):
                for part in range(N_KEYS // GATE_UNIT_KEYS):
                    gate_unit(kp, lt, part)

    acc_ref[...] += _dot(vt_ref[...], w_ref[...])

    @pl.when(c == pl.num_programs(1) - 1)
    def _():
        z = DN_ALPHA * h1_ref[...] + acc_ref[...].T
        out_ref[...] = _layer_norm(z, g2_ref[...], be2_ref[...])


def _peer_dense(u_bf, vt_bf, h1t, r2, e2, n1, e1, h1, g2, be2, *, ec, tt):
    E = u_bf.shape[0]
    T = h1.shape[0]
    tokrows = pl.BlockSpec((PEER_HEADS, N_KEYS, tt), lambda t, c: (0, 0, t))
    tokkeys = pl.BlockSpec((N_KEYS // 8, PEER_HEADS, 8, tt), lambda t, c: (0, 0, 0, t))
    vec = pl.BlockSpec((1, D_MODEL), lambda t, c: (0, 0))
    return pl.pallas_call(
        functools.partial(_peer_dense_kernel, ec=ec, tt=tt),
        grid=(T // tt, E // ec),
        in_specs=[
            pl.BlockSpec((ec, D_MODEL), lambda t, c: (c, 0)),
            pl.BlockSpec((D_MODEL, ec), lambda t, c: (0, c)),
            pl.BlockSpec((D_MODEL, tt), lambda t, c: (0, t)),
            tokrows, tokrows, tokkeys, tokkeys,
            pl.BlockSpec((tt, D_MODEL), lambda t, c: (t, 0)),
            vec, vec,
        ],
        out_specs=pl.BlockSpec((tt, D_MODEL), lambda t, c: (t, 0)),
        out_shape=jax.ShapeDtypeStruct((T, D_MODEL), F32),
        scratch_shapes=[
            pltpu.VMEM((D_MODEL, tt), F32),
            pltpu.VMEM((ec, min(tt, PEER_SCORE_COLS)), F32),
            pltpu.VMEM((ec, tt), BF16),
            pltpu.VMEM((PEER_HEADS, N_KEYS, tt + GATE_PAD), F32),
            pltpu.VMEM((PEER_HEADS, N_KEYS, tt + GATE_PAD), F32),
        ],
        compiler_params=pltpu.CompilerParams(
            dimension_semantics=("arbitrary", "arbitrary"), vmem_limit_bytes=VMEM_LIMIT),
        name="peer",
    )(u_bf, vt_bf, h1t, r2, e2, n1, e1, h1, g2, be2)


def _peer_slab_kernel(u_ref, vt_ref, xt_ref, r2_ref, e2_ref, n1_ref, e1_ref, h1_ref, g2_ref, be2_ref,
                      out_ref, acc_ref, w_ref, a_ref, *, ec, tt):
    c = pl.program_id(1)
    nkb = ec // N_KEYS
    assert nkb == 8

    @pl.when(c == 0)
    def _():
        acc_ref[...] = jnp.zeros(acc_ref.shape, F32)

    def row_bf16(ref, h, ib, sl):
        return jnp.broadcast_to(ref[c, h, ib:ib + 1, sl], (BF16_ROWS, BF16_COLS)).astype(BF16)

    rows = BF16_ROWS
    a_ref[...] = _dot(u_ref[...], xt_ref[...])
    for nt in range(tt // BF16_COLS):
        nsl = slice(nt * BF16_COLS, (nt + 1) * BF16_COLS)
        for ib in range(nkb):
            n1b = [row_bf16(n1_ref, h, ib, nsl) for h in range(PEER_HEADS)]
            e1b = [row_bf16(e1_ref, h, ib, nsl) for h in range(PEER_HEADS)]
            for rg in range(N_KEYS // rows):
                ksl = slice(rg * rows, (rg + 1) * rows)
                g = None
                for h in range(PEER_HEADS):
                    e2 = e2_ref[h, ksl, nsl]
                    term = jnp.where(r2_ref[h, ksl, nsl] < n1b[h], e2, jnp.zeros_like(e2)) * e1b[h]
                    g = term if g is None else g + term
                esl = slice(ib * N_KEYS + rg * rows, ib * N_KEYS + (rg + 1) * rows)
                ab = a_ref[esl, nsl]
                ge = 0.5 * ab * (1.0 + lax.erf(ab * (2.0 ** -0.5)))
                w_ref[esl, nsl] = ge.astype(BF16) * g
    acc_ref[...] += _dot(vt_ref[...], w_ref[...])

    @pl.when(c == pl.num_programs(1) - 1)
    def _():
        z = DN_ALPHA * h1_ref[...] + acc_ref[...].T
        out_ref[...] = _layer_norm(z, g2_ref[...], be2_ref[...])


def _peer_slab(u_bf, vt_bf, h1t, r2, e2, n1, e1, h1, g2, be2, *, ec, tt):
    E = u_bf.shape[0]
    T = h1.shape[0]
    tokrows = pl.BlockSpec((PEER_HEADS, N_KEYS, tt), lambda t, c: (0, 0, t))
    tokkeys = pl.BlockSpec((N_KEYS // 8, PEER_HEADS, 8, tt), lambda t, c: (0, 0, 0, t))
    vec = pl.BlockSpec((1, D_MODEL), lambda t, c: (0, 0))
    return pl.pallas_call(
        functools.partial(_peer_slab_kernel, ec=ec, tt=tt),
        grid=(T // tt, E // ec),
        in_specs=[
            pl.BlockSpec((ec, D_MODEL), lambda t, c: (c, 0)),
            pl.BlockSpec((D_MODEL, ec), lambda t, c: (0, c)),
            pl.BlockSpec((D_MODEL, tt), lambda t, c: (0, t)),
            tokrows, tokrows, tokkeys, tokkeys,
            pl.BlockSpec((tt, D_MODEL), lambda t, c: (t, 0)),
            vec, vec,
        ],
        out_specs=pl.BlockSpec((tt, D_MODEL), lambda t, c: (t, 0)),
        out_shape=jax.ShapeDtypeStruct((T, D_MODEL), F32),
        scratch_shapes=[
            pltpu.VMEM((D_MODEL, tt), F32),
            pltpu.VMEM((ec, tt), BF16),
            pltpu.VMEM((ec, tt), F32),
        ],
        compiler_params=pltpu.CompilerParams(
            dimension_semantics=("arbitrary", "arbitrary"), vmem_limit_bytes=VMEM_LIMIT),
        name="peer",
    )(u_bf, vt_bf, h1t, r2, e2, n1, e1, h1, g2, be2)


def _rope_tables(positions):
    T = positions.size
    half = ROT_DIM // 2
    inv_freq = ROPE_THETA ** (-jnp.arange(0, ROT_DIM, 2, dtype=F32) / ROT_DIM)
    ang = positions.reshape(T, 1).astype(F32) * inv_freq
    cos, sin = jnp.cos(ang), jnp.sin(ang)
    zeros = lambda w: jnp.zeros((T, w), F32)
    rc = jnp.concatenate([cos, cos, jnp.ones((T, HEAD_DIM - ROT_DIM), F32)], axis=1)
    rsa = jnp.concatenate([-sin, zeros(HEAD_DIM - half)], axis=1)
    rsb = jnp.concatenate([zeros(half), sin, zeros(HEAD_DIM - ROT_DIM)], axis=1)
    rep = LANES // HEAD_DIM
    return jnp.tile(rc, (1, rep)), jnp.tile(rsa, (1, rep)), jnp.tile(rsb, (1, rep))


def _tiles(seq, total_tokens):
    tm = next(c for c in (512, 256, BLOCK) if seq % c == 0)
    ts = next(c for c in (1024, 512, LANES) if total_tokens % c == 0)
    tt = 512 if total_tokens % 512 == 0 else 256
    ec = 8 * N_KEYS
    return tm, ts, tt, ec


def kernel(x, positions, w_in, b_in, attn_sinks, conv_w, w_out, b_out, ln1_g, ln1_b,
           w_pq, sub_keys1, sub_keys2, u_experts, v_experts, ln2_g, ln2_b):
    B, S, D = x.shape
    T = B * S
    assert D == D_MODEL and S % BLOCK == 0 and T % 256 == 0
    assert w_in.shape[0] == DEPTH
    tm, ts, tt, ec = _tiles(S, T)
    row = lambda v: v.reshape(1, -1).astype(F32)

    rc, rsa, rsb = _rope_tables(positions)
    h1, h1t, s1t, s2t = _front(
        x.reshape(T, D), rc, rsa, rsb, attn_sinks[0].astype(F32),
        w_in[0].astype(BF16), row(b_in[0]), conv_w[0].astype(F32),
        w_out[0].astype(BF16), row(b_out[0]), row(ln1_g[0]), row(ln1_b[0]),
        w_pq[0].astype(BF16), sub_keys1[0].astype(BF16), sub_keys2[0].astype(BF16),
        seq=S, tm=tm)
    r2, e2, n1, e1 = _select(s1t, s2t, ts=ts)
    out = _peer(u_experts[0].astype(BF16), v_experts[0].T.astype(BF16), h1t,
                r2, e2, n1, e1, h1, row(ln2_g[0]), row(ln2_b[0]), ec=ec, tt=tt)
    return out.reshape(B, S, D)
```

```python
import functools

import jax
import jax.numpy as jnp
from jax import lax
from jax.experimental import pallas as pl
from jax.experimental.pallas import tpu as pltpu

D_MODEL = 1024
HEAD_DIM = 64
N_Q_HEADS = 8
N_KV_HEADS = 2
ATTN_WIDTH = N_Q_HEADS * HEAD_DIM
KV_WIDTH = N_KV_HEADS * HEAD_DIM
BLOCK = 128
ROT_DIM = HEAD_DIM // 4
ROPE_THETA = 500000.0
CONV_WIDTH = D_MODEL - ATTN_WIDTH
CONV_K = 3
N_KEYS = 128
PEER_HEADS = 8
PEER_QDIM = 256
PEER_HALF = PEER_QDIM // 2
PEER_TOPK = 16
DEPTH = 1
DN_ALPHA = (2.0 * DEPTH) ** 0.25
LN_EPS = 1e-5

LANES = 128
BF16_TILE_ROWS = 16
PEER_DOT_ROWS = 128
PEER_SCORE_COLS = 512
F32_TILE_ROWS = 8
BF16_ROWS = 8
BF16_COLS = 256
GATE_UNIT_KEYS = 64
GATE_PAD = 2 * LANES
VMEM_LIMIT = 56 * 1024 * 1024

_O_K = ATTN_WIDTH
_O_V = _O_K + KV_WIDTH
_O_GB = _O_V + KV_WIDTH
_O_GC = _O_GB + CONV_WIDTH
_O_XC = _O_GC + CONV_WIDTH
_O_END = _O_XC + CONV_WIDTH

BF16 = jnp.bfloat16
F32 = jnp.float32
NEG_INF = float("-inf")


def _dot(a, b):
    return jnp.dot(a, b, preferred_element_type=F32)


def _dot_nt(a, b):
    return lax.dot_general(a, b, (((1,), (1,)), ((), ())), preferred_element_type=F32)


def _layer_norm(z, g, b):
    mu = jnp.mean(z, axis=-1, keepdims=True)
    zc = z - mu
    var = jnp.mean(zc * zc, axis=-1, keepdims=True)
    return zc * lax.rsqrt(var + LN_EPS) * g + b


def _front_kernel(sinks_ref, x_ref, rc_ref, rsa_ref, rsb_ref, w_in_ref, b_in_ref,
                  conv_w_ref, w_out_ref, b_out_ref, g1_ref, be1_ref, w_pq_ref,
                  k1_ref, k2_ref,
                  h1_ref, h1t_ref, s1_ref, s2_ref,
                  kbuf, vbuf, ubuf, kvar, vvar, ybuf, *, tm, tiles_per_seq):
    step = pl.program_id(0)
    nblk = tm // BLOCK

    @pl.when(step % tiles_per_seq == 0)
    def _():
        kbuf[0:BLOCK, :] = jnp.zeros((BLOCK, KV_WIDTH), F32)
        vbuf[0:BLOCK, :] = jnp.zeros((BLOCK, KV_WIDTH), F32)
        ubuf[0:8, :] = jnp.zeros((8, CONV_WIDTH), F32)

    x = x_ref[...]
    xb = x.astype(BF16)

    def proj(lo, hi):
        return _dot(xb, w_in_ref[:, lo:hi]) + b_in_ref[:, lo:hi]

    rc = rc_ref[...]
    rsa = rsa_ref[...]
    rsb = rsb_ref[...]

    def rope(t):
        return (t * rc + pltpu.roll(t, LANES - ROT_DIM // 2, 1) * rsa
                + pltpu.roll(t, ROT_DIM // 2, 1) * rsb)

    kbuf[BLOCK:BLOCK + tm, :] = rope(proj(_O_K, _O_V))
    vbuf[BLOCK:BLOCK + tm, :] = proj(_O_V, _O_GB)

    lane = lax.broadcasted_iota(jnp.int32, (tm + BLOCK, LANES), 1)
    lo = lane < HEAD_DIM
    for buf, var in ((kbuf, kvar), (vbuf, vvar)):
        t = buf[...]
        tr = pltpu.roll(t, HEAD_DIM, 1)
        var[0] = jnp.where(lo, t, 0.0).astype(BF16)
        var[1] = jnp.where(lo, 0.0, tr).astype(BF16)
        var[2] = jnp.where(lo, tr, 0.0).astype(BF16)
        var[3] = jnp.where(lo, 0.0, t).astype(BF16)

    qi = lax.broadcasted_iota(jnp.int32, (BLOCK, 2 * BLOCK), 0)
    kj = lax.broadcasted_iota(jnp.int32, (BLOCK, 2 * BLOCK), 1)
    diff = qi + BLOCK - kj
    band = (diff >= 0) & (diff < BLOCK)

    for p in range(N_Q_HEADS // 2):
        q_slab = (rope(proj(p * LANES, (p + 1) * LANES)) * (HEAD_DIM ** -0.5)).astype(BF16)
        for blk in range(nblk):
            first = (step * nblk + blk) % (tiles_per_seq * nblk) == 0
            kmin = jnp.where(first, BLOCK, 0)
            mask = band & (kj >= kmin)
            qs = q_slab[blk * BLOCK:(blk + 1) * BLOCK, :]
            o = jnp.zeros((BLOCK, LANES), F32)
            for hh in range(2):
                h = 2 * p + hh
                var = (h // (N_Q_HEADS // N_KV_HEADS)) * 2 + hh
                kc = kvar[var, blk * BLOCK:(blk + 2) * BLOCK, :]
                vc = vvar[var, blk * BLOCK:(blk + 2) * BLOCK, :]
                s = jnp.where(mask, _dot_nt(qs, kc), NEG_INF)
                sink = sinks_ref[h]
                m = jnp.maximum(jnp.max(s, axis=-1, keepdims=True), sink)
                pe = jnp.exp(s - m)
                den = jnp.sum(pe, axis=-1, keepdims=True) + jnp.exp(sink - m)
                o = o + _dot((pe / den).astype(BF16), vc)
            ybuf[blk * BLOCK:(blk + 1) * BLOCK, p * LANES:(p + 1) * LANES] = o.astype(BF16)

    u = proj(_O_GC, _O_XC) * proj(_O_XC, _O_END)
    ubuf[8:8 + tm, :] = u
    um1 = ubuf[7:7 + tm, :]
    um2 = ubuf[6:6 + tm, :]
    cw = conv_w_ref[...]
    yc = proj(_O_GB, _O_GC) * (cw[0:1, :] * um2 + cw[1:2, :] * um1 + cw[2:3, :] * u)
    ybuf[:, ATTN_WIDTH:] = yc.astype(BF16)

    ubuf[0:8, :] = ubuf[tm:tm + 8, :]
    kbuf[0:BLOCK, :] = kbuf[tm:tm + BLOCK, :]
    vbuf[0:BLOCK, :] = vbuf[tm:tm + BLOCK, :]

    mix = _dot(ybuf[...], w_out_ref[...]) + b_out_ref[...]
    h1 = _layer_norm(DN_ALPHA * x + mix, g1_ref[...], be1_ref[...])
    h1_ref[...] = h1
    h1t_ref[...] = h1.T.astype(BF16)

    qp = _dot(h1.astype(BF16), w_pq_ref[...]).astype(BF16)
    k1 = k1_ref[...]
    k2 = k2_ref[...]
    for h in range(PEER_HEADS):
        base = h * PEER_QDIM
        s1_ref[h] = _dot_nt(k1, qp[:, base:base + PEER_HALF])
        s2_ref[h] = _dot_nt(k2, qp[:, base + PEER_HALF:base + PEER_QDIM])


def _front(xf, rc, rsa, rsb, sinks, w_in, b_in, conv_w, w_out, b_out, g1, be1, w_pq, k1, k2,
           *, seq, tm):
    T = xf.shape[0]
    proj_w = w_in.shape[1]
    tiles_per_seq = seq // tm
    full = lambda shape: pl.BlockSpec(shape, lambda i: (0,) * len(shape))
    tok = lambda w: pl.BlockSpec((tm, w), lambda i: (i, 0))
    kern = functools.partial(_front_kernel, tm=tm, tiles_per_seq=tiles_per_seq)
    return pl.pallas_call(
        kern,
        grid=(T // tm,),
        in_specs=[
            pl.BlockSpec(memory_space=pltpu.SMEM),
            tok(D_MODEL), tok(LANES), tok(LANES), tok(LANES),
            full((D_MODEL, proj_w)), full((1, proj_w)),
            full((CONV_K, CONV_WIDTH)), full((D_MODEL, D_MODEL)), full((1, D_MODEL)),
            full((1, D_MODEL)), full((1, D_MODEL)),
            full((D_MODEL, PEER_HEADS * PEER_QDIM)),
            full((N_KEYS, PEER_HALF)), full((N_KEYS, PEER_HALF)),
        ],
        out_specs=[
            tok(D_MODEL),
            pl.BlockSpec((D_MODEL, tm), lambda i: (0, i)),
            pl.BlockSpec((PEER_HEADS, N_KEYS, tm), lambda i: (0, 0, i)),
            pl.BlockSpec((PEER_HEADS, N_KEYS, tm), lambda i: (0, 0, i)),
        ],
        out_shape=[
            jax.ShapeDtypeStruct((T, D_MODEL), F32),
            jax.ShapeDtypeStruct((D_MODEL, T), BF16),
            jax.ShapeDtypeStruct((PEER_HEADS, N_KEYS, T), F32),
            jax.ShapeDtypeStruct((PEER_HEADS, N_KEYS, T), F32),
        ],
        scratch_shapes=[
            pltpu.VMEM((tm + BLOCK, KV_WIDTH), F32),
            pltpu.VMEM((tm + BLOCK, KV_WIDTH), F32),
            pltpu.VMEM((tm + 8, CONV_WIDTH), F32),
            pltpu.VMEM((4, tm + BLOCK, KV_WIDTH), BF16),
            pltpu.VMEM((4, tm + BLOCK, KV_WIDTH), BF16),
            pltpu.VMEM((tm, D_MODEL), BF16),
        ],
        compiler_params=pltpu.CompilerParams(
            dimension_semantics=("arbitrary",), vmem_limit_bytes=VMEM_LIMIT),
        name="front",
    )(sinks, xf, rc, rsa, rsb, w_in, b_in, conv_w, w_out, b_out, g1, be1, w_pq, k1, k2)


def _top16(s):
    rows = lax.broadcasted_iota(jnp.int32, s.shape, 0)
    rem = s
    rank = jnp.full(s.shape, float(PEER_TOPK), F32)
    vals = []
    for k in range(PEER_TOPK):
        m = jnp.max(rem, axis=0, keepdims=True)
        idx = jnp.min(jnp.where(rem == m, rows, N_KEYS), axis=0, keepdims=True)
        sel = rows == idx
        rem = jnp.where(sel, NEG_INF, rem)
        rank = jnp.where(sel, float(k), rank)
        vals.append(m)
    return vals, rank


_N_CAND_ROWS = PEER_TOPK + 7 * 8 + 8


def _cand_positions():
    import numpy as np
    pos = np.full((_N_CAND_ROWS,), -1, np.int32)
    for r in range(_N_CAND_ROWS):
        if r < 16:
            a, b = 0, r
        elif r < 72:
            a, b = 1 + (r - 16) // 8, (r - 16) % 8
        else:
            a, b = 8 + (r - 72), 0
        if (a + 1) * (b + 1) <= PEER_TOPK:
            pos[r] = a * PEER_TOPK + b
    return pos


def _select_exact(s1, s2, pos, valid, row16):
    v1, rank1 = _top16(s1)
    v2, rank2 = _top16(s2)

    v1m = jnp.zeros((PEER_TOPK, LANES), F32)
    v2m = jnp.zeros((PEER_TOPK, LANES), F32)
    for a in range(PEER_TOPK):
        v1m = jnp.where(row16 == a, v1[a], v1m)
        v2m = jnp.where(row16 == a, v2[a], v2m)
    blocks = [v1[0] + v2m]
    for a in range(1, 8):
        blocks.append(v1[a] + v2m[0:8, :])
    blocks.append(v1m[8:16, :] + v2[0])
    cand = jnp.where(valid, jnp.concatenate(blocks, axis=0), NEG_INF)

    rem = cand
    chosen = jnp.zeros(cand.shape, F32)
    for _ in range(PEER_TOPK):
        m = jnp.max(rem, axis=0, keepdims=True)
        idx = jnp.min(jnp.where(rem == m, pos, 1 << 20), axis=0, keepdims=True)
        sel = (pos == idx) & valid
        rem = jnp.where(sel, NEG_INF, rem)
        chosen = jnp.where(sel, 1.0, chosen)

    top = v1[0] + v2[0]
    z = jnp.sum(chosen * jnp.exp(jnp.where(valid, cand, top) - top), axis=0, keepdims=True)

    counts = [jnp.sum(chosen[0:16, :], axis=0, keepdims=True)]
    for a in range(1, 8):
        counts.append(jnp.sum(chosen[8 + 8 * a:16 + 8 * a, :], axis=0, keepdims=True))
    for a in range(8, 16):
        counts.append(chosen[64 + a:65 + a, :])
    n1 = jnp.zeros(s1.shape, F32)
    for a in range(PEER_TOPK):
        n1 = jnp.where(rank1 == float(a), counts[a], n1)
    return n1, jnp.exp(s1 - v1[0]), rank2, jnp.exp(s2 - v2[0]) / z


def _sort_pairs(n):
    pairs = []
    p = 1
    while p < n:
        k = p
        while k >= 1:
            for j in range(k % p, n - k, 2 * k):
                for i in range(min(k, n - j - k)):
                    if (i + j) // (2 * p) == (i + j + k) // (2 * p):
                        pairs.append((i + j, i + j + k))
            k //= 2
        p *= 2
    return pairs


def _hi_lo(a, b):
    if a is None:
        return b, None
    if b is None:
        return a, None
    return jnp.maximum(a, b), jnp.minimum(a, b)


def _sublane_all(x, op):
    for shift in (4, 2, 1):
        x = op(x, pltpu.roll(x, shift, 0))
    return x


def _top16_sorted(x):
    x = list(x)
    for i, j in _sort_pairs(len(x)):
        x[i], x[j] = _hi_lo(x[i], x[j])
    for shift in (4, 2, 1):
        y = [None if t is None else pltpu.roll(t, shift, 0) for t in x]
        c = [_hi_lo(x[r], y[PEER_TOPK - 1 - r])[0] for r in range(PEER_TOPK)]
        d = PEER_TOPK // 2
        while d >= 1:
            for i in range(PEER_TOPK):
                if not i & d:
                    c[i], c[i + d] = _hi_lo(c[i], c[i + d])
            d //= 2
        x = c
    return x


def _select_fast(s1, s2):
    nv = N_KEYS // 8
    t1 = [s1[8 * v:8 * v + 8, :] for v in range(nv)]
    t2 = [s2[8 * v:8 * v + 8, :] for v in range(nv)]
    v1 = _top16_sorted(t1)
    v2 = _top16_sorted(t2)
    sub = lax.broadcasted_iota(jnp.int32, (8, LANES), 0)
    one = jnp.ones((8, LANES), F32)
    zero = jnp.zeros((8, LANES), F32)

    def spread(vals):
        out = vals[0]
        for r in range(1, 8):
            out = jnp.where(sub == r, vals[r], out)
        return out

    v2lo, v2hi, v1hi = spread(v2[0:8]), spread(v2[8:16]), spread(v1[8:16])
    cand = [v1[0] + v2lo, v1[0] + v2hi]
    for a in range(1, 8):
        cand.append(jnp.where(sub < PEER_TOPK // (a + 1), v1[a] + v2lo, NEG_INF))
    cand.append(v1hi + v2[0])
    best = _top16_sorted(cand + [None] * (PEER_TOPK - len(cand)))
    tau = best[PEER_TOPK - 1]
    z = one
    for k in range(1, PEER_TOPK):
        z = z + jnp.exp(best[k] - best[0])

    count = lambda t: _sublane_all(jnp.where(t >= tau, one, zero), jnp.add)
    counts = [count(cand[0]) + count(cand[1])]
    counts += [count(cand[a + 1]) for a in range(1, 8)]
    counts += [jnp.where(v1[a] + v2[0] >= tau, one, zero) for a in range(8, PEER_TOPK)]
    total = counts[0]
    for a in range(1, PEER_TOPK):
        total = total + counts[a]
    tie = jnp.where(total != float(PEER_TOPK), one, zero)

    n1 = [zero] * nv
    rank2 = [jnp.full((8, LANES), float(PEER_TOPK), F32)] * nv
    for a in reversed(range(PEER_TOPK)):
        n1 = [jnp.where(t >= v1[a], counts[a], n) for t, n in zip(t1, n1)]
        rank2 = [jnp.where(t >= v2[a], float(a), r) for t, r in zip(t2, rank2)]
    for v, tiles in ((v1, t1), (v2, t2)):
        inside = zero
        for t in tiles:
            inside = inside + jnp.where(t >= v[PEER_TOPK - 1], one, zero)
        tie = jnp.where(_sublane_all(inside, jnp.add) != float(PEER_TOPK), one, tie)
        for a in range(PEER_TOPK - 1):
            tie = jnp.where(v[a] == v[a + 1], one, tie)

    inv_z = 1.0 / z
    cat = lambda tiles: jnp.concatenate(tiles, axis=0)
    e1 = cat([jnp.exp(t - v1[0]) for t in t1])
    e2 = cat([jnp.exp(t - v2[0]) * inv_z for t in t2])
    return cat(n1), e1, cat(rank2), e2, tie


def _select_kernel(pos_ref, s1_ref, s2_ref, r2_ref, e2_ref, n1_ref, e1_ref, *, ts):
    def group(g, carry):
        sl = pl.ds(pl.multiple_of(g * LANES, LANES), LANES)
        s1 = s1_ref[0, :, sl]
        s2 = s2_ref[0, :, sl]
        n1, e1, rank2, e2, tie = _select_fast(s1, s2)

        def exact():
            pos = pos_ref[...]
            row16 = lax.broadcasted_iota(jnp.int32, (PEER_TOPK, LANES), 0)
            return _select_exact(s1, s2, pos, pos >= 0, row16)

        n1, e1, rank2, e2 = lax.cond(jnp.max(tie) > 0.0, exact, lambda: (n1, e1, rank2, e2))
        n1_ref[:, 0, :, sl] = n1.reshape(N_KEYS // 8, 8, LANES)
        e1_ref[:, 0, :, sl] = e1.reshape(N_KEYS // 8, 8, LANES)
        r2_ref[0, :, sl] = rank2.astype(BF16)
        e2_ref[0, :, sl] = e2.astype(BF16)
        return carry

    lax.fori_loop(0, ts // LANES, group, 0)


def _select(s1t, s2t, *, ts):
    H, K, T = s1t.shape
    import numpy as np
    pos = jnp.asarray(np.tile(_cand_positions()[:, None], (1, LANES)))
    blk = pl.BlockSpec((1, K, ts), lambda t, h: (h, 0, t))
    blk2 = blk
    blk1 = pl.BlockSpec((K // 8, 1, 8, ts), lambda t, h: (0, h, 0, t))
    return pl.pallas_call(
        functools.partial(_select_kernel, ts=ts),
        grid=(T // ts, H),
        in_specs=[pl.BlockSpec((_N_CAND_ROWS, LANES), lambda t, h: (0, 0)), blk, blk],
        out_specs=[blk2, blk2, blk1, blk1],
        out_shape=[
            jax.ShapeDtypeStruct((H, K, T), BF16),
            jax.ShapeDtypeStruct((H, K, T), BF16),
            jax.ShapeDtypeStruct((K // 8, H, 8, T), F32),
            jax.ShapeDtypeStruct((K // 8, H, 8, T), F32),
        ],
        compiler_params=pltpu.CompilerParams(
            dimension_semantics=("arbitrary", "arbitrary"), vmem_limit_bytes=VMEM_LIMIT),
        name="select",
    )(pos, s1t, s2t)


def _peer_kernel(u_ref, vt_ref, xt_ref, r2_in, e2_in, n1_in, e1_in, h1_ref, g2_ref, be2_ref,
                 out_ref, acc_ref, a0_ref, a1_ref, w0_ref, w1_ref,
                 r2_ref, e2_ref, n1_ref, e1_ref, *, ec, tt, nchunks):
    s = pl.program_id(1)
    nkb = ec // N_KEYS
    assert nkb == 8
    rows = BF16_TILE_ROWS

    @pl.when(s == 0)
    def _():
        acc_ref[...] = jnp.zeros(acc_ref.shape, F32)
        for ref in (a0_ref, a1_ref, w0_ref, w1_ref):
            ref[...] = jnp.zeros(ref.shape, ref.dtype)

    cg =jnp.clip(s - 1, 0, nchunks - 1)

    def stages(a_new, a_prev, w_new, w_old):
        def row_bf16(ref, h, ib, sl):
            return jnp.broadcast_to(ref[cg, h, ib:ib + 1, sl], (BF16_ROWS, BF16_COLS)).astype(BF16)

        def gate_unit(ib, nt):
            nsl = slice(nt * BF16_COLS, (nt + 1) * BF16_COLS)
            n1b = [row_bf16(n1_in, h, ib, nsl) for h in range(PEER_HEADS)]
            e1b = [row_bf16(e1_in, h, ib, nsl) for h in range(PEER_HEADS)]
            for rg in range(N_KEYS // BF16_ROWS):
                ksl = slice(rg * BF16_ROWS, (rg + 1) * BF16_ROWS)
                g = None
                for h in range(PEER_HEADS):
                    e2 = e2_in[h, ksl, nsl]
                    term = jnp.where(r2_in[h, ksl, nsl] < n1b[h], e2, jnp.zeros_like(e2)) * e1b[h]
                    g = term if g is None else g + term
                esl = slice(ib * N_KEYS + rg * BF16_ROWS, ib * N_KEYS + (rg + 1) * BF16_ROWS)
                ab = a_prev[esl, nsl]
                ge = 0.5 * ab * (1.0 + lax.erf(ab * (2.0 ** -0.5)))
                w_new[esl, nsl] = ge.astype(BF16) * g

        def out_piece(p):
            psl = slice(p * PEER_DOT_ROWS, (p + 1) * PEER_DOT_ROWS)
            acc_ref[psl, :] += _dot(vt_ref[psl, :], w_old[...])

        def score_piece(p):
            psl = slice(p * PEER_DOT_ROWS, (p + 1) * PEER_DOT_ROWS)
            a_new[psl, :] = _dot(u_ref[psl, :], xt_ref[...])

        units = [(ib, nt) for ib in range(nkb) for nt in range(tt // BF16_COLS)]
        npieces = ec // PEER_DOT_ROWS
        pieces = [f for p in range(npieces) for f in ((out_piece, p), (score_piece, p))]
        per = len(units) // len(pieces)
        assert per * len(pieces) == len(units)
        for n, (fn, p) in enumerate(pieces):
            for ib, nt in units[n * per:(n + 1) * per]:
                gate_unit(ib, nt)
            fn(p)


    @pl.when(s % 2 == 0)
    def _():
        stages(a0_ref, a1_ref, w1_ref, w0_ref)

    @pl.when(s % 2 == 1)
    def _():
        stages(a1_ref, a0_ref, w0_ref, w1_ref)

    @pl.when(s == nchunks + 1)
    def _():
        z = DN_ALPHA * h1_ref[...] + acc_ref[...].T
        out_ref[...] = _layer_norm(z, g2_ref[...], be2_ref[...])


def _peer(u_bf, vt_bf, h1t, r2, e2, n1, e1, h1, g2, be2, *, ec, tt):
    E = u_bf.shape[0]
    T = h1.shape[0]
    nchunks = E // ec
    last = nchunks - 1
    tokrows = pl.BlockSpec((PEER_HEADS, N_KEYS, tt), lambda t, s: (0, 0, t))
    tokkeys = pl.BlockSpec((N_KEYS // 8, PEER_HEADS, 8, tt), lambda t, s: (0, 0, 0, t))
    vec = pl.BlockSpec((1, D_MODEL), lambda t, s: (0, 0))
    return pl.pallas_call(
        functools.partial(_peer_kernel, ec=ec, tt=tt, nchunks=nchunks),
        grid=(T // tt, nchunks + 2),
        in_specs=[
            pl.BlockSpec((ec, D_MODEL), lambda t, s: (jnp.minimum(s, last), 0)),
            pl.BlockSpec((D_MODEL, ec), lambda t, s: (0, jnp.clip(s - 2, 0, last))),
            pl.BlockSpec((D_MODEL, tt), lambda t, s: (0, t)),
            tokrows, tokrows, tokkeys, tokkeys,
            pl.BlockSpec((tt, D_MODEL), lambda t, s: (t, 0)),
            vec, vec,
        ],
        out_specs=pl.BlockSpec((tt, D_MODEL), lambda t, s: (t, 0)),
        out_shape=jax.ShapeDtypeStruct((T, D_MODEL), F32),
        scratch_shapes=[
            pltpu.VMEM((D_MODEL, tt), F32),
            pltpu.VMEM((ec, tt), F32),
            pltpu.VMEM((ec, tt), F32),
            pltpu.VMEM((ec, tt), BF16),
            pltpu.VMEM((ec, tt), BF16),
            pltpu.VMEM((PEER_HEADS, N_KEYS, tt + GATE_PAD), F32),
            pltpu.VMEM((PEER_HEADS, N_KEYS, tt + GATE_PAD), F32),
            pltpu.VMEM((N_KEYS // 8, PEER_HEADS, 8, tt), F32),
            pltpu.VMEM((N_KEYS // 8, PEER_HEADS, 8, tt), F32),
        ],
        compiler_params=pltpu.CompilerParams(
            dimension_semantics=("arbitrary", "arbitrary"), vmem_limit_bytes=VMEM_LIMIT),
        name="peer",
    )(u_bf, vt_bf, h1t, r2, e2, n1, e1, h1, g2, be2)


def _peer_dense_kernel(u_ref, vt_ref, xt_ref, r2_in, e2_in, n1_in, e1_in, h1_ref, g2_ref, be2_ref,
                       out_ref, acc_ref, a_ref, w_ref, r2_ref, e2_ref, *, ec, tt):
    c = pl.program_id(1)
    nkb = ec // N_KEYS
    assert nkb == 8
    rows = F32_TILE_ROWS
    nrg = GATE_UNIT_KEYS // rows

    @pl.when(c == 0)
    def _():
        acc_ref[...] = jnp.zeros(acc_ref.shape, F32)
        r2_ref[:, :, 0:tt] = r2_in[...].astype(F32)
        e2_ref[:, :, LANES:LANES + tt] = e2_in[...].astype(F32)

    ta = a_ref.shape[1]

    def row_tile(ref, h, ib, sl):
        return jnp.broadcast_to(ref[c, h, ib:ib + 1, sl], (rows, LANES))

    def gate_unit(kp, lt, part):
        lsl = slice(lt * LANES, (lt + 1) * LANES)
        keys = (2 * kp, 2 * kp + 1)
        g = [[None] * nrg for _ in keys]
        for h in range(PEER_HEADS):
            n1b = [row_tile(n1_in, h, ib, lsl) for ib in keys]
            e1b = [row_tile(e1_in, h, ib, lsl) for ib in keys]
            for rg in range(nrg):
                j0 = part * GATE_UNIT_KEYS + rg * rows
                r2 = r2_ref[h, j0:j0 + rows, lsl]
                e2 = e2_ref[h, j0:j0 + rows, (lt + 1) * LANES:(lt + 2) * LANES]
                for k in range(2):
                    term = jnp.where(r2 < n1b[k], e2, 0.0) * e1b[k]
                    g[k][rg] = term if h == 0 else g[k][rg] + term
        for k, ib in enumerate(keys):
            for rp in range(nrg // 2):
                e0 = ib * N_KEYS + part * GATE_UNIT_KEYS + rp * BF16_TILE_ROWS
                esl = slice(e0, e0 + BF16_TILE_ROWS)
                ab = a_ref[esl, (lt * LANES) % ta:(lt * LANES) % ta + LANES]
                ge = 0.5 * ab * (1.0 + lax.erf(ab * (2.0 ** -0.5)))
                gg = jnp.concatenate([g[k][2 * rp], g[k][2 * rp + 1]], axis=0)
                w_ref[esl, lsl] = (ge * gg).astype(BF16)

    for nt in range(tt // ta):
        a_ref[...] = _dot(u_ref[...], xt_ref[:, nt * ta:(nt + 1) * ta])
        for lt in range(nt * ta // LANES, (nt + 1) * ta // LANES):
            for kp in range(nkb // 2):
                for part in range(N_KEYS // GATE_UNIT_KEYS):
                    gate_unit(kp, lt, part)

    acc_ref[...] += _dot(vt_ref[...], w_ref[...])

    @pl.when(c == pl.num_programs(1) - 1)
    def _():
        z = DN_ALPHA * h1_ref[...] + acc_ref[...].T
        out_ref[...] = _layer_norm(z, g2_ref[...], be2_ref[...])


def _peer_dense(u_bf, vt_bf, h1t, r2, e2, n1, e1, h1, g2, be2, *, ec, tt):
    E = u_bf.shape[0]
    T = h1.shape[0]
    tokrows = pl.BlockSpec((PEER_HEADS, N_KEYS, tt), lambda t, c: (0, 0, t))
    tokkeys = pl.BlockSpec((N_KEYS // 8, PEER_HEADS, 8, tt), lambda t, c: (0, 0, 0, t))
    vec = pl.BlockSpec((1, D_MODEL), lambda t, c: (0, 0))
    return pl.pallas_call(
        functools.partial(_peer_dense_kernel, ec=ec, tt=tt),
        grid=(T // tt, E // ec),
        in_specs=[
            pl.BlockSpec((ec, D_MODEL), lambda t, c: (c, 0)),
            pl.BlockSpec((D_MODEL, ec), lambda t, c: (0, c)),
            pl.BlockSpec((D_MODEL, tt), lambda t, c: (0, t)),
            tokrows, tokrows, tokkeys, tokkeys,
            pl.BlockSpec((tt, D_MODEL), lambda t, c: (t, 0)),
            vec, vec,
        ],
        out_specs=pl.BlockSpec((tt, D_MODEL), lambda t, c: (t, 0)),
        out_shape=jax.ShapeDtypeStruct((T, D_MODEL), F32),
        scratch_shapes=[
            pltpu.VMEM((D_MODEL, tt), F32),
            pltpu.VMEM((ec, min(tt, PEER_SCORE_COLS)), F32),
            pltpu.VMEM((ec, tt), BF16),
            pltpu.VMEM((PEER_HEADS, N_KEYS, tt + GATE_PAD), F32),
            pltpu.VMEM((PEER_HEADS, N_KEYS, tt + GATE_PAD), F32),
        ],
        compiler_params=pltpu.CompilerParams(
            dimension_semantics=("arbitrary", "arbitrary"), vmem_limit_bytes=VMEM_LIMIT),
        name="peer",
    )(u_bf, vt_bf, h1t, r2, e2, n1, e1, h1, g2, be2)


def _peer_slab_kernel(u_ref, vt_ref, xt_ref, r2_ref, e2_ref, n1_ref, e1_ref, h1_ref, g2_ref, be2_ref,
                      out_ref, acc_ref, w_ref, a_ref, *, ec, tt):
    c = pl.program_id(1)
    nkb = ec // N_KEYS
    ngrp = nkb // 8
    assert ngrp * 8 == nkb

    @pl.when(c == 0)
    def _():
        acc_ref[...] = jnp.zeros(acc_ref.shape, F32)

    def row_bf16(ref, h, ib, sl):
        row = ref[c * ngrp + ib // 8, h, ib % 8:ib % 8 + 1, sl]
        return jnp.broadcast_to(row, (BF16_ROWS, BF16_COLS)).astype(BF16)

    rows = BF16_ROWS
    a_ref[...] = _dot(u_ref[...], xt_ref[...])
    for nt in range(tt // BF16_COLS):
        nsl = slice(nt * BF16_COLS, (nt + 1) * BF16_COLS)
        for ib in range(nkb):
            n1b = [row_bf16(n1_ref, h, ib, nsl) for h in range(PEER_HEADS)]
            e1b = [row_bf16(e1_ref, h, ib, nsl) for h in range(PEER_HEADS)]
            for rg in range(N_KEYS // rows):
                ksl = slice(rg * rows, (rg + 1) * rows)
                g = None
                for h in range(PEER_HEADS):
                    e2 = e2_ref[h, ksl, nsl]
                    term = jnp.where(r2_ref[h, ksl, nsl] < n1b[h], e2, jnp.zeros_like(e2)) * e1b[h]
                    g = term if g is None else g + term
                esl = slice(ib * N_KEYS + rg * rows, ib * N_KEYS + (rg + 1) * rows)
                ab = a_ref[esl, nsl]
                ge = 0.5 * ab * (1.0 + lax.erf(ab * (2.0 ** -0.5)))
                w_ref[esl, nsl] = ge.astype(BF16) * g
    acc_ref[...] += _dot(vt_ref[...], w_ref[...])

    @pl.when(c == pl.num_programs(1) - 1)
    def _():
        z = DN_ALPHA * h1_ref[...] + acc_ref[...].T
        out_ref[...] = _layer_norm(z, g2_ref[...], be2_ref[...])


def _peer_slab(u_bf, vt_bf, h1t, r2, e2, n1, e1, h1, g2, be2, *, ec, tt):
    E = u_bf.shape[0]
    T = h1.shape[0]
    tokrows = pl.BlockSpec((PEER_HEADS, N_KEYS, tt), lambda t, c: (0, 0, t))
    tokkeys = pl.BlockSpec((N_KEYS // 8, PEER_HEADS, 8, tt), lambda t, c: (0, 0, 0, t))
    vec = pl.BlockSpec((1, D_MODEL), lambda t, c: (0, 0))
    return pl.pallas_call(
        functools.partial(_peer_slab_kernel, ec=ec, tt=tt),
        grid=(T // tt, E // ec),
        in_specs=[
            pl.BlockSpec((ec, D_MODEL), lambda t, c: (c, 0)),
            pl.BlockSpec((D_MODEL, ec), lambda t, c: (0, c)),
            pl.BlockSpec((D_MODEL, tt), lambda t, c: (0, t)),
            tokrows, tokrows, tokkeys, tokkeys,
            pl.BlockSpec((tt, D_MODEL), lambda t, c: (t, 0)),
            vec, vec,
        ],
        out_specs=pl.BlockSpec((tt, D_MODEL), lambda t, c: (t, 0)),
        out_shape=jax.ShapeDtypeStruct((T, D_MODEL), F32),
        scratch_shapes=[
            pltpu.VMEM((D_MODEL, tt), F32),
            pltpu.VMEM((ec, tt), BF16),
            pltpu.VMEM((ec, tt), F32),
        ],
        compiler_params=pltpu.CompilerParams(
            dimension_semantics=("arbitrary", "arbitrary"), vmem_limit_bytes=VMEM_LIMIT),
        name="peer",
    )(u_bf, vt_bf, h1t, r2, e2, n1, e1, h1, g2, be2)


def _rope_tables(positions):
    T = positions.size
    half = ROT_DIM // 2
    inv_freq = ROPE_THETA ** (-jnp.arange(0, ROT_DIM, 2, dtype=F32) / ROT_DIM)
    ang = positions.reshape(T, 1).astype(F32) * inv_freq
    cos, sin = jnp.cos(ang), jnp.sin(ang)
    zeros = lambda w: jnp.zeros((T, w), F32)
    rc = jnp.concatenate([cos, cos, jnp.ones((T, HEAD_DIM - ROT_DIM), F32)], axis=1)
    rsa = jnp.concatenate([-sin, zeros(HEAD_DIM - half)], axis=1)
    rsb = jnp.concatenate([zeros(half), sin, zeros(HEAD_DIM - ROT_DIM)], axis=1)
    rep = LANES // HEAD_DIM
    return jnp.tile(rc, (1, rep)), jnp.tile(rsa, (1, rep)), jnp.tile(rsb, (1, rep))


def _tiles(seq, total_tokens):
    tm = next(c for c in (512, 256, BLOCK) if seq % c == 0)
    ts = next(c for c in (1024, 512, LANES) if total_tokens % c == 0)
    tt = 512 if total_tokens % 512 == 0 else 256
    ec = 16 * N_KEYS
    return tm, ts, tt, ec


def kernel(x, positions, w_in, b_in, attn_sinks, conv_w, w_out, b_out, ln1_g, ln1_b,
           w_pq, sub_keys1, sub_keys2, u_experts, v_experts, ln2_g, ln2_b):
    B, S, D = x.shape
    T = B * S
    assert D == D_MODEL and S % BLOCK == 0 and T % 256 == 0
    assert w_in.shape[0] == DEPTH
    tm, ts, tt, ec = _tiles(S, T)
    row = lambda v: v.reshape(1, -1).astype(F32)

    rc, rsa, rsb = _rope_tables(positions)
    h1, h1t, s1t, s2t = _front(
        x.reshape(T, D), rc, rsa, rsb, attn_sinks[0].astype(F32),
        w_in[0].astype(BF16), row(b_in[0]), conv_w[0].astype(F32),
        w_out[0].astype(BF16), row(b_out[0]), row(ln1_g[0]), row(ln1_b[0]),
        w_pq[0].astype(BF16), sub_keys1[0].astype(BF16), sub_keys2[0].astype(BF16),
        seq=S, tm=tm)
    r2, e2, n1, e1 = _select(s1t, s2t, ts=ts)
    out = _peer_slab(u_experts[0].astype(BF16), v_experts[0].T.astype(BF16), h1t,
                r2, e2, n1, e1, h1, row(ln2_g[0]), row(ln2_b[0]), ec=ec, tt=tt)
    return out.reshape(B, S, D)
```

```python
import functools

import jax
import jax.numpy as jnp
from jax import lax
from jax.experimental import pallas as pl
from jax.experimental.pallas import tpu as pltpu

D_MODEL = 1024
HEAD_DIM = 64
N_Q_HEADS = 8
N_KV_HEADS = 2
ATTN_WIDTH = N_Q_HEADS * HEAD_DIM
KV_WIDTH = N_KV_HEADS * HEAD_DIM
BLOCK = 128
ROT_DIM = HEAD_DIM // 4
ROPE_THETA = 500000.0
CONV_WIDTH = D_MODEL - ATTN_WIDTH
CONV_K = 3
N_KEYS = 128
PEER_HEADS = 8
PEER_QDIM = 256
PEER_HALF = PEER_QDIM // 2
PEER_TOPK = 16
DEPTH = 1
DN_ALPHA = (2.0 * DEPTH) ** 0.25
LN_EPS = 1e-5

LANES = 128
BF16_ROWS = 8
BF16_COLS = 256
VMEM_LIMIT = 56 * 1024 * 1024

_O_K = ATTN_WIDTH
_O_V = _O_K + KV_WIDTH
_O_GB = _O_V + KV_WIDTH
_O_GC = _O_GB + CONV_WIDTH
_O_XC = _O_GC + CONV_WIDTH
_O_END = _O_XC + CONV_WIDTH

BF16 = jnp.bfloat16
F32 = jnp.float32
NEG_INF = float("-inf")


def _dot(a, b):
    return jnp.dot(a, b, preferred_element_type=F32)


def _dot_nt(a, b):
    return lax.dot_general(a, b, (((1,), (1,)), ((), ())), preferred_element_type=F32)


def _layer_norm(z, g, b):
    mu = jnp.mean(z, axis=-1, keepdims=True)
    zc = z - mu
    var = jnp.mean(zc * zc, axis=-1, keepdims=True)
    return zc * lax.rsqrt(var + LN_EPS) * g + b


def _front_kernel(sinks_ref, x_ref, rc_ref, rsa_ref, rsb_ref, w_in_ref, b_in_ref,
                  conv_w_ref, w_out_ref, b_out_ref, g1_ref, be1_ref, w_pq_ref,
                  k1_ref, k2_ref,
                  h1_ref, h1t_ref, s1_ref, s2_ref,
                  kbuf, vbuf, ubuf, kvar, vvar, ybuf, *, tm, tiles_per_seq):
    step = pl.program_id(0)
    nblk = tm // BLOCK

    @pl.when(step % tiles_per_seq == 0)
    def _():
        kbuf[0:BLOCK, :] = jnp.zeros((BLOCK, KV_WIDTH), F32)
        vbuf[0:BLOCK, :] = jnp.zeros((BLOCK, KV_WIDTH), F32)
        ubuf[0:8, :] = jnp.zeros((8, CONV_WIDTH), F32)

    x = x_ref[...]
    xb = x.astype(BF16)

    def proj(lo, hi):
        return _dot(xb, w_in_ref[:, lo:hi]) + b_in_ref[:, lo:hi]

    rc = rc_ref[...]
    rsa = rsa_ref[...]
    rsb = rsb_ref[...]

    def rope(t):
        return (t * rc + pltpu.roll(t, LANES - ROT_DIM // 2, 1) * rsa
                + pltpu.roll(t, ROT_DIM // 2, 1) * rsb)

    kbuf[BLOCK:BLOCK + tm, :] = rope(proj(_O_K, _O_V))
    vbuf[BLOCK:BLOCK + tm, :] = proj(_O_V, _O_GB)

    lane = lax.broadcasted_iota(jnp.int32, (tm + BLOCK, LANES), 1)
    lo = lane < HEAD_DIM
    for buf, var in ((kbuf, kvar), (vbuf, vvar)):
        t = buf[...]
        tr = pltpu.roll(t, HEAD_DIM, 1)
        var[0] = jnp.where(lo, t, 0.0).astype(BF16)
        var[1] = jnp.where(lo, 0.0, tr).astype(BF16)
        var[2] = jnp.where(lo, tr, 0.0).astype(BF16)
        var[3] = jnp.where(lo, 0.0, t).astype(BF16)

    qi = lax.broadcasted_iota(jnp.int32, (BLOCK, 2 * BLOCK), 0)
    kj = lax.broadcasted_iota(jnp.int32, (BLOCK, 2 * BLOCK), 1)
    diff = qi + BLOCK - kj
    band = (diff >= 0) & (diff < BLOCK)

    for p in range(N_Q_HEADS // 2):
        q_slab = (rope(proj(p * LANES, (p + 1) * LANES)) * (HEAD_DIM ** -0.5)).astype(BF16)
        for blk in range(nblk):
            first = (step * nblk + blk) % (tiles_per_seq * nblk) == 0
            kmin = jnp.where(first, BLOCK, 0)
            mask = band & (kj >= kmin)
            qs = q_slab[blk * BLOCK:(blk + 1) * BLOCK, :]
            o = jnp.zeros((BLOCK, LANES), F32)
            for hh in range(2):
                h = 2 * p + hh
                var = (h // (N_Q_HEADS // N_KV_HEADS)) * 2 + hh
                kc = kvar[var, blk * BLOCK:(blk + 2) * BLOCK, :]
                vc = vvar[var, blk * BLOCK:(blk + 2) * BLOCK, :]
                s = jnp.where(mask, _dot_nt(qs, kc), NEG_INF)
                sink = sinks_ref[h]
                m = jnp.maximum(jnp.max(s, axis=-1, keepdims=True), sink)
                pe = jnp.exp(s - m)
                den = jnp.sum(pe, axis=-1, keepdims=True) + jnp.exp(sink - m)
                o = o + _dot((pe / den).astype(BF16), vc)
            ybuf[blk * BLOCK:(blk + 1) * BLOCK, p * LANES:(p + 1) * LANES] = o.astype(BF16)

    u = proj(_O_GC, _O_XC) * proj(_O_XC, _O_END)
    ubuf[8:8 + tm, :] = u
    um1 = ubuf[7:7 + tm, :]
    um2 = ubuf[6:6 + tm, :]
    cw = conv_w_ref[...]
    yc = proj(_O_GB, _O_GC) * (cw[0:1, :] * um2 + cw[1:2, :] * um1 + cw[2:3, :] * u)
    ybuf[:, ATTN_WIDTH:] = yc.astype(BF16)

    ubuf[0:8, :] = ubuf[tm:tm + 8, :]
    kbuf[0:BLOCK, :] = kbuf[tm:tm + BLOCK, :]
    vbuf[0:BLOCK, :] = vbuf[tm:tm + BLOCK, :]

    mix = _dot(ybuf[...], w_out_ref[...]) + b_out_ref[...]
    h1 = _layer_norm(DN_ALPHA * x + mix, g1_ref[...], be1_ref[...])
    h1_ref[...] = h1
    h1t_ref[...] = h1.T.astype(BF16)

    qp = _dot(h1.astype(BF16), w_pq_ref[...]).astype(BF16)
    k1 = k1_ref[...]
    k2 = k2_ref[...]
    for h in range(PEER_HEADS):
        base = h * PEER_QDIM
        s1_ref[h] = _dot_nt(k1, qp[:, base:base + PEER_HALF])
        s2_ref[h] = _dot_nt(k2, qp[:, base + PEER_HALF:base + PEER_QDIM])


def _front(xf, rc, rsa, rsb, sinks, w_in, b_in, conv_w, w_out, b_out, g1, be1, w_pq, k1, k2,
           *, seq, tm):
    T = xf.shape[0]
    proj_w = w_in.shape[1]
    tiles_per_seq = seq // tm
    full = lambda shape: pl.BlockSpec(shape, lambda i: (0,) * len(shape))
    tok = lambda w: pl.BlockSpec((tm, w), lambda i: (i, 0))
    kern = functools.partial(_front_kernel, tm=tm, tiles_per_seq=tiles_per_seq)
    return pl.pallas_call(
        kern,
        grid=(T // tm,),
        in_specs=[
            pl.BlockSpec(memory_space=pltpu.SMEM),
            tok(D_MODEL), tok(LANES), tok(LANES), tok(LANES),
            full((D_MODEL, proj_w)), full((1, proj_w)),
            full((CONV_K, CONV_WIDTH)), full((D_MODEL, D_MODEL)), full((1, D_MODEL)),
            full((1, D_MODEL)), full((1, D_MODEL)),
            full((D_MODEL, PEER_HEADS * PEER_QDIM)),
            full((N_KEYS, PEER_HALF)), full((N_KEYS, PEER_HALF)),
        ],
        out_specs=[
            tok(D_MODEL),
            pl.BlockSpec((D_MODEL, tm), lambda i: (0, i)),
            pl.BlockSpec((PEER_HEADS, N_KEYS, tm), lambda i: (0, 0, i)),
            pl.BlockSpec((PEER_HEADS, N_KEYS, tm), lambda i: (0, 0, i)),
        ],
        out_shape=[
            jax.ShapeDtypeStruct((T, D_MODEL), F32),
            jax.ShapeDtypeStruct((D_MODEL, T), BF16),
            jax.ShapeDtypeStruct((PEER_HEADS, N_KEYS, T), F32),
            jax.ShapeDtypeStruct((PEER_HEADS, N_KEYS, T), F32),
        ],
        scratch_shapes=[
            pltpu.VMEM((tm + BLOCK, KV_WIDTH), F32),
            pltpu.VMEM((tm + BLOCK, KV_WIDTH), F32),
            pltpu.VMEM((tm + 8, CONV_WIDTH), F32),
            pltpu.VMEM((4, tm + BLOCK, KV_WIDTH), BF16),
            pltpu.VMEM((4, tm + BLOCK, KV_WIDTH), BF16),
            pltpu.VMEM((tm, D_MODEL), BF16),
        ],
        compiler_params=pltpu.CompilerParams(
            dimension_semantics=("arbitrary",), vmem_limit_bytes=VMEM_LIMIT),
        name="front",
    )(sinks, xf, rc, rsa, rsb, w_in, b_in, conv_w, w_out, b_out, g1, be1, w_pq, k1, k2)


def _top16(s):
    rows = lax.broadcasted_iota(jnp.int32, s.shape, 0)
    rem = s
    rank = jnp.full(s.shape, float(PEER_TOPK), F32)
    vals = []
    for k in range(PEER_TOPK):
        m = jnp.max(rem, axis=0, keepdims=True)
        idx = jnp.min(jnp.where(rem == m, rows, N_KEYS), axis=0, keepdims=True)
        sel = rows == idx
        rem = jnp.where(sel, NEG_INF, rem)
        rank = jnp.where(sel, float(k), rank)
        vals.append(m)
    return vals, rank


_N_CAND_ROWS = PEER_TOPK + 7 * 8 + 8


def _cand_positions():
    import numpy as np
    pos = np.full((_N_CAND_ROWS,), -1, np.int32)
    for r in range(_N_CAND_ROWS):
        if r < 16:
            a, b = 0, r
        elif r < 72:
            a, b = 1 + (r - 16) // 8, (r - 16) % 8
        else:
            a, b = 8 + (r - 72), 0
        if (a + 1) * (b + 1) <= PEER_TOPK:
            pos[r] = a * PEER_TOPK + b
    return pos


def _select_exact(s1, s2, pos, valid, row16):
    v1, rank1 = _top16(s1)
    v2, rank2 = _top16(s2)

    v1m = jnp.zeros((PEER_TOPK, LANES), F32)
    v2m = jnp.zeros((PEER_TOPK, LANES), F32)
    for a in range(PEER_TOPK):
        v1m = jnp.where(row16 == a, v1[a], v1m)
        v2m = jnp.where(row16 == a, v2[a], v2m)
    blocks = [v1[0] + v2m]
    for a in range(1, 8):
        blocks.append(v1[a] + v2m[0:8, :])
    blocks.append(v1m[8:16, :] + v2[0])
    cand = jnp.where(valid, jnp.concatenate(blocks, axis=0), NEG_INF)

    rem = cand
    chosen = jnp.zeros(cand.shape, F32)
    for _ in range(PEER_TOPK):
        m = jnp.max(rem, axis=0, keepdims=True)
        idx = jnp.min(jnp.where(rem == m, pos, 1 << 20), axis=0, keepdims=True)
        sel = (pos == idx) & valid
        rem = jnp.where(sel, NEG_INF, rem)
        chosen = jnp.where(sel, 1.0, chosen)

    top = v1[0] + v2[0]
    z = jnp.sum(chosen * jnp.exp(jnp.where(valid, cand, top) - top), axis=0, keepdims=True)

    counts = [jnp.sum(chosen[0:16, :], axis=0, keepdims=True)]
    for a in range(1, 8):
        counts.append(jnp.sum(chosen[8 + 8 * a:16 + 8 * a, :], axis=0, keepdims=True))
    for a in range(8, 16):
        counts.append(chosen[64 + a:65 + a, :])
    n1 = jnp.zeros(s1.shape, F32)
    for a in range(PEER_TOPK):
        n1 = jnp.where(rank1 == float(a), counts[a], n1)
    return n1, jnp.exp(s1 - v1[0]), rank2, jnp.exp(s2 - v2[0]) / z


def _sort_pairs(n):
    pairs = []
    p = 1
    while p < n:
        k = p
        while k >= 1:
            for j in range(k % p, n - k, 2 * k):
                for i in range(min(k, n - j - k)):
                    if (i + j) // (2 * p) == (i + j + k) // (2 * p):
                        pairs.append((i + j, i + j + k))
            k //= 2
        p *= 2
    return pairs


def _hi_lo(a, b):
    if a is None:
        return b, None
    if b is None:
        return a, None
    return jnp.maximum(a, b), jnp.minimum(a, b)


def _sublane_all(x, op):
    for shift in (4, 2, 1):
        x = op(x, pltpu.roll(x, shift, 0))
    return x


def _top16_sorted(x):
    x = list(x)
    for i, j in _sort_pairs(len(x)):
        x[i], x[j] = _hi_lo(x[i], x[j])
    for shift in (4, 2, 1):
        y = [None if t is None else pltpu.roll(t, shift, 0) for t in x]
        c = [_hi_lo(x[r], y[PEER_TOPK - 1 - r])[0] for r in range(PEER_TOPK)]
        d = PEER_TOPK // 2
        while d >= 1:
            for i in range(PEER_TOPK):
                if not i & d:
                    c[i], c[i + d] = _hi_lo(c[i], c[i + d])
            d //= 2
        x = c
    return x


def _select_fast(s1, s2):
    nv = N_KEYS // 8
    t1 = [s1[8 * v:8 * v + 8, :] for v in range(nv)]
    t2 = [s2[8 * v:8 * v + 8, :] for v in range(nv)]
    v1 = _top16_sorted(t1)
    v2 = _top16_sorted(t2)
    sub = lax.broadcasted_iota(jnp.int32, (8, LANES), 0)
    one = jnp.ones((8, LANES), F32)
    zero = jnp.zeros((8, LANES), F32)

    def spread(vals):
        out = vals[0]
        for r in range(1, 8):
            out = jnp.where(sub == r, vals[r], out)
        return out

    v2lo, v2hi, v1hi = spread(v2[0:8]), spread(v2[8:16]), spread(v1[8:16])
    cand = [v1[0] + v2lo, v1[0] + v2hi]
    for a in range(1, 8):
        cand.append(jnp.where(sub < PEER_TOPK // (a + 1), v1[a] + v2lo, NEG_INF))
    cand.append(v1hi + v2[0])
    best = _top16_sorted(cand + [None] * (PEER_TOPK - len(cand)))
    tau = best[PEER_TOPK - 1]
    z = one
    for k in range(1, PEER_TOPK):
        z = z + jnp.exp(best[k] - best[0])

    count = lambda t: _sublane_all(jnp.where(t >= tau, one, zero), jnp.add)
    counts = [count(cand[0]) + count(cand[1])]
    counts += [count(cand[a + 1]) for a in range(1, 8)]
    counts += [jnp.where(v1[a] + v2[0] >= tau, one, zero) for a in range(8, PEER_TOPK)]
    total = counts[0]
    for a in range(1, PEER_TOPK):
        total = total + counts[a]
    tie = jnp.where(total != float(PEER_TOPK), one, zero)

    n1 = [zero] * nv
    rank2 = [jnp.full((8, LANES), float(PEER_TOPK), F32)] * nv
    for a in reversed(range(PEER_TOPK)):
        n1 = [jnp.where(t >= v1[a], counts[a], n) for t, n in zip(t1, n1)]
        rank2 = [jnp.where(t >= v2[a], float(a), r) for t, r in zip(t2, rank2)]
    for v, tiles in ((v1, t1), (v2, t2)):
        inside = zero
        for t in tiles:
            inside = inside + jnp.where(t >= v[PEER_TOPK - 1], one, zero)
        tie = jnp.where(_sublane_all(inside, jnp.add) != float(PEER_TOPK), one, tie)
        for a in range(PEER_TOPK - 1):
            tie = jnp.where(v[a] == v[a + 1], one, tie)

    inv_z = 1.0 / z
    cat = lambda tiles: jnp.concatenate(tiles, axis=0)
    e1 = cat([jnp.exp(t - v1[0]) for t in t1])
    e2 = cat([jnp.exp(t - v2[0]) * inv_z for t in t2])
    return cat(n1), e1, cat(rank2), e2, tie


def _select_kernel(pos_ref, s1_ref, s2_ref, r2_ref, e2_ref, n1_ref, e1_ref, *, ts):
    def group(g, carry):
        sl = pl.ds(pl.multiple_of(g * LANES, LANES), LANES)
        s1 = s1_ref[0, :, sl]
        s2 = s2_ref[0, :, sl]
        n1, e1, rank2, e2, tie = _select_fast(s1, s2)

        def exact():
            pos = pos_ref[...]
            row16 = lax.broadcasted_iota(jnp.int32, (PEER_TOPK, LANES), 0)
            return _select_exact(s1, s2, pos, pos >= 0, row16)

        n1, e1, rank2, e2 = lax.cond(jnp.max(tie) > 0.0, exact, lambda: (n1, e1, rank2, e2))
        n1_ref[:, 0, :, sl] = n1.reshape(N_KEYS // 8, 8, LANES)
        e1_ref[:, 0, :, sl] = e1.reshape(N_KEYS // 8, 8, LANES)
        r2_ref[0, :, sl] = rank2.astype(BF16)
        e2_ref[0, :, sl] = e2.astype(BF16)
        return carry

    lax.fori_loop(0, ts // LANES, group, 0)


def _select(s1t, s2t, *, ts):
    H, K, T = s1t.shape
    import numpy as np
    pos = jnp.asarray(np.tile(_cand_positions()[:, None], (1, LANES)))
    blk = pl.BlockSpec((1, K, ts), lambda t, h: (h, 0, t))
    blk1 = pl.BlockSpec((K // 8, 1, 8, ts), lambda t, h: (0, h, 0, t))
    return pl.pallas_call(
        functools.partial(_select_kernel, ts=ts),
        grid=(T // ts, H),
        in_specs=[pl.BlockSpec((_N_CAND_ROWS, LANES), lambda t, h: (0, 0)), blk, blk],
        out_specs=[blk, blk, blk1, blk1],
        out_shape=[
            jax.ShapeDtypeStruct((H, K, T), BF16),
            jax.ShapeDtypeStruct((H, K, T), BF16),
            jax.ShapeDtypeStruct((K // 8, H, 8, T), F32),
            jax.ShapeDtypeStruct((K // 8, H, 8, T), F32),
        ],
        compiler_params=pltpu.CompilerParams(
            dimension_semantics=("arbitrary", "arbitrary"), vmem_limit_bytes=VMEM_LIMIT),
        name="select",
    )(pos, s1t, s2t)


def _peer_kernel(u_ref, vt_ref, xt_ref, r2_ref, e2_ref, n1_ref, e1_ref, h1_ref, g2_ref, be2_ref,
                 out_ref, acc_ref, w_ref, a_ref, *, ec, tt):
    c = pl.program_id(1)
    nkb = ec // N_KEYS
    ngrp = nkb // 8
    assert ngrp * 8 == nkb

    @pl.when(c == 0)
    def _():
        acc_ref[...] = jnp.zeros(acc_ref.shape, F32)

    def row_bf16(ref, h, ib, sl):
        tiles = [jnp.broadcast_to(
            ref[c * ngrp + ib // 8, h, ib % 8:ib % 8 + 1, sl.start + k * LANES:sl.start + (k + 1) * LANES],
            (BF16_ROWS, LANES)) for k in range(BF16_COLS // LANES)]
        return jnp.concatenate(tiles, axis=1).astype(BF16)

    rows = BF16_ROWS
    a_ref[...] = _dot(u_ref[...], xt_ref[...])
    for nt in range(tt // BF16_COLS):
        nsl = slice(nt * BF16_COLS, (nt + 1) * BF16_COLS)
        for ib in range(nkb):
            n1b = [row_bf16(n1_ref, h, ib, nsl) for h in range(PEER_HEADS)]
            e1b = [row_bf16(e1_ref, h, ib, nsl) for h in range(PEER_HEADS)]
            for rg in range(N_KEYS // rows):
                ksl = slice(rg * rows, (rg + 1) * rows)
                g = None
                for h in range(PEER_HEADS):
                    e2 = e2_ref[h, ksl, nsl]
                    term = jnp.where(r2_ref[h, ksl, nsl] < n1b[h], e2, jnp.zeros_like(e2)) * e1b[h]
                    g = term if g is None else g + term
                esl = slice(ib * N_KEYS + rg * rows, ib * N_KEYS + (rg + 1) * rows)
                ab = a_ref[esl, nsl]
                ge = 0.5 * ab * (1.0 + lax.erf(ab * (2.0 ** -0.5)))
                w_ref[esl, nsl] = ge.astype(BF16) * g
    acc_ref[...] += _dot(vt_ref[...], w_ref[...])

    @pl.when(c == pl.num_programs(1) - 1)
    def _():
        z = DN_ALPHA * h1_ref[...] + acc_ref[...].T
        out_ref[...] = _layer_norm(z, g2_ref[...], be2_ref[...])


def _peer(u_bf, vt_bf, h1t, r2, e2, n1, e1, h1, g2, be2, *, ec, tt):
    E = u_bf.shape[0]
    T = h1.shape[0]
    tokrows = pl.BlockSpec((PEER_HEADS, N_KEYS, tt), lambda t, c: (0, 0, t))
    tokkeys = pl.BlockSpec((N_KEYS // 8, PEER_HEADS, 8, tt), lambda t, c: (0, 0, 0, t))
    vec = pl.BlockSpec((1, D_MODEL), lambda t, c: (0, 0))
    return pl.pallas_call(
        functools.partial(_peer_kernel, ec=ec, tt=tt),
        grid=(T // tt, E // ec),
        in_specs=[
            pl.BlockSpec((ec, D_MODEL), lambda t, c: (c, 0)),
            pl.BlockSpec((D_MODEL, ec), lambda t, c: (0, c)),
            pl.BlockSpec((D_MODEL, tt), lambda t, c: (0, t)),
            tokrows, tokrows, tokkeys, tokkeys,
            pl.BlockSpec((tt, D_MODEL), lambda t, c: (t, 0)),
            vec, vec,
        ],
        out_specs=pl.BlockSpec((tt, D_MODEL), lambda t, c: (t, 0)),
        out_shape=jax.ShapeDtypeStruct((T, D_MODEL), F32),
        scratch_shapes=[
            pltpu.VMEM((D_MODEL, tt), F32),
            pltpu.VMEM((ec, tt), BF16),
            pltpu.VMEM((ec, tt), F32),
        ],
        compiler_params=pltpu.CompilerParams(
            dimension_semantics=("arbitrary", "arbitrary"), vmem_limit_bytes=VMEM_LIMIT),
        name="peer",
    )(u_bf, vt_bf, h1t, r2, e2, n1, e1, h1, g2, be2)


def _rope_tables(positions):
    T = positions.size
    half = ROT_DIM // 2
    inv_freq = ROPE_THETA ** (-jnp.arange(0, ROT_DIM, 2, dtype=F32) / ROT_DIM)
    ang = positions.reshape(T, 1).astype(F32) * inv_freq
    cos, sin = jnp.cos(ang), jnp.sin(ang)
    zeros = lambda w: jnp.zeros((T, w), F32)
    rc = jnp.concatenate([cos, cos, jnp.ones((T, HEAD_DIM - ROT_DIM), F32)], axis=1)
    rsa = jnp.concatenate([-sin, zeros(HEAD_DIM - half)], axis=1)
    rsb = jnp.concatenate([zeros(half), sin, zeros(HEAD_DIM - ROT_DIM)], axis=1)
    rep = LANES // HEAD_DIM
    return jnp.tile(rc, (1, rep)), jnp.tile(rsa, (1, rep)), jnp.tile(rsb, (1, rep))


def _tiles(seq, total_tokens):
    tm = next(c for c in (512, 256, BLOCK) if seq % c == 0)
    ts = next(c for c in (1024, 512, LANES) if total_tokens % c == 0)
    tt = 512 if total_tokens % 512 == 0 else 256
    ec = 16 * N_KEYS
    return tm, ts, tt, ec


def kernel(x, positions, w_in, b_in, attn_sinks, conv_w, w_out, b_out, ln1_g, ln1_b,
           w_pq, sub_keys1, sub_keys2, u_experts, v_experts, ln2_g, ln2_b):
    B, S, D = x.shape
    T = B * S
    assert D == D_MODEL and S % BLOCK == 0 and T % 256 == 0
    assert w_in.shape[0] == DEPTH
    tm, ts, tt, ec = _tiles(S, T)
    row = lambda v: v.reshape(1, -1).astype(F32)

    rc, rsa, rsb = _rope_tables(positions)
    h1, h1t, s1t, s2t = _front(
        x.reshape(T, D), rc, rsa, rsb, attn_sinks[0].astype(F32),
        w_in[0].astype(BF16), row(b_in[0]), conv_w[0].astype(F32),
        w_out[0].astype(BF16), row(b_out[0]), row(ln1_g[0]), row(ln1_b[0]),
        w_pq[0].astype(BF16), sub_keys1[0].astype(BF16), sub_keys2[0].astype(BF16),
        seq=S, tm=tm)
    r2, e2, n1, e1 = _select(s1t, s2t, ts=ts)
    out = _peer(u_experts[0].astype(BF16), v_experts[0].T.astype(BF16), h1t,
                r2, e2, n1, e1, h1, row(ln2_g[0]), row(ln2_b[0]), ec=ec, tt=tt)
    return out.reshape(B, S, D)
```

```python
import functools

import jax
import jax.numpy as jnp
from jax import lax
from jax.experimental import pallas as pl
from jax.experimental.pallas import tpu as pltpu

D_MODEL = 1024
HEAD_DIM = 64
N_Q_HEADS = 8
N_KV_HEADS = 2
ATTN_WIDTH = N_Q_HEADS * HEAD_DIM
KV_WIDTH = N_KV_HEADS * HEAD_DIM
BLOCK = 128
ROT_DIM = HEAD_DIM // 4
ROPE_THETA = 500000.0
CONV_WIDTH = D_MODEL - ATTN_WIDTH
CONV_K = 3
N_KEYS = 128
PEER_HEADS = 8
PEER_QDIM = 256
PEER_HALF = PEER_QDIM // 2
PEER_TOPK = 16
DEPTH = 1
DN_ALPHA = (2.0 * DEPTH) ** 0.25
LN_EPS = 1e-5

LANES = 128
BF16_ROWS = 8
BF16_COLS = 256
SELECT_UNROLL = 2
VMEM_LIMIT = 56 * 1024 * 1024

_O_K = ATTN_WIDTH
_O_V = _O_K + KV_WIDTH
_O_GB = _O_V + KV_WIDTH
_O_GC = _O_GB + CONV_WIDTH
_O_XC = _O_GC + CONV_WIDTH
_O_END = _O_XC + CONV_WIDTH

BF16 = jnp.bfloat16
F32 = jnp.float32
NEG_INF = float("-inf")


def _dot(a, b):
    return jnp.dot(a, b, preferred_element_type=F32)


def _dot_nt(a, b):
    return lax.dot_general(a, b, (((1,), (1,)), ((), ())), preferred_element_type=F32)


def _layer_norm(z, g, b):
    mu = jnp.mean(z, axis=-1, keepdims=True)
    zc = z - mu
    var = jnp.mean(zc * zc, axis=-1, keepdims=True)
    return zc * lax.rsqrt(var + LN_EPS) * g + b


def _front_kernel(sinks_ref, x_ref, rc_ref, rsa_ref, rsb_ref, w_in_ref, b_in_ref,
                  conv_w_ref, w_out_ref, b_out_ref, g1_ref, be1_ref, w_pq_ref,
                  k1_ref, k2_ref,
                  h1_ref, h1t_ref, s1_ref, s2_ref,
                  kbuf, vbuf, ubuf, kvar, vvar, ybuf, *, tm, tiles_per_seq):
    step = pl.program_id(0)
    nblk = tm // BLOCK

    @pl.when(step % tiles_per_seq == 0)
    def _():
        kbuf[0:BLOCK, :] = jnp.zeros((BLOCK, KV_WIDTH), F32)
        vbuf[0:BLOCK, :] = jnp.zeros((BLOCK, KV_WIDTH), F32)
        ubuf[0:8, :] = jnp.zeros((8, CONV_WIDTH), F32)

    x = x_ref[...]
    xb = x.astype(BF16)

    def proj(lo, hi):
        return _dot(xb, w_in_ref[:, lo:hi]) + b_in_ref[:, lo:hi]

    rc = rc_ref[...]
    rsa = rsa_ref[...]
    rsb = rsb_ref[...]

    def rope(t):
        return (t * rc + pltpu.roll(t, LANES - ROT_DIM // 2, 1) * rsa
                + pltpu.roll(t, ROT_DIM // 2, 1) * rsb)

    kbuf[BLOCK:BLOCK + tm, :] = rope(proj(_O_K, _O_V))
    vbuf[BLOCK:BLOCK + tm, :] = proj(_O_V, _O_GB)

    lane = lax.broadcasted_iota(jnp.int32, (tm + BLOCK, LANES), 1)
    lo = lane < HEAD_DIM
    for buf, var in ((kbuf, kvar), (vbuf, vvar)):
        t = buf[...]
        tr = pltpu.roll(t, HEAD_DIM, 1)
        var[0] = jnp.where(lo, t, 0.0).astype(BF16)
        var[1] = jnp.where(lo, 0.0, tr).astype(BF16)
        var[2] = jnp.where(lo, tr, 0.0).astype(BF16)
        var[3] = jnp.where(lo, 0.0, t).astype(BF16)

    qi = lax.broadcasted_iota(jnp.int32, (BLOCK, 2 * BLOCK), 0)
    kj = lax.broadcasted_iota(jnp.int32, (BLOCK, 2 * BLOCK), 1)
    diff = qi + BLOCK - kj
    band = (diff >= 0) & (diff < BLOCK)

    for p in range(N_Q_HEADS // 2):
        q_slab = (rope(proj(p * LANES, (p + 1) * LANES)) * (HEAD_DIM ** -0.5)).astype(BF16)
        for blk in range(nblk):
            first = (step * nblk + blk) % (tiles_per_seq * nblk) == 0
            kmin = jnp.where(first, BLOCK, 0)
            mask = band & (kj >= kmin)
            qs = q_slab[blk * BLOCK:(blk + 1) * BLOCK, :]
            o = jnp.zeros((BLOCK, LANES), F32)
            for hh in range(2):
                h = 2 * p + hh
                var = (h // (N_Q_HEADS // N_KV_HEADS)) * 2 + hh
                kc = kvar[var, blk * BLOCK:(blk + 2) * BLOCK, :]
                vc = vvar[var, blk * BLOCK:(blk + 2) * BLOCK, :]
                s = jnp.where(mask, _dot_nt(qs, kc), NEG_INF)
                sink = sinks_ref[h]
                m = jnp.maximum(jnp.max(s, axis=-1, keepdims=True), sink)
                pe = jnp.exp(s - m)
                den = jnp.sum(pe, axis=-1, keepdims=True) + jnp.exp(sink - m)
                o = o + _dot((pe / den).astype(BF16), vc)
            ybuf[blk * BLOCK:(blk + 1) * BLOCK, p * LANES:(p + 1) * LANES] = o.astype(BF16)

    u = proj(_O_GC, _O_XC) * proj(_O_XC, _O_END)
    ubuf[8:8 + tm, :] = u
    um1 = ubuf[7:7 + tm, :]
    um2 = ubuf[6:6 + tm, :]
    cw = conv_w_ref[...]
    yc = proj(_O_GB, _O_GC) * (cw[0:1, :] * um2 + cw[1:2, :] * um1 + cw[2:3, :] * u)
    ybuf[:, ATTN_WIDTH:] = yc.astype(BF16)

    ubuf[0:8, :] = ubuf[tm:tm + 8, :]
    kbuf[0:BLOCK, :] = kbuf[tm:tm + BLOCK, :]
    vbuf[0:BLOCK, :] = vbuf[tm:tm + BLOCK, :]

    mix = _dot(ybuf[...], w_out_ref[...]) + b_out_ref[...]
    h1 = _layer_norm(DN_ALPHA * x + mix, g1_ref[...], be1_ref[...])
    h1_ref[...] = h1
    h1t_ref[...] = h1.T.astype(BF16)

    qp = _dot(h1.astype(BF16), w_pq_ref[...]).astype(BF16)
    k1 = k1_ref[...]
    k2 = k2_ref[...]
    for h in range(PEER_HEADS):
        base = h * PEER_QDIM
        s1_ref[h] = _dot_nt(k1, qp[:, base:base + PEER_HALF])
        s2_ref[h] = _dot_nt(k2, qp[:, base + PEER_HALF:base + PEER_QDIM])


def _front(xf, rc, rsa, rsb, sinks, w_in, b_in, conv_w, w_out, b_out, g1, be1, w_pq, k1, k2,
           *, seq, tm):
    T = xf.shape[0]
    proj_w = w_in.shape[1]
    tiles_per_seq = seq // tm
    full = lambda shape: pl.BlockSpec(shape, lambda i: (0,) * len(shape))
    tok = lambda w: pl.BlockSpec((tm, w), lambda i: (i, 0))
    kern = functools.partial(_front_kernel, tm=tm, tiles_per_seq=tiles_per_seq)
    return pl.pallas_call(
        kern,
        grid=(T // tm,),
        in_specs=[
            pl.BlockSpec(memory_space=pltpu.SMEM),
            tok(D_MODEL), tok(LANES), tok(LANES), tok(LANES),
            full((D_MODEL, proj_w)), full((1, proj_w)),
            full((CONV_K, CONV_WIDTH)), full((D_MODEL, D_MODEL)), full((1, D_MODEL)),
            full((1, D_MODEL)), full((1, D_MODEL)),
            full((D_MODEL, PEER_HEADS * PEER_QDIM)),
            full((N_KEYS, PEER_HALF)), full((N_KEYS, PEER_HALF)),
        ],
        out_specs=[
            tok(D_MODEL),
            pl.BlockSpec((D_MODEL, tm), lambda i: (0, i)),
            pl.BlockSpec((PEER_HEADS, N_KEYS, tm), lambda i: (0, 0, i)),
            pl.BlockSpec((PEER_HEADS, N_KEYS, tm), lambda i: (0, 0, i)),
        ],
        out_shape=[
            jax.ShapeDtypeStruct((T, D_MODEL), F32),
            jax.ShapeDtypeStruct((D_MODEL, T), BF16),
            jax.ShapeDtypeStruct((PEER_HEADS, N_KEYS, T), F32),
            jax.ShapeDtypeStruct((PEER_HEADS, N_KEYS, T), F32),
        ],
        scratch_shapes=[
            pltpu.VMEM((tm + BLOCK, KV_WIDTH), F32),
            pltpu.VMEM((tm + BLOCK, KV_WIDTH), F32),
            pltpu.VMEM((tm + 8, CONV_WIDTH), F32),
            pltpu.VMEM((4, tm + BLOCK, KV_WIDTH), BF16),
            pltpu.VMEM((4, tm + BLOCK, KV_WIDTH), BF16),
            pltpu.VMEM((tm, D_MODEL), BF16),
        ],
        compiler_params=pltpu.CompilerParams(
            dimension_semantics=("arbitrary",), vmem_limit_bytes=VMEM_LIMIT),
        name="front",
    )(sinks, xf, rc, rsa, rsb, w_in, b_in, conv_w, w_out, b_out, g1, be1, w_pq, k1, k2)


def _top16(s):
    rows = lax.broadcasted_iota(jnp.int32, s.shape, 0)
    rem = s
    rank = jnp.full(s.shape, float(PEER_TOPK), F32)
    vals = []
    for k in range(PEER_TOPK):
        m = jnp.max(rem, axis=0, keepdims=True)
        idx = jnp.min(jnp.where(rem == m, rows, N_KEYS), axis=0, keepdims=True)
        sel = rows == idx
        rem = jnp.where(sel, NEG_INF, rem)
        rank = jnp.where(sel, float(k), rank)
        vals.append(m)
    return vals, rank


_N_CAND_ROWS = PEER_TOPK + 7 * 8 + 8


def _cand_positions():
    import numpy as np
    pos = np.full((_N_CAND_ROWS,), -1, np.int32)
    for r in range(_N_CAND_ROWS):
        if r < 16:
            a, b = 0, r
        elif r < 72:
            a, b = 1 + (r - 16) // 8, (r - 16) % 8
        else:
            a, b = 8 + (r - 72), 0
        if (a + 1) * (b + 1) <= PEER_TOPK:
            pos[r] = a * PEER_TOPK + b
    return pos


def _select_exact(s1, s2, pos, valid, row16):
    v1, rank1 = _top16(s1)
    v2, rank2 = _top16(s2)

    v1m = jnp.zeros((PEER_TOPK, LANES), F32)
    v2m = jnp.zeros((PEER_TOPK, LANES), F32)
    for a in range(PEER_TOPK):
        v1m = jnp.where(row16 == a, v1[a], v1m)
        v2m = jnp.where(row16 == a, v2[a], v2m)
    blocks = [v1[0] + v2m]
    for a in range(1, 8):
        blocks.append(v1[a] + v2m[0:8, :])
    blocks.append(v1m[8:16, :] + v2[0])
    cand = jnp.where(valid, jnp.concatenate(blocks, axis=0), NEG_INF)

    rem = cand
    chosen = jnp.zeros(cand.shape, F32)
    for _ in range(PEER_TOPK):
        m = jnp.max(rem, axis=0, keepdims=True)
        idx = jnp.min(jnp.where(rem == m, pos, 1 << 20), axis=0, keepdims=True)
        sel = (pos == idx) & valid
        rem = jnp.where(sel, NEG_INF, rem)
        chosen = jnp.where(sel, 1.0, chosen)

    top = v1[0] + v2[0]
    z = jnp.sum(chosen * jnp.exp(jnp.where(valid, cand, top) - top), axis=0, keepdims=True)

    counts = [jnp.sum(chosen[0:16, :], axis=0, keepdims=True)]
    for a in range(1, 8):
        counts.append(jnp.sum(chosen[8 + 8 * a:16 + 8 * a, :], axis=0, keepdims=True))
    for a in range(8, 16):
        counts.append(chosen[64 + a:65 + a, :])
    n1 = jnp.zeros(s1.shape, F32)
    for a in range(PEER_TOPK):
        n1 = jnp.where(rank1 == float(a), counts[a], n1)
    return n1, jnp.exp(s1 - v1[0]), rank2, jnp.exp(s2 - v2[0]) / z


def _sort_pairs(n):
    pairs = []
    p = 1
    while p < n:
        k = p
        while k >= 1:
            for j in range(k % p, n - k, 2 * k):
                for i in range(min(k, n - j - k)):
                    if (i + j) // (2 * p) == (i + j + k) // (2 * p):
                        pairs.append((i + j, i + j + k))
            k //= 2
        p *= 2
    return pairs


def _hi_lo(a, b):
    if a is None:
        return b, None
    if b is None:
        return a, None
    return jnp.maximum(a, b), jnp.minimum(a, b)


def _sublane_all(x, op):
    for shift in (4, 2, 1):
        x = op(x, pltpu.roll(x, shift, 0))
    return x


def _top16_sorted(x):
    x = list(x)
    for i, j in _sort_pairs(len(x)):
        x[i], x[j] = _hi_lo(x[i], x[j])
    for shift in (4, 2, 1):
        y = [None if t is None else pltpu.roll(t, shift, 0) for t in x]
        c = [_hi_lo(x[r], y[PEER_TOPK - 1 - r])[0] for r in range(PEER_TOPK)]
        d = PEER_TOPK // 2
        while d >= 1:
            for i in range(PEER_TOPK):
                if not i & d:
                    c[i], c[i + d] = _hi_lo(c[i], c[i + d])
            d //= 2
        x = c
    return x


def _select_fast(s1, s2):
    nv = N_KEYS // 8
    t1 = [s1[8 * v:8 * v + 8, :] for v in range(nv)]
    t2 = [s2[8 * v:8 * v + 8, :] for v in range(nv)]
    v1 = _top16_sorted(t1)
    v2 = _top16_sorted(t2)
    sub = lax.broadcasted_iota(jnp.int32, (8, LANES), 0)
    one = jnp.ones((8, LANES), F32)
    zero = jnp.zeros((8, LANES), F32)

    def spread(vals):
        out = vals[0]
        for r in range(1, 8):
            out = jnp.where(sub == r, vals[r], out)
        return out

    v2lo, v2hi, v1hi = spread(v2[0:8]), spread(v2[8:16]), spread(v1[8:16])
    cand = [v1[0] + v2lo, v1[0] + v2hi]
    for a in range(1, 8):
        cand.append(jnp.where(sub < PEER_TOPK // (a + 1), v1[a] + v2lo, NEG_INF))
    cand.append(v1hi + v2[0])
    best = _top16_sorted(cand + [None] * (PEER_TOPK - len(cand)))
    tau = best[PEER_TOPK - 1]
    z = one
    for k in range(1, PEER_TOPK):
        z = z + jnp.exp(best[k] - best[0])

    count = lambda t: _sublane_all(jnp.where(t >= tau, one, zero), jnp.add)
    counts = [count(cand[0]) + count(cand[1])]
    counts += [count(cand[a + 1]) for a in range(1, 8)]
    counts += [jnp.where(v1[a] + v2[0] >= tau, one, zero) for a in range(8, PEER_TOPK)]
    total = counts[0]
    for a in range(1, PEER_TOPK):
        total = total + counts[a]
    tie = jnp.where(total != float(PEER_TOPK), one, zero)

    n1 = [zero] * nv
    rank2 = [jnp.full((8, LANES), float(PEER_TOPK), F32)] * nv
    for a in reversed(range(PEER_TOPK)):
        n1 = [jnp.where(t >= v1[a], counts[a], n) for t, n in zip(t1, n1)]
        rank2 = [jnp.where(t >= v2[a], float(a), r) for t, r in zip(t2, rank2)]
    for v, tiles in ((v1, t1), (v2, t2)):
        inside = zero
        for t in tiles:
            inside = inside + jnp.where(t >= v[PEER_TOPK - 1], one, zero)
        tie = jnp.where(_sublane_all(inside, jnp.add) != float(PEER_TOPK), one, tie)
        for a in range(PEER_TOPK - 1):
            tie = jnp.where(v[a] == v[a + 1], one, tie)

    inv_z = 1.0 / z
    cat = lambda tiles: jnp.concatenate(tiles, axis=0)
    e1 = cat([jnp.exp(t - v1[0]) for t in t1])
    e2 = cat([jnp.exp(t - v2[0]) * inv_z for t in t2])
    return cat(n1), e1, cat(rank2), e2, tie


def _select_kernel(pos_ref, s1_ref, s2_ref, r2_ref, e2_ref, n1_ref, e1_ref, *, ts):
    def lanes(g):
        return pl.ds(pl.multiple_of(g * LANES, LANES), LANES)

    def finish(sl, fast):
        n1, e1, rank2, e2, tie = fast

        def exact():
            pos = pos_ref[...]
            row16 = lax.broadcasted_iota(jnp.int32, (PEER_TOPK, LANES), 0)
            return _select_exact(s1_ref[0, :, sl], s2_ref[0, :, sl], pos, pos >= 0, row16)

        n1, e1, rank2, e2 = lax.cond(jnp.max(tie) > 0.0, exact, lambda: (n1, e1, rank2, e2))
        n1_ref[:, 0, :, sl] = n1.reshape(N_KEYS // 8, 8, LANES)
        e1_ref[:, 0, :, sl] = e1.reshape(N_KEYS // 8, 8, LANES)
        r2_ref[0, :, sl] = rank2.astype(BF16)
        e2_ref[0, :, sl] = e2.astype(BF16)

    def trip(i, carry):
        sls = [lanes(SELECT_UNROLL * i + k) for k in range(SELECT_UNROLL)]
        fasts = [_select_fast(s1_ref[0, :, sl], s2_ref[0, :, sl]) for sl in sls]
        for sl, fast in zip(sls, fasts):
            finish(sl, fast)
        return carry

    assert (ts // LANES) % SELECT_UNROLL == 0
    lax.fori_loop(0, ts // LANES // SELECT_UNROLL, trip, 0)


def _select(s1t, s2t, *, ts):
    H, K, T = s1t.shape
    import numpy as np
    pos = jnp.asarray(np.tile(_cand_positions()[:, None], (1, LANES)))
    blk = pl.BlockSpec((1, K, ts), lambda t, h: (h, 0, t))
    blk1 = pl.BlockSpec((K // 8, 1, 8, ts), lambda t, h: (0, h, 0, t))
    return pl.pallas_call(
        functools.partial(_select_kernel, ts=ts),
        grid=(T // ts, H),
        in_specs=[pl.BlockSpec((_N_CAND_ROWS, LANES), lambda t, h: (0, 0)), blk, blk],
        out_specs=[blk, blk, blk1, blk1],
        out_shape=[
            jax.ShapeDtypeStruct((H, K, T), BF16),
            jax.ShapeDtypeStruct((H, K, T), BF16),
            jax.ShapeDtypeStruct((K // 8, H, 8, T), F32),
            jax.ShapeDtypeStruct((K // 8, H, 8, T), F32),
        ],
        compiler_params=pltpu.CompilerParams(
            dimension_semantics=("arbitrary", "arbitrary"), vmem_limit_bytes=VMEM_LIMIT),
        name="select",
    )(pos, s1t, s2t)


def _peer_kernel(u_ref, vt_ref, xt_ref, r2_ref, e2_ref, n1_ref, e1_ref, h1_ref, g2_ref, be2_ref,
                 out_ref, acc_ref, w_ref, a_ref, *, ec, tt):
    c = pl.program_id(1)
    nkb = ec // N_KEYS
    ngrp = nkb // 8
    assert ngrp * 8 == nkb

    @pl.when(c == 0)
    def _():
        acc_ref[...] = jnp.zeros(acc_ref.shape, F32)

    def row_bf16(ref, h, ib, sl):
        tiles = [jnp.broadcast_to(
            ref[c * ngrp + ib // 8, h, ib % 8:ib % 8 + 1, sl.start + k * LANES:sl.start + (k + 1) * LANES],
            (BF16_ROWS, LANES)) for k in range(BF16_COLS // LANES)]
        return jnp.concatenate(tiles, axis=1).astype(BF16)

    rows = BF16_ROWS
    a_ref[...] = _dot(u_ref[...], xt_ref[...])
    for nt in range(tt // BF16_COLS):
        nsl = slice(nt * BF16_COLS, (nt + 1) * BF16_COLS)
        for ib in range(nkb):
            n1b = [row_bf16(n1_ref, h, ib, nsl) for h in range(PEER_HEADS)]
            e1b = [row_bf16(e1_ref, h, ib, nsl) for h in range(PEER_HEADS)]
            for rg in range(N_KEYS // rows):
                ksl = slice(rg * rows, (rg + 1) * rows)
                g = None
                for h in range(PEER_HEADS):
                    e2 = e2_ref[h, ksl, nsl]
                    term = jnp.where(r2_ref[h, ksl, nsl] < n1b[h], e2, jnp.zeros_like(e2)) * e1b[h]
                    g = term if g is None else g + term
                esl = slice(ib * N_KEYS + rg * rows, ib * N_KEYS + (rg + 1) * rows)
                ab = a_ref[esl, nsl]
                ge = 0.5 * ab * (1.0 + lax.erf(ab * (2.0 ** -0.5)))
                w_ref[esl, nsl] = ge.astype(BF16) * g
    acc_ref[...] += _dot(vt_ref[...], w_ref[...])

    @pl.when(c == pl.num_programs(1) - 1)
    def _():
        z = DN_ALPHA * h1_ref[...] + acc_ref[...].T
        out_ref[...] = _layer_norm(z, g2_ref[...], be2_ref[...])


def _peer(u_bf, vt_bf, h1t, r2, e2, n1, e1, h1, g2, be2, *, ec, tt):
    E = u_bf.shape[0]
    T = h1.shape[0]
    tokrows = pl.BlockSpec((PEER_HEADS, N_KEYS, tt), lambda t, c: (0, 0, t))
    tokkeys = pl.BlockSpec((N_KEYS // 8, PEER_HEADS, 8, tt), lambda t, c: (0, 0, 0, t))
    vec = pl.BlockSpec((1, D_MODEL), lambda t, c: (0, 0))
    return pl.pallas_call(
        functools.partial(_peer_kernel, ec=ec, tt=tt),
        grid=(T // tt, E // ec),
        in_specs=[
            pl.BlockSpec((ec, D_MODEL), lambda t, c: (c, 0)),
            pl.BlockSpec((D_MODEL, ec), lambda t, c: (0, c)),
            pl.BlockSpec((D_MODEL, tt), lambda t, c: (0, t)),
            tokrows, tokrows, tokkeys, tokkeys,
            pl.BlockSpec((tt, D_MODEL), lambda t, c: (t, 0)),
            vec, vec,
        ],
        out_specs=pl.BlockSpec((tt, D_MODEL), lambda t, c: (t, 0)),
        out_shape=jax.ShapeDtypeStruct((T, D_MODEL), F32),
        scratch_shapes=[
            pltpu.VMEM((D_MODEL, tt), F32),
            pltpu.VMEM((ec, tt), BF16),
            pltpu.VMEM((ec, tt), F32),
        ],
        compiler_params=pltpu.CompilerParams(
            dimension_semantics=("arbitrary", "arbitrary"), vmem_limit_bytes=VMEM_LIMIT),
        name="peer",
    )(u_bf, vt_bf, h1t, r2, e2, n1, e1, h1, g2, be2)


def _rope_tables(positions):
    T = positions.size
    half = ROT_DIM // 2
    inv_freq = ROPE_THETA ** (-jnp.arange(0, ROT_DIM, 2, dtype=F32) / ROT_DIM)
    ang = positions.reshape(T, 1).astype(F32) * inv_freq
    cos, sin = jnp.cos(ang), jnp.sin(ang)
    zeros = lambda w: jnp.zeros((T, w), F32)
    rc = jnp.concatenate([cos, cos, jnp.ones((T, HEAD_DIM - ROT_DIM), F32)], axis=1)
    rsa = jnp.concatenate([-sin, zeros(HEAD_DIM - half)], axis=1)
    rsb = jnp.concatenate([zeros(half), sin, zeros(HEAD_DIM - ROT_DIM)], axis=1)
    rep = LANES // HEAD_DIM
    return jnp.tile(rc, (1, rep)), jnp.tile(rsa, (1, rep)), jnp.tile(rsb, (1, rep))


def _tiles(seq, total_tokens):
    tm = next(c for c in (512, 256, BLOCK) if seq % c == 0)
    ts = next(c for c in (1024, 512, LANES) if total_tokens % c == 0)
    tt = 512 if total_tokens % 512 == 0 else 256
    ec = 16 * N_KEYS
    return tm, ts, tt, ec


def kernel(x, positions, w_in, b_in, attn_sinks, conv_w, w_out, b_out, ln1_g, ln1_b,
           w_pq, sub_keys1, sub_keys2, u_experts, v_experts, ln2_g, ln2_b):
    B, S, D = x.shape
    T = B * S
    assert D == D_MODEL and S % BLOCK == 0 and T % 256 == 0
    assert w_in.shape[0] == DEPTH
    tm, ts, tt, ec = _tiles(S, T)
    row = lambda v: v.reshape(1, -1).astype(F32)

    rc, rsa, rsb = _rope_tables(positions)
    h1, h1t, s1t, s2t = _front(
        x.reshape(T, D), rc, rsa, rsb, attn_sinks[0].astype(F32),
        w_in[0].astype(BF16), row(b_in[0]), conv_w[0].astype(F32),
        w_out[0].astype(BF16), row(b_out[0]), row(ln1_g[0]), row(ln1_b[0]),
        w_pq[0].astype(BF16), sub_keys1[0].astype(BF16), sub_keys2[0].astype(BF16),
        seq=S, tm=tm)
    r2, e2, n1, e1 = _select(s1t, s2t, ts=ts)
    out = _peer(u_experts[0].astype(BF16), v_experts[0].T.astype(BF16), h1t,
                r2, e2, n1, e1, h1, row(ln2_g[0]), row(ln2_b[0]), ec=ec, tt=tt)
    return out.reshape(B, S, D)
```

```python
import functools

import jax
import jax.numpy as jnp
from jax import lax
from jax.experimental import pallas as pl
from jax.experimental.pallas import tpu as pltpu

D_MODEL = 1024
HEAD_DIM = 64
N_Q_HEADS = 8
N_KV_HEADS = 2
ATTN_WIDTH = N_Q_HEADS * HEAD_DIM
KV_WIDTH = N_KV_HEADS * HEAD_DIM
BLOCK = 128
ROT_DIM = HEAD_DIM // 4
ROPE_THETA = 500000.0
CONV_WIDTH = D_MODEL - ATTN_WIDTH
CONV_K = 3
N_KEYS = 128
PEER_HEADS = 8
PEER_QDIM = 256
PEER_HALF = PEER_QDIM // 2
PEER_TOPK = 16
DEPTH = 1
DN_ALPHA = (2.0 * DEPTH) ** 0.25
LN_EPS = 1e-5

LANES = 128
BF16_ROWS = 8
BF16_COLS = 256
SELECT_UNROLL = 2
GELU_HALF = 0.5
VMEM_LIMIT = 56 * 1024 * 1024

_O_K = ATTN_WIDTH
_O_V = _O_K + KV_WIDTH
_O_GB = _O_V + KV_WIDTH
_O_GC = _O_GB + CONV_WIDTH
_O_XC = _O_GC + CONV_WIDTH
_O_END = _O_XC + CONV_WIDTH

BF16 = jnp.bfloat16
F32 = jnp.float32
NEG_INF = float("-inf")


def _dot(a, b):
    return jnp.dot(a, b, preferred_element_type=F32)


def _dot_nt(a, b):
    return lax.dot_general(a, b, (((1,), (1,)), ((), ())), preferred_element_type=F32)


def _layer_norm(z, g, b):
    mu = jnp.mean(z, axis=-1, keepdims=True)
    zc = z - mu
    var = jnp.mean(zc * zc, axis=-1, keepdims=True)
    return zc * lax.rsqrt(var + LN_EPS) * g + b


def _front_kernel(sinks_ref, x_ref, rc_ref, rsa_ref, rsb_ref, w_in_ref, b_in_ref,
                  conv_w_ref, w_out_ref, b_out_ref, g1_ref, be1_ref, w_pq_ref,
                  k1_ref, k2_ref,
                  h1_ref, h1t_ref, s1_ref, s2_ref,
                  kbuf, vbuf, ubuf, kvar, vvar, ybuf, *, tm, tiles_per_seq):
    step = pl.program_id(0)
    nblk = tm // BLOCK

    @pl.when(step % tiles_per_seq == 0)
    def _():
        kbuf[0:BLOCK, :] = jnp.zeros((BLOCK, KV_WIDTH), F32)
        vbuf[0:BLOCK, :] = jnp.zeros((BLOCK, KV_WIDTH), F32)
        ubuf[0:8, :] = jnp.zeros((8, CONV_WIDTH), F32)

    x = x_ref[...]
    xb = x.astype(BF16)

    def proj(lo, hi):
        return _dot(xb, w_in_ref[:, lo:hi]) + b_in_ref[:, lo:hi]

    rc = rc_ref[...]
    rsa = rsa_ref[...]
    rsb = rsb_ref[...]

    def rope(t):
        return (t * rc + pltpu.roll(t, LANES - ROT_DIM // 2, 1) * rsa
                + pltpu.roll(t, ROT_DIM // 2, 1) * rsb)

    kbuf[BLOCK:BLOCK + tm, :] = rope(proj(_O_K, _O_V))
    vbuf[BLOCK:BLOCK + tm, :] = proj(_O_V, _O_GB)

    lane = lax.broadcasted_iota(jnp.int32, (tm + BLOCK, LANES), 1)
    lo = lane < HEAD_DIM
    for buf, var in ((kbuf, kvar), (vbuf, vvar)):
        t = buf[...]
        tr = pltpu.roll(t, HEAD_DIM, 1)
        var[0] = jnp.where(lo, t, 0.0).astype(BF16)
        var[1] = jnp.where(lo, 0.0, tr).astype(BF16)
        var[2] = jnp.where(lo, tr, 0.0).astype(BF16)
        var[3] = jnp.where(lo, 0.0, t).astype(BF16)

    qi = lax.broadcasted_iota(jnp.int32, (BLOCK, 2 * BLOCK), 0)
    kj = lax.broadcasted_iota(jnp.int32, (BLOCK, 2 * BLOCK), 1)
    diff = qi + BLOCK - kj
    band = (diff >= 0) & (diff < BLOCK)

    for p in range(N_Q_HEADS // 2):
        q_slab = (rope(proj(p * LANES, (p + 1) * LANES)) * (HEAD_DIM ** -0.5)).astype(BF16)
        for blk in range(nblk):
            first = (step * nblk + blk) % (tiles_per_seq * nblk) == 0
            kmin = jnp.where(first, BLOCK, 0)
            mask = band & (kj >= kmin)
            qs = q_slab[blk * BLOCK:(blk + 1) * BLOCK, :]
            o = jnp.zeros((BLOCK, LANES), F32)
            for hh in range(2):
                h = 2 * p + hh
                var = (h // (N_Q_HEADS // N_KV_HEADS)) * 2 + hh
                kc = kvar[var, blk * BLOCK:(blk + 2) * BLOCK, :]
                vc = vvar[var, blk * BLOCK:(blk + 2) * BLOCK, :]
                s = jnp.where(mask, _dot_nt(qs, kc), NEG_INF)
                sink = sinks_ref[h]
                m = jnp.maximum(jnp.max(s, axis=-1, keepdims=True), sink)
                pe = jnp.exp(s - m)
                den = jnp.sum(pe, axis=-1, keepdims=True) + jnp.exp(sink - m)
                o = o + _dot((pe / den).astype(BF16), vc)
            ybuf[blk * BLOCK:(blk + 1) * BLOCK, p * LANES:(p + 1) * LANES] = o.astype(BF16)

    u = proj(_O_GC, _O_XC) * proj(_O_XC, _O_END)
    ubuf[8:8 + tm, :] = u
    um1 = ubuf[7:7 + tm, :]
    um2 = ubuf[6:6 + tm, :]
    cw = conv_w_ref[...]
    yc = proj(_O_GB, _O_GC) * (cw[0:1, :] * um2 + cw[1:2, :] * um1 + cw[2:3, :] * u)
    ybuf[:, ATTN_WIDTH:] = yc.astype(BF16)

    ubuf[0:8, :] = ubuf[tm:tm + 8, :]
    kbuf[0:BLOCK, :] = kbuf[tm:tm + BLOCK, :]
    vbuf[0:BLOCK, :] = vbuf[tm:tm + BLOCK, :]

    mix = _dot(ybuf[...], w_out_ref[...]) + b_out_ref[...]
    h1 = _layer_norm(DN_ALPHA * x + mix, g1_ref[...], be1_ref[...])
    h1_ref[...] = h1
    h1t_ref[...] = h1.T.astype(BF16)

    qp = _dot(h1.astype(BF16), w_pq_ref[...]).astype(BF16)
    k1 = k1_ref[...]
    k2 = k2_ref[...]
    for h in range(PEER_HEADS):
        base = h * PEER_QDIM
        s1_ref[h] = _dot_nt(k1, qp[:, base:base + PEER_HALF])
        s2_ref[h] = _dot_nt(k2, qp[:, base + PEER_HALF:base + PEER_QDIM])


def _front(xf, rc, rsa, rsb, sinks, w_in, b_in, conv_w, w_out, b_out, g1, be1, w_pq, k1, k2,
           *, seq, tm):
    T = xf.shape[0]
    proj_w = w_in.shape[1]
    tiles_per_seq = seq // tm
    full = lambda shape: pl.BlockSpec(shape, lambda i: (0,) * len(shape))
    tok = lambda w: pl.BlockSpec((tm, w), lambda i: (i, 0))
    kern = functools.partial(_front_kernel, tm=tm, tiles_per_seq=tiles_per_seq)
    return pl.pallas_call(
        kern,
        grid=(T // tm,),
        in_specs=[
            pl.BlockSpec(memory_space=pltpu.SMEM),
            tok(D_MODEL), tok(LANES), tok(LANES), tok(LANES),
            full((D_MODEL, proj_w)), full((1, proj_w)),
            full((CONV_K, CONV_WIDTH)), full((D_MODEL, D_MODEL)), full((1, D_MODEL)),
            full((1, D_MODEL)), full((1, D_MODEL)),
            full((D_MODEL, PEER_HEADS * PEER_QDIM)),
            full((N_KEYS, PEER_HALF)), full((N_KEYS, PEER_HALF)),
        ],
        out_specs=[
            tok(D_MODEL),
            pl.BlockSpec((D_MODEL, tm), lambda i: (0, i)),
            pl.BlockSpec((PEER_HEADS, N_KEYS, tm), lambda i: (0, 0, i)),
            pl.BlockSpec((PEER_HEADS, N_KEYS, tm), lambda i: (0, 0, i)),
        ],
        out_shape=[
            jax.ShapeDtypeStruct((T, D_MODEL), F32),
            jax.ShapeDtypeStruct((D_MODEL, T), BF16),
            jax.ShapeDtypeStruct((PEER_HEADS, N_KEYS, T), F32),
            jax.ShapeDtypeStruct((PEER_HEADS, N_KEYS, T), F32),
        ],
        scratch_shapes=[
            pltpu.VMEM((tm + BLOCK, KV_WIDTH), F32),
            pltpu.VMEM((tm + BLOCK, KV_WIDTH), F32),
            pltpu.VMEM((tm + 8, CONV_WIDTH), F32),
            pltpu.VMEM((4, tm + BLOCK, KV_WIDTH), BF16),
            pltpu.VMEM((4, tm + BLOCK, KV_WIDTH), BF16),
            pltpu.VMEM((tm, D_MODEL), BF16),
        ],
        compiler_params=pltpu.CompilerParams(
            dimension_semantics=("arbitrary",), vmem_limit_bytes=VMEM_LIMIT),
        name="front",
    )(sinks, xf, rc, rsa, rsb, w_in, b_in, conv_w, w_out, b_out, g1, be1, w_pq, k1, k2)


def _top16(s):
    rows = lax.broadcasted_iota(jnp.int32, s.shape, 0)
    rem = s
    rank = jnp.full(s.shape, float(PEER_TOPK), F32)
    vals = []
    for k in range(PEER_TOPK):
        m = jnp.max(rem, axis=0, keepdims=True)
        idx = jnp.min(jnp.where(rem == m, rows, N_KEYS), axis=0, keepdims=True)
        sel = rows == idx
        rem = jnp.where(sel, NEG_INF, rem)
        rank = jnp.where(sel, float(k), rank)
        vals.append(m)
    return vals, rank


_N_CAND_ROWS = PEER_TOPK + 7 * 8 + 8


def _cand_positions():
    import numpy as np
    pos = np.full((_N_CAND_ROWS,), -1, np.int32)
    for r in range(_N_CAND_ROWS):
        if r < 16:
            a, b = 0, r
        elif r < 72:
            a, b = 1 + (r - 16) // 8, (r - 16) % 8
        else:
            a, b = 8 + (r - 72), 0
        if (a + 1) * (b + 1) <= PEER_TOPK:
            pos[r] = a * PEER_TOPK + b
    return pos


def _select_exact(s1, s2, pos, valid, row16):
    v1, rank1 = _top16(s1)
    v2, rank2 = _top16(s2)

    v1m = jnp.zeros((PEER_TOPK, LANES), F32)
    v2m = jnp.zeros((PEER_TOPK, LANES), F32)
    for a in range(PEER_TOPK):
        v1m = jnp.where(row16 == a, v1[a], v1m)
        v2m = jnp.where(row16 == a, v2[a], v2m)
    blocks = [v1[0] + v2m]
    for a in range(1, 8):
        blocks.append(v1[a] + v2m[0:8, :])
    blocks.append(v1m[8:16, :] + v2[0])
    cand = jnp.where(valid, jnp.concatenate(blocks, axis=0), NEG_INF)

    rem = cand
    chosen = jnp.zeros(cand.shape, F32)
    for _ in range(PEER_TOPK):
        m = jnp.max(rem, axis=0, keepdims=True)
        idx = jnp.min(jnp.where(rem == m, pos, 1 << 20), axis=0, keepdims=True)
        sel = (pos == idx) & valid
        rem = jnp.where(sel, NEG_INF, rem)
        chosen = jnp.where(sel, 1.0, chosen)

    top = v1[0] + v2[0]
    z = jnp.sum(chosen * jnp.exp(jnp.where(valid, cand, top) - top), axis=0, keepdims=True)

    counts = [jnp.sum(chosen[0:16, :], axis=0, keepdims=True)]
    for a in range(1, 8):
        counts.append(jnp.sum(chosen[8 + 8 * a:16 + 8 * a, :], axis=0, keepdims=True))
    for a in range(8, 16):
        counts.append(chosen[64 + a:65 + a, :])
    n1 = jnp.zeros(s1.shape, F32)
    for a in range(PEER_TOPK):
        n1 = jnp.where(rank1 == float(a), counts[a], n1)
    return n1, jnp.exp(s1 - v1[0]), rank2, jnp.exp(s2 - v2[0]) * (GELU_HALF / z)


def _sort_pairs(n):
    pairs = []
    p = 1
    while p < n:
        k = p
        while k >= 1:
            for j in range(k % p, n - k, 2 * k):
                for i in range(min(k, n - j - k)):
                    if (i + j) // (2 * p) == (i + j + k) // (2 * p):
                        pairs.append((i + j, i + j + k))
            k //= 2
        p *= 2
    return pairs


def _hi_lo(a, b):
    if a is None:
        return b, None
    if b is None:
        return a, None
    return jnp.maximum(a, b), jnp.minimum(a, b)


def _sublane_all(x, op):
    for shift in (4, 2, 1):
        x = op(x, pltpu.roll(x, shift, 0))
    return x


def _top16_sorted(x):
    x = list(x)
    for i, j in _sort_pairs(len(x)):
        x[i], x[j] = _hi_lo(x[i], x[j])
    for shift in (4, 2, 1):
        y = [None if t is None else pltpu.roll(t, shift, 0) for t in x]
        c = [_hi_lo(x[r], y[PEER_TOPK - 1 - r])[0] for r in range(PEER_TOPK)]
        d = PEER_TOPK // 2
        while d >= 1:
            for i in range(PEER_TOPK):
                if not i & d:
                    c[i], c[i + d] = _hi_lo(c[i], c[i + d])
            d //= 2
        x = c
    return x


def _select_fast(s1, s2):
    nv = N_KEYS // 8
    t1 = [s1[8 * v:8 * v + 8, :] for v in range(nv)]
    t2 = [s2[8 * v:8 * v + 8, :] for v in range(nv)]
    v1 = _top16_sorted(t1)
    v2 = _top16_sorted(t2)
    sub = lax.broadcasted_iota(jnp.int32, (8, LANES), 0)
    one = jnp.ones((8, LANES), F32)
    zero = jnp.zeros((8, LANES), F32)

    def spread(vals):
        out = vals[0]
        for r in range(1, 8):
            out = jnp.where(sub == r, vals[r], out)
        return out

    v2lo, v2hi, v1hi = spread(v2[0:8]), spread(v2[8:16]), spread(v1[8:16])
    cand = [v1[0] + v2lo, v1[0] + v2hi]
    for a in range(1, 8):
        cand.append(jnp.where(sub < PEER_TOPK // (a + 1), v1[a] + v2lo, NEG_INF))
    cand.append(v1hi + v2[0])
    best = _top16_sorted(cand + [None] * (PEER_TOPK - len(cand)))
    tau = best[PEER_TOPK - 1]
    z = one
    for k in range(1, PEER_TOPK):
        z = z + jnp.exp(best[k] - best[0])

    count = lambda t: _sublane_all(jnp.where(t >= tau, one, zero), jnp.add)
    counts = [count(cand[0]) + count(cand[1])]
    counts += [count(cand[a + 1]) for a in range(1, 8)]
    counts += [jnp.where(v1[a] + v2[0] >= tau, one, zero) for a in range(8, PEER_TOPK)]
    total = counts[0]
    for a in range(1, PEER_TOPK):
        total = total + counts[a]
    tie = jnp.where(total != float(PEER_TOPK), one, zero)

    n1 = [zero] * nv
    rank2 = [jnp.full((8, LANES), float(PEER_TOPK), F32)] * nv
    for a in reversed(range(PEER_TOPK)):
        n1 = [jnp.where(t >= v1[a], counts[a], n) for t, n in zip(t1, n1)]
        rank2 = [jnp.where(t >= v2[a], float(a), r) for t, r in zip(t2, rank2)]
    for v, tiles in ((v1, t1), (v2, t2)):
        inside = zero
        for t in tiles:
            inside = inside + jnp.where(t >= v[PEER_TOPK - 1], one, zero)
        tie = jnp.where(_sublane_all(inside, jnp.add) != float(PEER_TOPK), one, tie)
        for a in range(PEER_TOPK - 1):
            tie = jnp.where(v[a] == v[a + 1], one, tie)

    inv_z = GELU_HALF / z
    cat = lambda tiles: jnp.concatenate(tiles, axis=0)
    e1 = cat([jnp.exp(t - v1[0]) for t in t1])
    e2 = cat([jnp.exp(t - v2[0]) * inv_z for t in t2])
    return cat(n1), e1, cat(rank2), e2, tie


def _select_kernel(pos_ref, s1_ref, s2_ref, r2_ref, e2_ref, n1_ref, e1_ref, *, ts):
    def lanes(g):
        return pl.ds(pl.multiple_of(g * LANES, LANES), LANES)

    def finish(sl, fast):
        n1, e1, rank2, e2, tie = fast

        def exact():
            pos = pos_ref[...]
            row16 = lax.broadcasted_iota(jnp.int32, (PEER_TOPK, LANES), 0)
            return _select_exact(s1_ref[0, :, sl], s2_ref[0, :, sl], pos, pos >= 0, row16)

        n1, e1, rank2, e2 = lax.cond(jnp.max(tie) > 0.0, exact, lambda: (n1, e1, rank2, e2))
        n1_ref[:, 0, :, sl] = n1.reshape(N_KEYS // 8, 8, LANES)
        e1_ref[:, 0, :, sl] = e1.reshape(N_KEYS // 8, 8, LANES)
        r2_ref[0, :, sl] = rank2.astype(BF16)
        e2_ref[0, :, sl] = e2.astype(BF16)

    def trip(i, carry):
        sls = [lanes(SELECT_UNROLL * i + k) for k in range(SELECT_UNROLL)]
        fasts = [_select_fast(s1_ref[0, :, sl], s2_ref[0, :, sl]) for sl in sls]
        for sl, fast in zip(sls, fasts):
            finish(sl, fast)
        return carry

    assert (ts // LANES) % SELECT_UNROLL == 0
    lax.fori_loop(0, ts // LANES // SELECT_UNROLL, trip, 0)


def _select(s1t, s2t, *, ts):
    H, K, T = s1t.shape
    import numpy as np
    pos = jnp.asarray(np.tile(_cand_positions()[:, None], (1, LANES)))
    blk = pl.BlockSpec((1, K, ts), lambda t, h: (h, 0, t))
    blk1 = pl.BlockSpec((K // 8, 1, 8, ts), lambda t, h: (0, h, 0, t))
    return pl.pallas_call(
        functools.partial(_select_kernel, ts=ts),
        grid=(T // ts, H),
        in_specs=[pl.BlockSpec((_N_CAND_ROWS, LANES), lambda t, h: (0, 0)), blk, blk],
        out_specs=[blk, blk, blk1, blk1],
        out_shape=[
            jax.ShapeDtypeStruct((H, K, T), BF16),
            jax.ShapeDtypeStruct((H, K, T), BF16),
            jax.ShapeDtypeStruct((K // 8, H, 8, T), F32),
            jax.ShapeDtypeStruct((K // 8, H, 8, T), F32),
        ],
        compiler_params=pltpu.CompilerParams(
            dimension_semantics=("arbitrary", "arbitrary"), vmem_limit_bytes=VMEM_LIMIT),
        name="select",
    )(pos, s1t, s2t)


def _peer_kernel(u_ref, vt_ref, xt_ref, r2_ref, e2_ref, n1_ref, e1_ref, h1_ref, g2_ref, be2_ref,
                 out_ref, acc_ref, w_ref, a_ref, *, ec, tt):
    c = pl.program_id(1)
    nkb = ec // N_KEYS
    ngrp = nkb // 8
    assert ngrp * 8 == nkb

    @pl.when(c == 0)
    def _():
        acc_ref[...] = jnp.zeros(acc_ref.shape, F32)

    def row_bf16(ref, h, ib, sl):
        tiles = [jnp.broadcast_to(
            ref[c * ngrp + ib // 8, h, ib % 8:ib % 8 + 1, sl.start + k * LANES:sl.start + (k + 1) * LANES],
            (BF16_ROWS, LANES)) for k in range(BF16_COLS // LANES)]
        return jnp.concatenate(tiles, axis=1).astype(BF16)

    rows = BF16_ROWS
    a_ref[...] = _dot(u_ref[...], xt_ref[...])
    for nt in range(tt // BF16_COLS):
        nsl = slice(nt * BF16_COLS, (nt + 1) * BF16_COLS)
        for ib in range(nkb):
            n1b = [row_bf16(n1_ref, h, ib, nsl) for h in range(PEER_HEADS)]
            e1b = [row_bf16(e1_ref, h, ib, nsl) for h in range(PEER_HEADS)]
            for rg in range(N_KEYS // rows):
                ksl = slice(rg * rows, (rg + 1) * rows)
                g = None
                for h in range(PEER_HEADS):
                    e2 = e2_ref[h, ksl, nsl]
                    term = jnp.where(r2_ref[h, ksl, nsl] < n1b[h], e2, jnp.zeros_like(e2)) * e1b[h]
                    g = term if g is None else g + term
                esl = slice(ib * N_KEYS + rg * rows, ib * N_KEYS + (rg + 1) * rows)
                ab = a_ref[esl, nsl]
                ge = ab * (1.0 + lax.erf(ab * (2.0 ** -0.5)))
                w_ref[esl, nsl] = ge.astype(BF16) * g
    acc_ref[...] += _dot(vt_ref[...], w_ref[...])

    @pl.when(c == pl.num_programs(1) - 1)
    def _():
        z = DN_ALPHA * h1_ref[...] + acc_ref[...].T
        out_ref[...] = _layer_norm(z, g2_ref[...], be2_ref[...])


def _peer(u_bf, vt_bf, h1t, r2, e2, n1, e1, h1, g2, be2, *, ec, tt):
    E = u_bf.shape[0]
    T = h1.shape[0]
    tokrows = pl.BlockSpec((PEER_HEADS, N_KEYS, tt), lambda t, c: (0, 0, t))
    tokkeys = pl.BlockSpec((N_KEYS // 8, PEER_HEADS, 8, tt), lambda t, c: (0, 0, 0, t))
    vec = pl.BlockSpec((1, D_MODEL), lambda t, c: (0, 0))
    return pl.pallas_call(
        functools.partial(_peer_kernel, ec=ec, tt=tt),
        grid=(T // tt, E // ec),
        in_specs=[
            pl.BlockSpec((ec, D_MODEL), lambda t, c: (c, 0)),
            pl.BlockSpec((D_MODEL, ec), lambda t, c: (0, c)),
            pl.BlockSpec((D_MODEL, tt), lambda t, c: (0, t)),
            tokrows, tokrows, tokkeys, tokkeys,
            pl.BlockSpec((tt, D_MODEL), lambda t, c: (t, 0)),
            vec, vec,
        ],
        out_specs=pl.BlockSpec((tt, D_MODEL), lambda t, c: (t, 0)),
        out_shape=jax.ShapeDtypeStruct((T, D_MODEL), F32),
        scratch_shapes=[
            pltpu.VMEM((D_MODEL, tt), F32),
            pltpu.VMEM((ec, tt), BF16),
            pltpu.VMEM((ec, tt), F32),
        ],
        compiler_params=pltpu.CompilerParams(
            dimension_semantics=("arbitrary", "arbitrary"), vmem_limit_bytes=VMEM_LIMIT),
        name="peer",
    )(u_bf, vt_bf, h1t, r2, e2, n1, e1, h1, g2, be2)


def _rope_tables(positions):
    T = positions.size
    half = ROT_DIM // 2
    inv_freq = ROPE_THETA ** (-jnp.arange(0, ROT_DIM, 2, dtype=F32) / ROT_DIM)
    ang = positions.reshape(T, 1).astype(F32) * inv_freq
    cos, sin = jnp.cos(ang), jnp.sin(ang)
    zeros = lambda w: jnp.zeros((T, w), F32)
    rc = jnp.concatenate([cos, cos, jnp.ones((T, HEAD_DIM - ROT_DIM), F32)], axis=1)
    rsa = jnp.concatenate([-sin, zeros(HEAD_DIM - half)], axis=1)
    rsb = jnp.concatenate([zeros(half), sin, zeros(HEAD_DIM - ROT_DIM)], axis=1)
    rep = LANES // HEAD_DIM
    return jnp.tile(rc, (1, rep)), jnp.tile(rsa, (1, rep)), jnp.tile(rsb, (1, rep))


def _tiles(seq, total_tokens):
    tm = next(c for c in (512, 256, BLOCK) if seq % c == 0)
    ts = next(c for c in (1024, 512, LANES) if total_tokens % c == 0)
    tt = 512 if total_tokens % 512 == 0 else 256
    ec = 16 * N_KEYS
    return tm, ts, tt, ec


def kernel(x, positions, w_in, b_in, attn_sinks, conv_w, w_out, b_out, ln1_g, ln1_b,
           w_pq, sub_keys1, sub_keys2, u_experts, v_experts, ln2_g, ln2_b):
    B, S, D = x.shape
    T = B * S
    assert D == D_MODEL and S % BLOCK == 0 and T % 256 == 0
    assert w_in.shape[0] == DEPTH
    tm, ts, tt, ec = _tiles(S, T)
    row = lambda v: v.reshape(1, -1).astype(F32)

    rc, rsa, rsb = _rope_tables(positions)
    h1, h1t, s1t, s2t = _front(
        x.reshape(T, D), rc, rsa, rsb, attn_sinks[0].astype(F32),
        w_in[0].astype(BF16), row(b_in[0]), conv_w[0].astype(F32),
        w_out[0].astype(BF16), row(b_out[0]), row(ln1_g[0]), row(ln1_b[0]),
        w_pq[0].astype(BF16), sub_keys1[0].astype(BF16), sub_keys2[0].astype(BF16),
        seq=S, tm=tm)
    r2, e2, n1, e1 = _select(s1t, s2t, ts=ts)
    out = _peer(u_experts[0].astype(BF16), v_experts[0].T.astype(BF16), h1t,
                r2, e2, n1, e1, h1, row(ln2_g[0]), row(ln2_b[0]), ec=ec, tt=tt)
    return out.reshape(B, S, D)
```

```python
import functools

import jax
import jax.numpy as jnp
from jax import lax
from jax.experimental import pallas as pl
from jax.experimental.pallas import tpu as pltpu

D_MODEL = 1024
HEAD_DIM = 64
N_Q_HEADS = 8
N_KV_HEADS = 2
ATTN_WIDTH = N_Q_HEADS * HEAD_DIM
KV_WIDTH = N_KV_HEADS * HEAD_DIM
BLOCK = 128
ROT_DIM = HEAD_DIM // 4
ROPE_THETA = 500000.0
CONV_WIDTH = D_MODEL - ATTN_WIDTH
CONV_K = 3
N_KEYS = 128
PEER_HEADS = 8
PEER_QDIM = 256
PEER_HALF = PEER_QDIM // 2
PEER_TOPK = 16
DEPTH = 1
DN_ALPHA = (2.0 * DEPTH) ** 0.25
LN_EPS = 1e-5

LANES = 128
BF16_ROWS = 8
BF16_COLS = 256
SELECT_UNROLL = 2
GELU_HALF = 0.5
VMEM_LIMIT = 56 * 1024 * 1024

_O_K = ATTN_WIDTH
_O_V = _O_K + KV_WIDTH
_O_GB = _O_V + KV_WIDTH
_O_GC = _O_GB + CONV_WIDTH
_O_XC = _O_GC + CONV_WIDTH
_O_END = _O_XC + CONV_WIDTH

BF16 = jnp.bfloat16
F32 = jnp.float32
NEG_INF = float("-inf")


def _dot(a, b):
    return jnp.dot(a, b, preferred_element_type=F32)


def _dot_nt(a, b):
    return lax.dot_general(a, b, (((1,), (1,)), ((), ())), preferred_element_type=F32)


def _layer_norm(z, g, b):
    mu = jnp.mean(z, axis=-1, keepdims=True)
    zc = z - mu
    var = jnp.mean(zc * zc, axis=-1, keepdims=True)
    return zc * lax.rsqrt(var + LN_EPS) * g + b


def _front_kernel(sinks_ref, x_ref, rope_ref, w_in_ref, b_in_ref,
                  conv_w_ref, w_out_ref, b_out_ref, g1_ref, be1_ref, w_pq_ref,
                  k1_ref, k2_ref,
                  h1_ref, h1t_ref, s1_ref, s2_ref,
                  kbuf, vbuf, ubuf, kvar, vvar, ybuf, *, tm, tiles_per_seq):
    step = pl.program_id(0)
    nblk = tm // BLOCK

    @pl.when(step % tiles_per_seq == 0)
    def _():
        kbuf[0:BLOCK, :] = jnp.zeros((BLOCK, KV_WIDTH), F32)
        vbuf[0:BLOCK, :] = jnp.zeros((BLOCK, KV_WIDTH), F32)
        ubuf[0:8, :] = jnp.zeros((8, CONV_WIDTH), F32)

    x = x_ref[...]
    xb = x.astype(BF16)

    def proj(lo, hi):
        return _dot(xb, w_in_ref[:, lo:hi]) + b_in_ref[:, lo:hi]

    tab = rope_ref[...]
    head_lane = lax.broadcasted_iota(jnp.int32, (tm, LANES), 1) % HEAD_DIM
    rot_a = head_lane < ROT_DIM // 2
    rot_b = (head_lane >= ROT_DIM // 2) & (head_lane < ROT_DIM)
    rc = jnp.where(rot_a, tab, jnp.where(rot_b, pltpu.roll(tab, ROT_DIM // 2, 1), 1.0))
    rsa = jnp.where(rot_a, -pltpu.roll(tab, LANES - ROT_DIM // 2, 1), 0.0)
    rsb = jnp.where(rot_b, tab, 0.0)

    def rope(t):
        return (t * rc + pltpu.roll(t, LANES - ROT_DIM // 2, 1) * rsa
                + pltpu.roll(t, ROT_DIM // 2, 1) * rsb)

    kbuf[BLOCK:BLOCK + tm, :] = rope(proj(_O_K, _O_V))
    vbuf[BLOCK:BLOCK + tm, :] = proj(_O_V, _O_GB)

    lane = lax.broadcasted_iota(jnp.int32, (tm + BLOCK, LANES), 1)
    lo = lane < HEAD_DIM
    for buf, var in ((kbuf, kvar), (vbuf, vvar)):
        t = buf[...]
        tr = pltpu.roll(t, HEAD_DIM, 1)
        var[0] = jnp.where(lo, t, 0.0).astype(BF16)
        var[1] = jnp.where(lo, 0.0, tr).astype(BF16)
        var[2] = jnp.where(lo, tr, 0.0).astype(BF16)
        var[3] = jnp.where(lo, 0.0, t).astype(BF16)

    qi = lax.broadcasted_iota(jnp.int32, (BLOCK, 2 * BLOCK), 0)
    kj = lax.broadcasted_iota(jnp.int32, (BLOCK, 2 * BLOCK), 1)
    diff = qi + BLOCK - kj
    band = (diff >= 0) & (diff < BLOCK)

    for p in range(N_Q_HEADS // 2):
        q_slab = (rope(proj(p * LANES, (p + 1) * LANES)) * (HEAD_DIM ** -0.5)).astype(BF16)
        for blk in range(nblk):
            first = (step * nblk + blk) % (tiles_per_seq * nblk) == 0
            kmin = jnp.where(first, BLOCK, 0)
            mask = band & (kj >= kmin)
            qs = q_slab[blk * BLOCK:(blk + 1) * BLOCK, :]
            o = jnp.zeros((BLOCK, LANES), F32)
            for hh in range(2):
                h = 2 * p + hh
                var = (h // (N_Q_HEADS // N_KV_HEADS)) * 2 + hh
                kc = kvar[var, blk * BLOCK:(blk + 2) * BLOCK, :]
                vc = vvar[var, blk * BLOCK:(blk + 2) * BLOCK, :]
                s = jnp.where(mask, _dot_nt(qs, kc), NEG_INF)
                sink = sinks_ref[h]
                m = jnp.maximum(jnp.max(s, axis=-1, keepdims=True), sink)
                pe = jnp.exp(s - m)
                den = jnp.sum(pe, axis=-1, keepdims=True) + jnp.exp(sink - m)
                o = o + _dot((pe / den).astype(BF16), vc)
            ybuf[blk * BLOCK:(blk + 1) * BLOCK, p * LANES:(p + 1) * LANES] = o.astype(BF16)

    u = proj(_O_GC, _O_XC) * proj(_O_XC, _O_END)
    ubuf[8:8 + tm, :] = u
    um1 = ubuf[7:7 + tm, :]
    um2 = ubuf[6:6 + tm, :]
    cw = conv_w_ref[...]
    yc = proj(_O_GB, _O_GC) * (cw[0:1, :] * um2 + cw[1:2, :] * um1 + cw[2:3, :] * u)
    ybuf[:, ATTN_WIDTH:] = yc.astype(BF16)

    ubuf[0:8, :] = ubuf[tm:tm + 8, :]
    kbuf[0:BLOCK, :] = kbuf[tm:tm + BLOCK, :]
    vbuf[0:BLOCK, :] = vbuf[tm:tm + BLOCK, :]

    mix = _dot(ybuf[...], w_out_ref[...]) + b_out_ref[...]
    h1 = _layer_norm(DN_ALPHA * x + mix, g1_ref[...], be1_ref[...])
    h1_ref[...] = h1
    h1t_ref[...] = h1.T.astype(BF16)

    qp = _dot(h1.astype(BF16), w_pq_ref[...]).astype(BF16)
    k1 = k1_ref[...]
    k2 = k2_ref[...]
    for h in range(PEER_HEADS):
        base = h * PEER_QDIM
        s1_ref[h] = _dot_nt(k1, qp[:, base:base + PEER_HALF])
        s2_ref[h] = _dot_nt(k2, qp[:, base + PEER_HALF:base + PEER_QDIM])


def _front(xf, rope, sinks, w_in, b_in, conv_w, w_out, b_out, g1, be1, w_pq, k1, k2,
           *, seq, tm):
    T = xf.shape[0]
    proj_w = w_in.shape[1]
    tiles_per_seq = seq // tm
    full = lambda shape: pl.BlockSpec(shape, lambda i: (0,) * len(shape))
    tok = lambda w: pl.BlockSpec((tm, w), lambda i: (i, 0))
    kern = functools.partial(_front_kernel, tm=tm, tiles_per_seq=tiles_per_seq)
    return pl.pallas_call(
        kern,
        grid=(T // tm,),
        in_specs=[
            pl.BlockSpec(memory_space=pltpu.SMEM),
            tok(D_MODEL), tok(LANES),
            full((D_MODEL, proj_w)), full((1, proj_w)),
            full((CONV_K, CONV_WIDTH)), full((D_MODEL, D_MODEL)), full((1, D_MODEL)),
            full((1, D_MODEL)), full((1, D_MODEL)),
            full((D_MODEL, PEER_HEADS * PEER_QDIM)),
            full((N_KEYS, PEER_HALF)), full((N_KEYS, PEER_HALF)),
        ],
        out_specs=[
            tok(D_MODEL),
            pl.BlockSpec((D_MODEL, tm), lambda i: (0, i)),
            pl.BlockSpec((PEER_HEADS, N_KEYS, tm), lambda i: (0, 0, i)),
            pl.BlockSpec((PEER_HEADS, N_KEYS, tm), lambda i: (0, 0, i)),
        ],
        out_shape=[
            jax.ShapeDtypeStruct((T, D_MODEL), F32),
            jax.ShapeDtypeStruct((D_MODEL, T), BF16),
            jax.ShapeDtypeStruct((PEER_HEADS, N_KEYS, T), F32),
            jax.ShapeDtypeStruct((PEER_HEADS, N_KEYS, T), F32),
        ],
        scratch_shapes=[
            pltpu.VMEM((tm + BLOCK, KV_WIDTH), F32),
            pltpu.VMEM((tm + BLOCK, KV_WIDTH), F32),
            pltpu.VMEM((tm + 8, CONV_WIDTH), F32),
            pltpu.VMEM((4, tm + BLOCK, KV_WIDTH), BF16),
            pltpu.VMEM((4, tm + BLOCK, KV_WIDTH), BF16),
            pltpu.VMEM((tm, D_MODEL), BF16),
        ],
        compiler_params=pltpu.CompilerParams(
            dimension_semantics=("arbitrary",), vmem_limit_bytes=VMEM_LIMIT),
        name="front",
    )(sinks, xf, rope, w_in, b_in, conv_w, w_out, b_out, g1, be1, w_pq, k1, k2)


def _top16(s):
    rows = lax.broadcasted_iota(jnp.int32, s.shape, 0)
    rem = s
    rank = jnp.full(s.shape, float(PEER_TOPK), F32)
    vals = []
    for k in range(PEER_TOPK):
        m = jnp.max(rem, axis=0, keepdims=True)
        idx = jnp.min(jnp.where(rem == m, rows, N_KEYS), axis=0, keepdims=True)
        sel = rows == idx
        rem = jnp.where(sel, NEG_INF, rem)
        rank = jnp.where(sel, float(k), rank)
        vals.append(m)
    return vals, rank


_N_CAND_ROWS = PEER_TOPK + 7 * 8 + 8


def _cand_positions():
    import numpy as np
    pos = np.full((_N_CAND_ROWS,), -1, np.int32)
    for r in range(_N_CAND_ROWS):
        if r < 16:
            a, b = 0, r
        elif r < 72:
            a, b = 1 + (r - 16) // 8, (r - 16) % 8
        else:
            a, b = 8 + (r - 72), 0
        if (a + 1) * (b + 1) <= PEER_TOPK:
            pos[r] = a * PEER_TOPK + b
    return pos


def _select_exact(s1, s2, pos, valid, row16):
    v1, rank1 = _top16(s1)
    v2, rank2 = _top16(s2)

    v1m = jnp.zeros((PEER_TOPK, LANES), F32)
    v2m = jnp.zeros((PEER_TOPK, LANES), F32)
    for a in range(PEER_TOPK):
        v1m = jnp.where(row16 == a, v1[a], v1m)
        v2m = jnp.where(row16 == a, v2[a], v2m)
    blocks = [v1[0] + v2m]
    for a in range(1, 8):
        blocks.append(v1[a] + v2m[0:8, :])
    blocks.append(v1m[8:16, :] + v2[0])
    cand = jnp.where(valid, jnp.concatenate(blocks, axis=0), NEG_INF)

    rem = cand
    chosen = jnp.zeros(cand.shape, F32)
    for _ in range(PEER_TOPK):
        m = jnp.max(rem, axis=0, keepdims=True)
        idx = jnp.min(jnp.where(rem == m, pos, 1 << 20), axis=0, keepdims=True)
        sel = (pos == idx) & valid
        rem = jnp.where(sel, NEG_INF, rem)
        chosen = jnp.where(sel, 1.0, chosen)

    top = v1[0] + v2[0]
    z = jnp.sum(chosen * jnp.exp(jnp.where(valid, cand, top) - top), axis=0, keepdims=True)

    counts = [jnp.sum(chosen[0:16, :], axis=0, keepdims=True)]
    for a in range(1, 8):
        counts.append(jnp.sum(chosen[8 + 8 * a:16 + 8 * a, :], axis=0, keepdims=True))
    for a in range(8, 16):
        counts.append(chosen[64 + a:65 + a, :])
    n1 = jnp.zeros(s1.shape, F32)
    for a in range(PEER_TOPK):
        n1 = jnp.where(rank1 == float(a), counts[a], n1)
    return n1, jnp.exp(s1 - v1[0]), rank2, jnp.exp(s2 - v2[0]) * (GELU_HALF / z)


def _sort_pairs(n):
    pairs = []
    p = 1
    while p < n:
        k = p
        while k >= 1:
            for j in range(k % p, n - k, 2 * k):
                for i in range(min(k, n - j - k)):
                    if (i + j) // (2 * p) == (i + j + k) // (2 * p):
                        pairs.append((i + j, i + j + k))
            k //= 2
        p *= 2
    return pairs


def _hi_lo(a, b):
    if a is None:
        return b, None
    if b is None:
        return a, None
    return jnp.maximum(a, b), jnp.minimum(a, b)


def _sublane_all(x, op):
    for shift in (4, 2, 1):
        x = op(x, pltpu.roll(x, shift, 0))
    return x


def _top16_sorted(x):
    x = list(x)
    for i, j in _sort_pairs(len(x)):
        x[i], x[j] = _hi_lo(x[i], x[j])
    for shift in (4, 2, 1):
        y = [None if t is None else pltpu.roll(t, shift, 0) for t in x]
        c = [_hi_lo(x[r], y[PEER_TOPK - 1 - r])[0] for r in range(PEER_TOPK)]
        d = PEER_TOPK // 2
        while d >= 1:
            for i in range(PEER_TOPK):
                if not i & d:
                    c[i], c[i + d] = _hi_lo(c[i], c[i + d])
            d //= 2
        x = c
    return x


def _select_fast(s1, s2):
    nv = N_KEYS // 8
    t1 = [s1[8 * v:8 * v + 8, :] for v in range(nv)]
    t2 = [s2[8 * v:8 * v + 8, :] for v in range(nv)]
    v1 = _top16_sorted(t1)
    v2 = _top16_sorted(t2)
    sub = lax.broadcasted_iota(jnp.int32, (8, LANES), 0)
    one = jnp.ones((8, LANES), F32)
    zero = jnp.zeros((8, LANES), F32)

    def spread(vals):
        out = vals[0]
        for r in range(1, 8):
            out = jnp.where(sub == r, vals[r], out)
        return out

    v2lo, v2hi, v1hi = spread(v2[0:8]), spread(v2[8:16]), spread(v1[8:16])
    cand = [v1[0] + v2lo, v1[0] + v2hi]
    for a in range(1, 8):
        cand.append(jnp.where(sub < PEER_TOPK // (a + 1), v1[a] + v2lo, NEG_INF))
    cand.append(v1hi + v2[0])
    best = _top16_sorted(cand + [None] * (PEER_TOPK - len(cand)))
    tau = best[PEER_TOPK - 1]
    z = one
    for k in range(1, PEER_TOPK):
        z = z + jnp.exp(best[k] - best[0])

    count = lambda t: _sublane_all(jnp.where(t >= tau, one, zero), jnp.add)
    counts = [count(cand[0]) + count(cand[1])]
    counts += [count(cand[a + 1]) for a in range(1, 8)]
    counts += [jnp.where(v1[a] + v2[0] >= tau, one, zero) for a in range(8, PEER_TOPK)]
    total = counts[0]
    for a in range(1, PEER_TOPK):
        total = total + counts[a]
    tie = jnp.where(total != float(PEER_TOPK), one, zero)

    n1 = [zero] * nv
    rank2 = [jnp.full((8, LANES), float(PEER_TOPK), F32)] * nv
    for a in reversed(range(PEER_TOPK)):
        n1 = [jnp.where(t >= v1[a], counts[a], n) for t, n in zip(t1, n1)]
        rank2 = [jnp.where(t >= v2[a], float(a), r) for t, r in zip(t2, rank2)]
    for v, tiles in ((v1, t1), (v2, t2)):
        inside = zero
        for t in tiles:
            inside = inside + jnp.where(t >= v[PEER_TOPK - 1], one, zero)
        tie = jnp.where(_sublane_all(inside, jnp.add) != float(PEER_TOPK), one, tie)
        for a in range(PEER_TOPK - 1):
            tie = jnp.where(v[a] == v[a + 1], one, tie)

    inv_z = GELU_HALF / z
    cat = lambda tiles: jnp.concatenate(tiles, axis=0)
    e1 = cat([jnp.exp(t - v1[0]) for t in t1])
    e2 = cat([jnp.exp(t - v2[0]) * inv_z for t in t2])
    return cat(n1), e1, cat(rank2), e2, tie


def _select_kernel(pos_ref, s1_ref, s2_ref, r2_ref, e2_ref, n1_ref, e1_ref, *, ts):
    def lanes(g):
        return pl.ds(pl.multiple_of(g * LANES, LANES), LANES)

    def finish(sl, fast):
        n1, e1, rank2, e2, tie = fast

        def exact():
            pos = pos_ref[...]
            row16 = lax.broadcasted_iota(jnp.int32, (PEER_TOPK, LANES), 0)
            return _select_exact(s1_ref[0, :, sl], s2_ref[0, :, sl], pos, pos >= 0, row16)

        n1, e1, rank2, e2 = lax.cond(jnp.max(tie) > 0.0, exact, lambda: (n1, e1, rank2, e2))
        n1_ref[:, 0, :, sl] = n1.reshape(N_KEYS // 8, 8, LANES)
        e1_ref[:, 0, :, sl] = e1.reshape(N_KEYS // 8, 8, LANES)
        r2_ref[0, :, sl] = rank2.astype(BF16)
        e2_ref[0, :, sl] = e2.astype(BF16)

    def trip(i, carry):
        sls = [lanes(SELECT_UNROLL * i + k) for k in range(SELECT_UNROLL)]
        fasts = [_select_fast(s1_ref[0, :, sl], s2_ref[0, :, sl]) for sl in sls]
        for sl, fast in zip(sls, fasts):
            finish(sl, fast)
        return carry

    assert (ts // LANES) % SELECT_UNROLL == 0
    lax.fori_loop(0, ts // LANES // SELECT_UNROLL, trip, 0)


def _select(s1t, s2t, *, ts):
    H, K, T = s1t.shape
    import numpy as np
    pos = jnp.asarray(np.tile(_cand_positions()[:, None], (1, LANES)))
    blk = pl.BlockSpec((1, K, ts), lambda t, h: (h, 0, t))
    blk1 = pl.BlockSpec((K // 8, 1, 8, ts), lambda t, h: (0, h, 0, t))
    return pl.pallas_call(
        functools.partial(_select_kernel, ts=ts),
        grid=(T // ts, H),
        in_specs=[pl.BlockSpec((_N_CAND_ROWS, LANES), lambda t, h: (0, 0)), blk, blk],
        out_specs=[blk, blk, blk1, blk1],
        out_shape=[
            jax.ShapeDtypeStruct((H, K, T), BF16),
            jax.ShapeDtypeStruct((H, K, T), BF16),
            jax.ShapeDtypeStruct((K // 8, H, 8, T), F32),
            jax.ShapeDtypeStruct((K // 8, H, 8, T), F32),
        ],
        compiler_params=pltpu.CompilerParams(
            dimension_semantics=("arbitrary", "arbitrary"), vmem_limit_bytes=VMEM_LIMIT),
        name="select",
    )(pos, s1t, s2t)


def _peer_kernel(u_ref, vt_ref, xt_ref, r2_ref, e2_ref, n1_ref, e1_ref, h1_ref, g2_ref, be2_ref,
                 out_ref, acc_ref, w_ref, a_ref, *, ec, tt):
    c = pl.program_id(1)
    nkb = ec // N_KEYS
    ngrp = nkb // 8
    assert ngrp * 8 == nkb

    @pl.when(c == 0)
    def _():
        acc_ref[...] = jnp.zeros(acc_ref.shape, F32)

    def row_bf16(ref, h, ib, sl):
        tiles = [jnp.broadcast_to(
            ref[c * ngrp + ib // 8, h, ib % 8:ib % 8 + 1, sl.start + k * LANES:sl.start + (k + 1) * LANES],
            (BF16_ROWS, LANES)) for k in range(BF16_COLS // LANES)]
        return jnp.concatenate(tiles, axis=1).astype(BF16)

    rows = BF16_ROWS
    a_ref[...] = _dot(u_ref[...], xt_ref[...])
    for nt in range(tt // BF16_COLS):
        nsl = slice(nt * BF16_COLS, (nt + 1) * BF16_COLS)
        for ib in range(nkb):
            n1b = [row_bf16(n1_ref, h, ib, nsl) for h in range(PEER_HEADS)]
            e1b = [row_bf16(e1_ref, h, ib, nsl) for h in range(PEER_HEADS)]
            for rg in range(N_KEYS // rows):
                ksl = slice(rg * rows, (rg + 1) * rows)
                g = None
                for h in range(PEER_HEADS):
                    e2 = e2_ref[h, ksl, nsl]
                    term = jnp.where(r2_ref[h, ksl, nsl] < n1b[h], e2, jnp.zeros_like(e2)) * e1b[h]
                    g = term if g is None else g + term
                esl = slice(ib * N_KEYS + rg * rows, ib * N_KEYS + (rg + 1) * rows)
                ab = a_ref[esl, nsl]
                ge = ab * (1.0 + lax.erf(ab * (2.0 ** -0.5)))
                w_ref[esl, nsl] = ge.astype(BF16) * g
    acc_ref[...] += _dot(vt_ref[...], w_ref[...])

    @pl.when(c == pl.num_programs(1) - 1)
    def _():
        z = DN_ALPHA * h1_ref[...] + acc_ref[...].T
        out_ref[...] = _layer_norm(z, g2_ref[...], be2_ref[...])


def _peer(u_bf, vt_bf, h1t, r2, e2, n1, e1, h1, g2, be2, *, ec, tt):
    E = u_bf.shape[0]
    T = h1.shape[0]
    tokrows = pl.BlockSpec((PEER_HEADS, N_KEYS, tt), lambda t, c: (0, 0, t))
    tokkeys = pl.BlockSpec((N_KEYS // 8, PEER_HEADS, 8, tt), lambda t, c: (0, 0, 0, t))
    vec = pl.BlockSpec((1, D_MODEL), lambda t, c: (0, 0))
    return pl.pallas_call(
        functools.partial(_peer_kernel, ec=ec, tt=tt),
        grid=(T // tt, E // ec),
        in_specs=[
            pl.BlockSpec((ec, D_MODEL), lambda t, c: (c, 0)),
            pl.BlockSpec((D_MODEL, ec), lambda t, c: (0, c)),
            pl.BlockSpec((D_MODEL, tt), lambda t, c: (0, t)),
            tokrows, tokrows, tokkeys, tokkeys,
            pl.BlockSpec((tt, D_MODEL), lambda t, c: (t, 0)),
            vec, vec,
        ],
        out_specs=pl.BlockSpec((tt, D_MODEL), lambda t, c: (t, 0)),
        out_shape=jax.ShapeDtypeStruct((T, D_MODEL), F32),
        scratch_shapes=[
            pltpu.VMEM((D_MODEL, tt), F32),
            pltpu.VMEM((ec, tt), BF16),
            pltpu.VMEM((ec, tt), F32),
        ],
        compiler_params=pltpu.CompilerParams(
            dimension_semantics=("arbitrary", "arbitrary"), vmem_limit_bytes=VMEM_LIMIT),
        name="peer",
    )(u_bf, vt_bf, h1t, r2, e2, n1, e1, h1, g2, be2)


def _rope_table(positions):
    T = positions.size
    inv_freq = ROPE_THETA ** (-jnp.arange(0, ROT_DIM, 2, dtype=F32) / ROT_DIM)
    ang = positions.reshape(T, 1).astype(F32) * inv_freq
    head = jnp.concatenate([jnp.cos(ang), jnp.sin(ang), jnp.zeros((T, HEAD_DIM - ROT_DIM), F32)],
                           axis=1)
    return jnp.tile(head, (1, LANES // HEAD_DIM))


def _tiles(seq, total_tokens):
    tm = next(c for c in (512, 256, BLOCK) if seq % c == 0)
    ts = next(c for c in (1024, 512, LANES) if total_tokens % c == 0)
    tt = 512 if total_tokens % 512 == 0 else 256
    ec = 16 * N_KEYS
    return tm, ts, tt, ec


def kernel(x, positions, w_in, b_in, attn_sinks, conv_w, w_out, b_out, ln1_g, ln1_b,
           w_pq, sub_keys1, sub_keys2, u_experts, v_experts, ln2_g, ln2_b):
    B, S, D = x.shape
    T = B * S
    assert D == D_MODEL and S % BLOCK == 0 and T % 256 == 0
    assert w_in.shape[0] == DEPTH
    tm, ts, tt, ec = _tiles(S, T)
    row = lambda v: v.reshape(1, -1).astype(F32)

    h1, h1t, s1t, s2t = _front(
        x.reshape(T, D), _rope_table(positions), attn_sinks[0].astype(F32),
        w_in[0].astype(BF16), row(b_in[0]), conv_w[0].astype(F32),
        w_out[0].astype(BF16), row(b_out[0]), row(ln1_g[0]), row(ln1_b[0]),
        w_pq[0].astype(BF16), sub_keys1[0].astype(BF16), sub_keys2[0].astype(BF16),
        seq=S, tm=tm)
    r2, e2, n1, e1 = _select(s1t, s2t, ts=ts)
    out = _peer(u_experts[0].astype(BF16), v_experts[0].T.astype(BF16), h1t,
                r2, e2, n1, e1, h1, row(ln2_g[0]), row(ln2_b[0]), ec=ec, tt=tt)
    return out.reshape(B, S, D)
```

```python
import functools

import jax
import jax.numpy as jnp
from jax import lax
from jax.experimental import pallas as pl
from jax.experimental.pallas import tpu as pltpu

D_MODEL = 1024
HEAD_DIM = 64
N_Q_HEADS = 8
N_KV_HEADS = 2
ATTN_WIDTH = N_Q_HEADS * HEAD_DIM
KV_WIDTH = N_KV_HEADS * HEAD_DIM
BLOCK = 128
ROT_DIM = HEAD_DIM // 4
ROPE_THETA = 500000.0
CONV_WIDTH = D_MODEL - ATTN_WIDTH
CONV_K = 3
N_KEYS = 128
PEER_HEADS = 8
PEER_QDIM = 256
PEER_HALF = PEER_QDIM // 2
PEER_TOPK = 16
DEPTH = 1
DN_ALPHA = (2.0 * DEPTH) ** 0.25
LN_EPS = 1e-5

LANES = 128
BF16_ROWS = 8
BF16_COLS = 256
SELECT_UNROLL = 2
GELU_IN = 2.0 ** -0.5
GELU_GATE = 2.0 ** -0.5
VMEM_LIMIT = 56 * 1024 * 1024

_O_K = ATTN_WIDTH
_O_V = _O_K + KV_WIDTH
_O_GB = _O_V + KV_WIDTH
_O_GC = _O_GB + CONV_WIDTH
_O_XC = _O_GC + CONV_WIDTH
_O_END = _O_XC + CONV_WIDTH

BF16 = jnp.bfloat16
F32 = jnp.float32
NEG_INF = float("-inf")


def _dot(a, b):
    return jnp.dot(a, b, preferred_element_type=F32)


def _dot_nt(a, b):
    return lax.dot_general(a, b, (((1,), (1,)), ((), ())), preferred_element_type=F32)


def _layer_norm(z, g, b):
    mu = jnp.mean(z, axis=-1, keepdims=True)
    zc = z - mu
    var = jnp.mean(zc * zc, axis=-1, keepdims=True)
    return zc * lax.rsqrt(var + LN_EPS) * g + b


def _front_kernel(sinks_ref, x_ref, rope_ref, w_in_ref, b_in_ref,
                  conv_w_ref, w_out_ref, b_out_ref, g1_ref, be1_ref, w_pq_ref,
                  k1_ref, k2_ref,
                  h1_ref, h1t_ref, s1_ref, s2_ref,
                  kbuf, vbuf, ubuf, kvar, vvar, ybuf, *, tm, tiles_per_seq):
    step = pl.program_id(0)
    nblk = tm // BLOCK

    @pl.when(step % tiles_per_seq == 0)
    def _():
        kbuf[0:BLOCK, :] = jnp.zeros((BLOCK, KV_WIDTH), F32)
        vbuf[0:BLOCK, :] = jnp.zeros((BLOCK, KV_WIDTH), F32)
        ubuf[0:8, :] = jnp.zeros((8, CONV_WIDTH), F32)

    x = x_ref[...]
    xb = x.astype(BF16)

    def proj(lo, hi):
        return _dot(xb, w_in_ref[:, lo:hi]) + b_in_ref[:, lo:hi]

    tab = rope_ref[...]
    head_lane = lax.broadcasted_iota(jnp.int32, (tm, LANES), 1) % HEAD_DIM
    rot_a = head_lane < ROT_DIM // 2
    rot_b = (head_lane >= ROT_DIM // 2) & (head_lane < ROT_DIM)
    rc = jnp.where(rot_a, tab, jnp.where(rot_b, pltpu.roll(tab, ROT_DIM // 2, 1), 1.0))
    rsa = jnp.where(rot_a, -pltpu.roll(tab, LANES - ROT_DIM // 2, 1), 0.0)
    rsb = jnp.where(rot_b, tab, 0.0)

    def rope(t):
        return (t * rc + pltpu.roll(t, LANES - ROT_DIM // 2, 1) * rsa
                + pltpu.roll(t, ROT_DIM // 2, 1) * rsb)

    kbuf[BLOCK:BLOCK + tm, :] = rope(proj(_O_K, _O_V))
    vbuf[BLOCK:BLOCK + tm, :] = proj(_O_V, _O_GB)

    lane = lax.broadcasted_iota(jnp.int32, (tm + BLOCK, LANES), 1)
    lo = lane < HEAD_DIM
    for buf, var in ((kbuf, kvar), (vbuf, vvar)):
        t = buf[...]
        tr = pltpu.roll(t, HEAD_DIM, 1)
        var[0] = jnp.where(lo, t, 0.0).astype(BF16)
        var[1] = jnp.where(lo, 0.0, tr).astype(BF16)
        var[2] = jnp.where(lo, tr, 0.0).astype(BF16)
        var[3] = jnp.where(lo, 0.0, t).astype(BF16)

    qi = lax.broadcasted_iota(jnp.int32, (BLOCK, 2 * BLOCK), 0)
    kj = lax.broadcasted_iota(jnp.int32, (BLOCK, 2 * BLOCK), 1)
    diff = qi + BLOCK - kj
    band = (diff >= 0) & (diff < BLOCK)

    for p in range(N_Q_HEADS // 2):
        q_slab = (rope(proj(p * LANES, (p + 1) * LANES)) * (HEAD_DIM ** -0.5)).astype(BF16)
        for blk in range(nblk):
            first = (step * nblk + blk) % (tiles_per_seq * nblk) == 0
            kmin = jnp.where(first, BLOCK, 0)
            mask = band & (kj >= kmin)
            qs = q_slab[blk * BLOCK:(blk + 1) * BLOCK, :]
            o = jnp.zeros((BLOCK, LANES), F32)
            for hh in range(2):
                h = 2 * p + hh
                var = (h // (N_Q_HEADS // N_KV_HEADS)) * 2 + hh
                kc = kvar[var, blk * BLOCK:(blk + 2) * BLOCK, :]
                vc = vvar[var, blk * BLOCK:(blk + 2) * BLOCK, :]
                s = jnp.where(mask, _dot_nt(qs, kc), NEG_INF)
                sink = sinks_ref[h]
                m = jnp.maximum(jnp.max(s, axis=-1, keepdims=True), sink)
                pe = jnp.exp(s - m)
                den = jnp.sum(pe, axis=-1, keepdims=True) + jnp.exp(sink - m)
                o = o + _dot((pe / den).astype(BF16), vc)
            ybuf[blk * BLOCK:(blk + 1) * BLOCK, p * LANES:(p + 1) * LANES] = o.astype(BF16)

    u = proj(_O_GC, _O_XC) * proj(_O_XC, _O_END)
    ubuf[8:8 + tm, :] = u
    um1 = ubuf[7:7 + tm, :]
    um2 = ubuf[6:6 + tm, :]
    cw = conv_w_ref[...]
    yc = proj(_O_GB, _O_GC) * (cw[0:1, :] * um2 + cw[1:2, :] * um1 + cw[2:3, :] * u)
    ybuf[:, ATTN_WIDTH:] = yc.astype(BF16)

    ubuf[0:8, :] = ubuf[tm:tm + 8, :]
    kbuf[0:BLOCK, :] = kbuf[tm:tm + BLOCK, :]
    vbuf[0:BLOCK, :] = vbuf[tm:tm + BLOCK, :]

    mix = _dot(ybuf[...], w_out_ref[...]) + b_out_ref[...]
    h1 = _layer_norm(DN_ALPHA * x + mix, g1_ref[...], be1_ref[...])
    h1_ref[...] = h1
    h1t_ref[...] = h1.T.astype(BF16)

    qp = _dot(h1.astype(BF16), w_pq_ref[...]).astype(BF16)
    k1 = k1_ref[...]
    k2 = k2_ref[...]
    for h in range(PEER_HEADS):
        base = h * PEER_QDIM
        s1_ref[h] = _dot_nt(k1, qp[:, base:base + PEER_HALF])
        s2_ref[h] = _dot_nt(k2, qp[:, base + PEER_HALF:base + PEER_QDIM])


def _front(xf, rope, sinks, w_in, b_in, conv_w, w_out, b_out, g1, be1, w_pq, k1, k2,
           *, seq, tm):
    T = xf.shape[0]
    proj_w = w_in.shape[1]
    tiles_per_seq = seq // tm
    full = lambda shape: pl.BlockSpec(shape, lambda i: (0,) * len(shape))
    tok = lambda w: pl.BlockSpec((tm, w), lambda i: (i, 0))
    kern = functools.partial(_front_kernel, tm=tm, tiles_per_seq=tiles_per_seq)
    return pl.pallas_call(
        kern,
        grid=(T // tm,),
        in_specs=[
            pl.BlockSpec(memory_space=pltpu.SMEM),
            tok(D_MODEL), tok(LANES),
            full((D_MODEL, proj_w)), full((1, proj_w)),
            full((CONV_K, CONV_WIDTH)), full((D_MODEL, D_MODEL)), full((1, D_MODEL)),
            full((1, D_MODEL)), full((1, D_MODEL)),
            full((D_MODEL, PEER_HEADS * PEER_QDIM)),
            full((N_KEYS, PEER_HALF)), full((N_KEYS, PEER_HALF)),
        ],
        out_specs=[
            tok(D_MODEL),
            pl.BlockSpec((D_MODEL, tm), lambda i: (0, i)),
            pl.BlockSpec((PEER_HEADS, N_KEYS, tm), lambda i: (0, 0, i)),
            pl.BlockSpec((PEER_HEADS, N_KEYS, tm), lambda i: (0, 0, i)),
        ],
        out_shape=[
            jax.ShapeDtypeStruct((T, D_MODEL), F32),
            jax.ShapeDtypeStruct((D_MODEL, T), BF16),
            jax.ShapeDtypeStruct((PEER_HEADS, N_KEYS, T), F32),
            jax.ShapeDtypeStruct((PEER_HEADS, N_KEYS, T), F32),
        ],
        scratch_shapes=[
            pltpu.VMEM((tm + BLOCK, KV_WIDTH), F32),
            pltpu.VMEM((tm + BLOCK, KV_WIDTH), F32),
            pltpu.VMEM((tm + 8, CONV_WIDTH), F32),
            pltpu.VMEM((4, tm + BLOCK, KV_WIDTH), BF16),
            pltpu.VMEM((4, tm + BLOCK, KV_WIDTH), BF16),
            pltpu.VMEM((tm, D_MODEL), BF16),
        ],
        compiler_params=pltpu.CompilerParams(
            dimension_semantics=("arbitrary",), vmem_limit_bytes=VMEM_LIMIT),
        name="front",
    )(sinks, xf, rope, w_in, b_in, conv_w, w_out, b_out, g1, be1, w_pq, k1, k2)


def _top16(s):
    rows = lax.broadcasted_iota(jnp.int32, s.shape, 0)
    rem = s
    rank = jnp.full(s.shape, float(PEER_TOPK), F32)
    vals = []
    for k in range(PEER_TOPK):
        m = jnp.max(rem, axis=0, keepdims=True)
        idx = jnp.min(jnp.where(rem == m, rows, N_KEYS), axis=0, keepdims=True)
        sel = rows == idx
        rem = jnp.where(sel, NEG_INF, rem)
        rank = jnp.where(sel, float(k), rank)
        vals.append(m)
    return vals, rank


_N_CAND_ROWS = PEER_TOPK + 7 * 8 + 8


def _cand_positions():
    import numpy as np
    pos = np.full((_N_CAND_ROWS,), -1, np.int32)
    for r in range(_N_CAND_ROWS):
        if r < 16:
            a, b = 0, r
        elif r < 72:
            a, b = 1 + (r - 16) // 8, (r - 16) % 8
        else:
            a, b = 8 + (r - 72), 0
        if (a + 1) * (b + 1) <= PEER_TOPK:
            pos[r] = a * PEER_TOPK + b
    return pos


def _select_exact(s1, s2, pos, valid, row16):
    v1, rank1 = _top16(s1)
    v2, rank2 = _top16(s2)

    v1m = jnp.zeros((PEER_TOPK, LANES), F32)
    v2m = jnp.zeros((PEER_TOPK, LANES), F32)
    for a in range(PEER_TOPK):
        v1m = jnp.where(row16 == a, v1[a], v1m)
        v2m = jnp.where(row16 == a, v2[a], v2m)
    blocks = [v1[0] + v2m]
    for a in range(1, 8):
        blocks.append(v1[a] + v2m[0:8, :])
    blocks.append(v1m[8:16, :] + v2[0])
    cand = jnp.where(valid, jnp.concatenate(blocks, axis=0), NEG_INF)

    rem = cand
    chosen = jnp.zeros(cand.shape, F32)
    for _ in range(PEER_TOPK):
        m = jnp.max(rem, axis=0, keepdims=True)
        idx = jnp.min(jnp.where(rem == m, pos, 1 << 20), axis=0, keepdims=True)
        sel = (pos == idx) & valid
        rem = jnp.where(sel, NEG_INF, rem)
        chosen = jnp.where(sel, 1.0, chosen)

    top = v1[0] + v2[0]
    z = jnp.sum(chosen * jnp.exp(jnp.where(valid, cand, top) - top), axis=0, keepdims=True)

    counts = [jnp.sum(chosen[0:16, :], axis=0, keepdims=True)]
    for a in range(1, 8):
        counts.append(jnp.sum(chosen[8 + 8 * a:16 + 8 * a, :], axis=0, keepdims=True))
    for a in range(8, 16):
        counts.append(chosen[64 + a:65 + a, :])
    n1 = jnp.zeros(s1.shape, F32)
    for a in range(PEER_TOPK):
        n1 = jnp.where(rank1 == float(a), counts[a], n1)
    return n1, jnp.exp(s1 - v1[0]), rank2, jnp.exp(s2 - v2[0]) * (GELU_GATE / z)


def _sort_pairs(n):
    pairs = []
    p = 1
    while p < n:
        k = p
        while k >= 1:
            for j in range(k % p, n - k, 2 * k):
                for i in range(min(k, n - j - k)):
                    if (i + j) // (2 * p) == (i + j + k) // (2 * p):
                        pairs.append((i + j, i + j + k))
            k //= 2
        p *= 2
    return pairs


def _hi_lo(a, b):
    if a is None:
        return b, None
    if b is None:
        return a, None
    return jnp.maximum(a, b), jnp.minimum(a, b)


def _sublane_all(x, op):
    for shift in (4, 2, 1):
        x = op(x, pltpu.roll(x, shift, 0))
    return x


def _top16_sorted(x):
    x = list(x)
    for i, j in _sort_pairs(len(x)):
        x[i], x[j] = _hi_lo(x[i], x[j])
    for shift in (4, 2, 1):
        y = [None if t is None else pltpu.roll(t, shift, 0) for t in x]
        c = [_hi_lo(x[r], y[PEER_TOPK - 1 - r])[0] for r in range(PEER_TOPK)]
        d = PEER_TOPK // 2
        while d >= 1:
            for i in range(PEER_TOPK):
                if not i & d:
                    c[i], c[i + d] = _hi_lo(c[i], c[i + d])
            d //= 2
        x = c
    return x


def _select_fast(s1, s2):
    nv = N_KEYS // 8
    t1 = [s1[8 * v:8 * v + 8, :] for v in range(nv)]
    t2 = [s2[8 * v:8 * v + 8, :] for v in range(nv)]
    v1 = _top16_sorted(t1)
    v2 = _top16_sorted(t2)
    sub = lax.broadcasted_iota(jnp.int32, (8, LANES), 0)
    one = jnp.ones((8, LANES), F32)
    zero = jnp.zeros((8, LANES), F32)

    def spread(vals):
        out = vals[0]
        for r in range(1, 8):
            out = jnp.where(sub == r, vals[r], out)
        return out

    v2lo, v2hi, v1hi = spread(v2[0:8]), spread(v2[8:16]), spread(v1[8:16])
    cand = [v1[0] + v2lo, v1[0] + v2hi]
    for a in range(1, 8):
        cand.append(jnp.where(sub < PEER_TOPK // (a + 1), v1[a] + v2lo, NEG_INF))
    cand.append(v1hi + v2[0])
    best = _top16_sorted(cand + [None] * (PEER_TOPK - len(cand)))
    tau = best[PEER_TOPK - 1]
    z = one
    for k in range(1, PEER_TOPK):
        z = z + jnp.exp(best[k] - best[0])

    count = lambda t: _sublane_all(jnp.where(t >= tau, one, zero), jnp.add)
    counts = [count(cand[0]) + count(cand[1])]
    counts += [count(cand[a + 1]) for a in range(1, 8)]
    counts += [jnp.where(v1[a] + v2[0] >= tau, one, zero) for a in range(8, PEER_TOPK)]
    total = counts[0]
    for a in range(1, PEER_TOPK):
        total = total + counts[a]
    tie = jnp.where(total != float(PEER_TOPK), one, zero)

    n1 = [zero] * nv
    rank2 = [jnp.full((8, LANES), float(PEER_TOPK), F32)] * nv
    for a in reversed(range(PEER_TOPK)):
        n1 = [jnp.where(t >= v1[a], counts[a], n) for t, n in zip(t1, n1)]
        rank2 = [jnp.where(t >= v2[a], float(a), r) for t, r in zip(t2, rank2)]
    for v, tiles in ((v1, t1), (v2, t2)):
        inside = zero
        for t in tiles:
            inside = inside + jnp.where(t >= v[PEER_TOPK - 1], one, zero)
        tie = jnp.where(_sublane_all(inside, jnp.add) != float(PEER_TOPK), one, tie)
        for a in range(PEER_TOPK - 1):
            tie = jnp.where(v[a] == v[a + 1], one, tie)

    inv_z = GELU_GATE / z
    cat = lambda tiles: jnp.concatenate(tiles, axis=0)
    e1 = cat([jnp.exp(t - v1[0]) for t in t1])
    e2 = cat([jnp.exp(t - v2[0]) * inv_z for t in t2])
    return cat(n1), e1, cat(rank2), e2, tie


def _select_kernel(pos_ref, s1_ref, s2_ref, r2_ref, e2_ref, n1_ref, e1_ref, *, ts):
    def lanes(g):
        return pl.ds(pl.multiple_of(g * LANES, LANES), LANES)

    def finish(sl, fast):
        n1, e1, rank2, e2, tie = fast

        def exact():
            pos = pos_ref[...]
            row16 = lax.broadcasted_iota(jnp.int32, (PEER_TOPK, LANES), 0)
            return _select_exact(s1_ref[0, :, sl], s2_ref[0, :, sl], pos, pos >= 0, row16)

        n1, e1, rank2, e2 = lax.cond(jnp.max(tie) > 0.0, exact, lambda: (n1, e1, rank2, e2))
        n1_ref[:, 0, :, sl] = n1.reshape(N_KEYS // 8, 8, LANES)
        e1_ref[:, 0, :, sl] = e1.reshape(N_KEYS // 8, 8, LANES)
        r2_ref[0, :, sl] = rank2.astype(BF16)
        e2_ref[0, :, sl] = e2.astype(BF16)

    def trip(i, carry):
        sls = [lanes(SELECT_UNROLL * i + k) for k in range(SELECT_UNROLL)]
        fasts = [_select_fast(s1_ref[0, :, sl], s2_ref[0, :, sl]) for sl in sls]
        for sl, fast in zip(sls, fasts):
            finish(sl, fast)
        return carry

    assert (ts // LANES) % SELECT_UNROLL == 0
    lax.fori_loop(0, ts // LANES // SELECT_UNROLL, trip, 0)


def _select(s1t, s2t, *, ts):
    H, K, T = s1t.shape
    import numpy as np
    pos = jnp.asarray(np.tile(_cand_positions()[:, None], (1, LANES)))
    blk = pl.BlockSpec((1, K, ts), lambda t, h: (h, 0, t))
    blk1 = pl.BlockSpec((K // 8, 1, 8, ts), lambda t, h: (0, h, 0, t))
    return pl.pallas_call(
        functools.partial(_select_kernel, ts=ts),
        grid=(T // ts, H),
        in_specs=[pl.BlockSpec((_N_CAND_ROWS, LANES), lambda t, h: (0, 0)), blk, blk],
        out_specs=[blk, blk, blk1, blk1],
        out_shape=[
            jax.ShapeDtypeStruct((H, K, T), BF16),
            jax.ShapeDtypeStruct((H, K, T), BF16),
            jax.ShapeDtypeStruct((K // 8, H, 8, T), F32),
            jax.ShapeDtypeStruct((K // 8, H, 8, T), F32),
        ],
        compiler_params=pltpu.CompilerParams(
            dimension_semantics=("arbitrary", "arbitrary"), vmem_limit_bytes=VMEM_LIMIT),
        name="select",
    )(pos, s1t, s2t)


def _peer_kernel(u_ref, vt_ref, xt_ref, r2_ref, e2_ref, n1_ref, e1_ref, h1_ref, g2_ref, be2_ref,
                 out_ref, acc_ref, w_ref, a_ref, *, ec, tt):
    c = pl.program_id(1)
    nkb = ec // N_KEYS
    ngrp = nkb // 8
    assert ngrp * 8 == nkb

    @pl.when(c == 0)
    def _():
        acc_ref[...] = jnp.zeros(acc_ref.shape, F32)

    def row_bf16(ref, h, ib, sl):
        tiles = [jnp.broadcast_to(
            ref[c * ngrp + ib // 8, h, ib % 8:ib % 8 + 1, sl.start + k * LANES:sl.start + (k + 1) * LANES],
            (BF16_ROWS, LANES)) for k in range(BF16_COLS // LANES)]
        return jnp.concatenate(tiles, axis=1).astype(BF16)

    rows = BF16_ROWS
    a_ref[...] = _dot(u_ref[...], xt_ref[...])
    for nt in range(tt // BF16_COLS):
        nsl = slice(nt * BF16_COLS, (nt + 1) * BF16_COLS)
        for ib in range(nkb):
            n1b = [row_bf16(n1_ref, h, ib, nsl) for h in range(PEER_HEADS)]
            e1b = [row_bf16(e1_ref, h, ib, nsl) for h in range(PEER_HEADS)]
            for rg in range(N_KEYS // rows):
                ksl = slice(rg * rows, (rg + 1) * rows)
                g = None
                for h in range(PEER_HEADS):
                    e2 = e2_ref[h, ksl, nsl]
                    term = jnp.where(r2_ref[h, ksl, nsl] < n1b[h], e2, jnp.zeros_like(e2)) * e1b[h]
                    g = term if g is None else g + term
                esl = slice(ib * N_KEYS + rg * rows, ib * N_KEYS + (rg + 1) * rows)
                ab = a_ref[esl, nsl]
                ge = ab * (1.0 + lax.erf(ab))
                w_ref[esl, nsl] = ge.astype(BF16) * g
    acc_ref[...] += _dot(vt_ref[...], w_ref[...])

    @pl.when(c == pl.num_programs(1) - 1)
    def _():
        z = DN_ALPHA * h1_ref[...] + acc_ref[...].T
        out_ref[...] = _layer_norm(z, g2_ref[...], be2_ref[...])


def _peer(u_bf, vt_bf, h1t, r2, e2, n1, e1, h1, g2, be2, *, ec, tt):
    E = u_bf.shape[0]
    T = h1.shape[0]
    tokrows = pl.BlockSpec((PEER_HEADS, N_KEYS, tt), lambda t, c: (0, 0, t))
    tokkeys = pl.BlockSpec((N_KEYS // 8, PEER_HEADS, 8, tt), lambda t, c: (0, 0, 0, t))
    vec = pl.BlockSpec((1, D_MODEL), lambda t, c: (0, 0))
    return pl.pallas_call(
        functools.partial(_peer_kernel, ec=ec, tt=tt),
        grid=(T // tt, E // ec),
        in_specs=[
            pl.BlockSpec((ec, D_MODEL), lambda t, c: (c, 0)),
            pl.BlockSpec((D_MODEL, ec), lambda t, c: (0, c)),
            pl.BlockSpec((D_MODEL, tt), lambda t, c: (0, t)),
            tokrows, tokrows, tokkeys, tokkeys,
            pl.BlockSpec((tt, D_MODEL), lambda t, c: (t, 0)),
            vec, vec,
        ],
        out_specs=pl.BlockSpec((tt, D_MODEL), lambda t, c: (t, 0)),
        out_shape=jax.ShapeDtypeStruct((T, D_MODEL), F32),
        scratch_shapes=[
            pltpu.VMEM((D_MODEL, tt), F32),
            pltpu.VMEM((ec, tt), BF16),
            pltpu.VMEM((ec, tt), F32),
        ],
        compiler_params=pltpu.CompilerParams(
            dimension_semantics=("arbitrary", "arbitrary"), vmem_limit_bytes=VMEM_LIMIT),
        name="peer",
    )(u_bf, vt_bf, h1t, r2, e2, n1, e1, h1, g2, be2)


def _rope_table(positions):
    T = positions.size
    inv_freq = ROPE_THETA ** (-jnp.arange(0, ROT_DIM, 2, dtype=F32) / ROT_DIM)
    ang = positions.reshape(T, 1).astype(F32) * inv_freq
    head = jnp.concatenate([jnp.cos(ang), jnp.sin(ang), jnp.zeros((T, HEAD_DIM - ROT_DIM), F32)],
                           axis=1)
    return jnp.tile(head, (1, LANES // HEAD_DIM))


def _tiles(seq, total_tokens):
    tm = next(c for c in (512, 256, BLOCK) if seq % c == 0)
    ts = next(c for c in (1024, 512, LANES) if total_tokens % c == 0)
    tt = 512 if total_tokens % 512 == 0 else 256
    ec = 16 * N_KEYS
    return tm, ts, tt, ec


def kernel(x, positions, w_in, b_in, attn_sinks, conv_w, w_out, b_out, ln1_g, ln1_b,
           w_pq, sub_keys1, sub_keys2, u_experts, v_experts, ln2_g, ln2_b):
    B, S, D = x.shape
    T = B * S
    assert D == D_MODEL and S % BLOCK == 0 and T % 256 == 0
    assert w_in.shape[0] == DEPTH
    tm, ts, tt, ec = _tiles(S, T)
    row = lambda v: v.reshape(1, -1).astype(F32)

    h1, h1t, s1t, s2t = _front(
        x.reshape(T, D), _rope_table(positions), attn_sinks[0].astype(F32),
        w_in[0].astype(BF16), row(b_in[0]), conv_w[0].astype(F32),
        w_out[0].astype(BF16), row(b_out[0]), row(ln1_g[0]), row(ln1_b[0]),
        w_pq[0].astype(BF16), sub_keys1[0].astype(BF16), sub_keys2[0].astype(BF16),
        seq=S, tm=tm)
    r2, e2, n1, e1 = _select(s1t, s2t, ts=ts)
    out = _peer((u_experts[0] * GELU_IN).astype(BF16), v_experts[0].T.astype(BF16), h1t,
                r2, e2, n1, e1, h1, row(ln2_g[0]), row(ln2_b[0]), ec=ec, tt=tt)
    return out.reshape(B, S, D)
```
